```python
import functools
import jax, jax.numpy as jnp
from jax import lax
import numpy as np

D_MODEL = 2048
BATCH = 1
SEQ = 8192
DEPTH = 2
DEC_BATCH = 128
DEC_SEQ = 1
PAST_LEN = 8192
PAGE_SIZE = 128

SSM_EXPAND = 2
D_INNER = SSM_EXPAND * D_MODEL
SSM_HEADDIM = 64
N_SSM_HEADS = D_INNER // SSM_HEADDIM
D_STATE = 128
N_GROUPS = 8
HEADS_PER_GROUP = N_SSM_HEADS // N_GROUPS
CONV_W = 4
CONV_DIM = D_INNER + 2 * N_GROUPS * D_STATE
SSD_CHUNK = 128
HEAD_DIM = 64
N_HEADS = D_MODEL // HEAD_DIM
N_KV_HEADS = 8
Q_PER_KV = N_HEADS // N_KV_HEADS
ATTN_DIM = N_HEADS * HEAD_DIM
KV_DIM = N_KV_HEADS * HEAD_DIM
WINDOW = 128
ATTN_BLOCK = WINDOW
D_FF = ((8 * D_MODEL // 3 + 255) // 256) * 256
IN_SPLITS = (D_INNER, CONV_DIM, N_SSM_HEADS, ATTN_DIM, KV_DIM, KV_DIM, D_MODEL, D_MODEL)
IN_DIM = sum(IN_SPLITS)
EPS = 1e-6

kernel_name = "hybrid_ssd_swa_sink_gated_decoder_step"


def split_cols(t, sizes):
    out, start = [], 0
    for s in sizes:
        out.append(t[..., start:start + s])
        start += s
    return out


def rmsnorm(x, g):
    x32 = x.astype(jnp.float32)
    y = x32 * lax.rsqrt(jnp.mean(x32 * x32, axis=-1, keepdims=True) + EPS)
    return (y * g.astype(jnp.float32)).astype(x.dtype)


def gated_rmsnorm(y, z, g):
    h = (y.astype(jnp.float32) * jax.nn.silu(z.astype(jnp.float32)))
    hg = h.reshape(h.shape[:-1] + (N_GROUPS, D_INNER // N_GROUPS))
    hg = hg * lax.rsqrt(jnp.mean(hg * hg, axis=-1, keepdims=True) + EPS)
    return (hg.reshape(h.shape) * g.astype(jnp.float32)).astype(y.dtype)


def causal_conv(xbc, buf, w, b):
    L = xbc.shape[1]
    xp = jnp.concatenate([buf.astype(xbc.dtype), xbc], axis=1)
    out = b + sum(xp[:, k:k + L] * w[k] for k in range(CONV_W))
    return jax.nn.silu(out), xp[:, -(CONV_W - 1):]


def ssd_scan(x, dt, A, B, C, h0):
    b, L = x.shape[:2]
    cl = SSD_CHUNK if L >= SSD_CHUNK else L
    nc = -(-L // cl)
    pad = nc * cl - L
    if pad:
        padw = lambda t: jnp.pad(t, [(0, 0), (0, pad)] + [(0, 0)] * (t.ndim - 2))
        x, dt, B, C = padw(x), padw(dt), padw(B), padw(C)
    G, R = N_GROUPS, HEADS_PER_GROUP
    x = x.reshape(b, nc, cl, G, R, SSM_HEADDIM)
    dt = dt.reshape(b, nc, cl, G, R)
    B = B.reshape(b, nc, cl, G, D_STATE)
    C = C.reshape(b, nc, cl, G, D_STATE)
    a = dt * A.reshape(G, R)
    a_cs = jnp.cumsum(a, axis=2)
    xdt = x * dt[..., None]
    causal = jnp.tril(jnp.ones((cl, cl), dtype=bool))
    seg = a_cs[:, :, :, None] - a_cs[:, :, None, :]
    decay = jnp.exp(jnp.where(causal[None, None, :, :, None, None], seg, -jnp.inf))
    cb = jnp.einsum("bclgn,bcsgn->bclsg", C, B)
    y_diag = jnp.einsum("bclsgr,bcsgrp->bclgrp", cb[..., None] * decay, xdt)
    to_end = jnp.exp(a_cs[:, :, -1:] - a_cs)
    states = jnp.einsum("bclgn,bclgrp->bcgrpn", B, xdt * to_end[..., None])
    chunk_decay = jnp.exp(a_cs[:, :, -1])

    def step(h, inp):
        dec, st = inp
        return dec[..., None, None] * h + st, h

    h_final, h_start = lax.scan(
        step, h0.reshape(b, G, R, SSM_HEADDIM, D_STATE),
        (jnp.moveaxis(chunk_decay, 1, 0), jnp.moveaxis(states, 1, 0)))
    h_start = jnp.moveaxis(h_start, 0, 1)
    y_off = jnp.einsum("bclgn,bcgrpn->bclgrp", C, h_start) * jnp.exp(a_cs)[..., None]
    y = (y_diag + y_off).reshape(b, nc * cl, N_SSM_HEADS, SSM_HEADDIM)[:, :L]
    return y, h_final.reshape(b, N_SSM_HEADS, SSM_HEADDIM, D_STATE)


def mamba_branch(z, xbc, dt_raw, conv_buf, h0, p):
    b, L = z.shape[:2]
    xbc, buf_new = causal_conv(xbc, conv_buf, p["conv_w"], p["conv_b"])
    xs, Bm, Cm = split_cols(xbc, (D_INNER, N_GROUPS * D_STATE, N_GROUPS * D_STATE))
    xs = xs.reshape(b, L, N_SSM_HEADS, SSM_HEADDIM).astype(jnp.float32)
    Bm = Bm.reshape(b, L, N_GROUPS, D_STATE).astype(jnp.float32)
    Cm = Cm.reshape(b, L, N_GROUPS, D_STATE).astype(jnp.float32)
    dt = jax.nn.softplus(dt_raw.astype(jnp.float32) + p["dt_bias"].astype(jnp.float32))
    A = -jnp.exp(p["a_log"].astype(jnp.float32))
    y, h_new = ssd_scan(xs, dt, A, Bm, Cm, h0.astype(jnp.float32))
    y = y + p["d_skip"].astype(jnp.float32)[:, None] * xs
    y = y.reshape(b, L, D_INNER).astype(z.dtype)
    return gated_rmsnorm(y, z, p["ssm_norm"]), h_new.astype(h0.dtype), buf_new


def sink_attend(q, k, v, q_pos, k_pos, sinks):
    s = jnp.einsum("bnqkgd,bnskd->bnkgqs", q, k).astype(jnp.float32) * (HEAD_DIM ** -0.5)
    kp, qp = k_pos[:, None, :], q_pos[:, :, None]
    mask = (kp <= qp) & (kp >= qp - WINDOW) & (kp >= 0)
    s = jnp.where(mask[None, :, None, None], s, -jnp.inf)
    sink = sinks.astype(jnp.float32).reshape(1, 1, N_KV_HEADS, Q_PER_KV, 1, 1)
    m = jnp.maximum(jnp.max(s, axis=-1, keepdims=True), sink)
    pr = jnp.exp(s - m)
    pr = pr / (jnp.sum(pr, axis=-1, keepdims=True) + jnp.exp(sink - m))
    return jnp.einsum("bnkgqs,bnskd->bnqkgd", pr.astype(v.dtype), v)


def prompt_attention(q, k, v, sinks):
    b, L = q.shape[:2]
    nb = L // ATTN_BLOCK
    qb = q.reshape(b, nb, ATTN_BLOCK, N_KV_HEADS, Q_PER_KV, HEAD_DIM)

    def band(t):
        tp = jnp.concatenate([jnp.zeros((b, WINDOW, N_KV_HEADS, HEAD_DIM), t.dtype), t], axis=1)
        tp = tp.reshape(b, nb + 1, ATTN_BLOCK, N_KV_HEADS, HEAD_DIM)
        return jnp.concatenate([tp[:, :-1], tp[:, 1:]], axis=2)

    q_pos = jnp.arange(L, dtype=jnp.int32).reshape(nb, ATTN_BLOCK)
    kp = jnp.arange(-WINDOW, L, dtype=jnp.int32).reshape(nb + 1, ATTN_BLOCK)
    k_pos = jnp.concatenate([kp[:-1], kp[1:]], axis=1)
    o = sink_attend(qb, band(k), band(v), q_pos, k_pos, sinks).reshape(b, L, ATTN_DIM)
    return o, k[:, L - WINDOW:], v[:, L - WINDOW:]


def sample_attention(q, k, v, sinks, cache_k, cache_v):
    b, L = q.shape[:2]
    keys = jnp.concatenate([cache_k.astype(k.dtype), k], axis=1)
    vals = jnp.concatenate([cache_v.astype(v.dtype), v], axis=1)
    q_pos = PAST_LEN + jnp.arange(L, dtype=jnp.int32)
    k_pos = jnp.concatenate([PAST_LEN - WINDOW + jnp.arange(WINDOW, dtype=jnp.int32), q_pos])
    qb = q.reshape(b, 1, L, N_KV_HEADS, Q_PER_KV, HEAD_DIM)
    o = sink_attend(qb, keys[:, None], vals[:, None], q_pos[None], k_pos[None], sinks)
    return o.reshape(b, L, ATTN_DIM), keys[:, -WINDOW:], vals[:, -WINDOW:]


def mixer_layer(x, p, conv_buf, h0, attend):
    b, L, _ = x.shape
    xn = rmsnorm(x, p["norm_mix"])
    z, xbc, dt_raw, q, k, v, g_ssm, g_attn = split_cols(xn @ p["w_in"], IN_SPLITS)
    y_ssm, h_new, buf_new = mamba_branch(z, xbc, dt_raw, conv_buf, h0, p)
    o, k_new, v_new = attend(q.reshape(b, L, N_HEADS, HEAD_DIM),
                             k.reshape(b, L, N_KV_HEADS, HEAD_DIM),
                             v.reshape(b, L, N_KV_HEADS, HEAD_DIM), p["attn_sinks"])
    merged = (jax.nn.sigmoid(g_ssm) * (y_ssm @ p["w_ssm_proj"])
              + jax.nn.sigmoid(g_attn) * (o @ p["w_attn_proj"]))
    return x + merged @ p["w_out"], h_new, buf_new, k_new, v_new


def swiglu_ffn(x, p):
    xn = rmsnorm(x, p["norm_ffn"])
    return x + (jax.nn.silu(xn @ p["w_gate"]) * (xn @ p["w_up"])) @ p["w_down"]


def setup_inputs(seed: int = 0) -> dict:
    key = jax.random.key(seed)
    ks = jax.random.split(key, 24)
    f32 = jnp.float32
    nrm = lambda k, shape, s=1.0: jax.random.normal(k, shape, f32) * s
    dt0 = jnp.exp(jax.random.uniform(ks[9], (DEPTH, N_SSM_HEADS), f32, np.log(1e-3), np.log(1e-1)))
    return {
        "x_prompt": nrm(ks[0], (BATCH, SEQ, D_MODEL)),
        "x_sample": nrm(ks[1], (DEC_BATCH, DEC_SEQ, D_MODEL)),
        "state_ssm": nrm(ks[2], (DEPTH, DEC_BATCH, N_SSM_HEADS, SSM_HEADDIM, D_STATE), 0.5),
        "state_conv": nrm(ks[3], (DEPTH, DEC_BATCH, CONV_W - 1, CONV_DIM)),
        "cache_k": nrm(ks[4], (DEPTH, DEC_BATCH, WINDOW, N_KV_HEADS, HEAD_DIM)),
        "cache_v": nrm(ks[5], (DEPTH, DEC_BATCH, WINDOW, N_KV_HEADS, HEAD_DIM)),
        "norm_mix": 1.0 + nrm(ks[6], (DEPTH, D_MODEL), 0.02),
        "w_in": nrm(ks[7], (DEPTH, D_MODEL, IN_DIM), D_MODEL ** -0.5),
        "conv_w": nrm(ks[8], (DEPTH, CONV_W, CONV_DIM), CONV_W ** -0.5),
        "conv_b": nrm(ks[10], (DEPTH, CONV_DIM), 0.02),
        "dt_bias": dt0 + jnp.log(-jnp.expm1(-dt0)),
        "a_log": jnp.log(jax.random.uniform(ks[11], (DEPTH, N_SSM_HEADS), f32, 1.0, 16.0)),
        "d_skip": 1.0 + nrm(ks[12], (DEPTH, N_SSM_HEADS), 0.1),
        "ssm_norm": 1.0 + nrm(ks[13], (DEPTH, D_INNER), 0.02),
        "w_ssm_proj": nrm(ks[14], (DEPTH, D_INNER, D_MODEL), D_INNER ** -0.5),
        "w_attn_proj": nrm(ks[15], (DEPTH, ATTN_DIM, D_MODEL), ATTN_DIM ** -0.5),
        "attn_sinks": nrm(ks[16], (DEPTH, N_HEADS)),
        "w_out": nrm(ks[17], (DEPTH, D_MODEL, D_MODEL), D_MODEL ** -0.5),
        "norm_ffn": 1.0 + nrm(ks[18], (DEPTH, D_MODEL), 0.02),
        "w_gate": nrm(ks[19], (DEPTH, D_MODEL, D_FF), D_MODEL ** -0.5),
        "w_up": nrm(ks[20], (DEPTH, D_MODEL, D_FF), D_MODEL ** -0.5),
        "w_down": nrm(ks[21], (DEPTH, D_FF, D_MODEL), D_FF ** -0.5),
        "norm_final": 1.0 + nrm(ks[22], (D_MODEL,), 0.02),
    }


def reference(x_prompt, x_sample, state_ssm, state_conv, cache_k, cache_v,
              norm_mix, w_in, conv_w, conv_b, dt_bias, a_log, d_skip, ssm_norm,
              w_ssm_proj, w_attn_proj, attn_sinks, w_out, norm_ffn, w_gate, w_up,
              w_down, norm_final):
    yp, ys = x_prompt, x_sample
    bp = x_prompt.shape[0]
    p_ssm, p_conv, p_k, p_v = [], [], [], []
    s_ssm, s_conv, s_k, s_v = [], [], [], []
    for l in range(DEPTH):
        p = {
            "norm_mix": norm_mix[l], "w_in": w_in[l], "conv_w": conv_w[l], "conv_b": conv_b[l],
            "dt_bias": dt_bias[l], "a_log": a_log[l], "d_skip": d_skip[l], "ssm_norm": ssm_norm[l],
            "w_ssm_proj": w_ssm_proj[l], "w_attn_proj": w_attn_proj[l], "attn_sinks": attn_sinks[l],
            "w_out": w_out[l], "norm_ffn": norm_ffn[l], "w_gate": w_gate[l], "w_up": w_up[l],
            "w_down": w_down[l],
        }
        conv0 = jnp.zeros((bp, CONV_W - 1, CONV_DIM), x_prompt.dtype)
        h0 = jnp.zeros((bp, N_SSM_HEADS, SSM_HEADDIM, D_STATE), x_prompt.dtype)
        yp, h_p, buf_p, k_pn, v_pn = mixer_layer(yp, p, conv0, h0, prompt_attention)
        yp = swiglu_ffn(yp, p)
        attend_s = functools.partial(sample_attention, cache_k=cache_k[l], cache_v=cache_v[l])
        ys, h_s, buf_s, k_sn, v_sn = mixer_layer(ys, p, state_conv[l], state_ssm[l], attend_s)
        ys = swiglu_ffn(ys, p)
        p_ssm.append(h_p); p_conv.append(buf_p); p_k.append(k_pn); p_v.append(v_pn)
        s_ssm.append(h_s); s_conv.append(buf_s); s_k.append(k_sn); s_v.append(v_sn)
    y_prompt = rmsnorm(yp, norm_final)
    y_sample = rmsnorm(ys, norm_final)
    return (y_prompt, y_sample,
            jnp.stack(p_ssm), jnp.stack(p_conv), jnp.stack(p_k), jnp.stack(p_v),
            jnp.stack(s_ssm), jnp.stack(s_conv), jnp.stack(s_k), jnp.stack(s_v))
```

```python
import functools

import jax
import jax.numpy as jnp
from jax import lax
from jax.experimental import pallas as pl
from jax.experimental.pallas import tpu as pltpu

F32 = jnp.float32
BF16 = jnp.bfloat16

D_MODEL = 2048
SEQ = 8192
DEPTH = 2
DEC_BATCH = 128
D_INNER = 4096
HEADDIM = 64
N_SSM_HEADS = 64
D_STATE = 128
N_GROUPS = 8
GROUP_W = D_INNER // N_GROUPS
CONV_W = 4
BC_W = N_GROUPS * D_STATE
CONV_DIM = D_INNER + 2 * BC_W
CHUNK = 128
N_HEADS = 32
N_KV = 8
Q_PER_KV = 4
KV_DIM = N_KV * HEADDIM
WINDOW = 128
D_FF = 5632
EPS = 1e-6
IN_DIM = 17472

OFF_Z = 0
OFF_X = 4096
OFF_B = OFF_X + D_INNER
OFF_C = OFF_B + BC_W
OFF_Q = 10240
OFF_K = 12288
OFF_V = 12800
OFF_GS = 13312
OFF_GA = 15360
OFF_DT = 17408
PROJ_W = 17920

LANES = 128
SUBLANES = 8
VMEM_LIMIT = 56 * 1024 * 1024


def _cparams(*sem):
    return pltpu.CompilerParams(dimension_semantics=sem, vmem_limit_bytes=VMEM_LIMIT)


def _dot(a, b):
    return jnp.dot(a, b, preferred_element_type=F32)


def _dot_nt(a, b):
    return lax.dot_general(a, b, (((1,), (1,)), ((), ())), preferred_element_type=F32)


def _div_pow2(v, d):
    return jnp.right_shift(v, d.bit_length() - 1)


def _mod_pow2(v, d):
    return jnp.bitwise_and(v, d - 1)


def _split3(x):
    hi = x.astype(BF16)
    r1 = x - hi.astype(F32)
    mid = r1.astype(BF16)
    lo = (r1 - mid.astype(F32)).astype(BF16)
    return hi, mid, lo


def _dot3_r(x, sel):
    hi, mid, lo = _split3(x)
    return _dot(hi, sel) + _dot(mid, sel) + _dot(lo, sel)


def _dot3_l(sel, x):
    hi, mid, lo = _split3(x)
    return _dot(sel, hi) + _dot(sel, mid) + _dot(sel, lo)


def _silu(x):
    return x * jax.nn.sigmoid(x)


def _softplus(x):
    return jnp.maximum(x, 0.0) + jnp.log1p(jnp.exp(-jnp.abs(x)))


def _rms_scale(x, g):
    r = lax.rsqrt(jnp.mean(x * x, axis=-1, keepdims=True) + EPS)
    return x * r * g


def _norm_matmul_kernel(x_ref, g_ref, w_ref, o_ref, xn_ref):
    @pl.when(pl.program_id(1) == 0)
    def _():
        xn_ref[...] = _rms_scale(x_ref[...], g_ref[...]).astype(BF16)

    o_ref[...] = _dot(xn_ref[...], w_ref[...])


def _in_proj(x, g, w, tm, tn):
    m, k = x.shape
    n = w.shape[1]
    return pl.pallas_call(
        _norm_matmul_kernel,
        grid=(m // tm, n // tn),
        in_specs=[pl.BlockSpec((tm, k), lambda i, j: (i, 0)),
                  pl.BlockSpec((1, k), lambda i, j: (0, 0)),
                  pl.BlockSpec((k, tn), lambda i, j: (0, j))],
        out_specs=pl.BlockSpec((tm, tn), lambda i, j: (i, j)),
        out_shape=jax.ShapeDtypeStruct((m, n), F32),
        scratch_shapes=[pltpu.VMEM((tm, k), BF16)],
        compiler_params=_cparams("parallel", "arbitrary"),
        name="in_proj",
    )(x, g, w)


def _ffn_up_kernel(x_ref, g_ref, wg_ref, wu_ref, o_ref, xn_ref):
    @pl.when(pl.program_id(1) == 0)
    def _():
        xn_ref[...] = _rms_scale(x_ref[...], g_ref[...]).astype(BF16)

    xn = xn_ref[...]
    o_ref[...] = (_silu(_dot(xn, wg_ref[...])) * _dot(xn, wu_ref[...])).astype(BF16)


def _ffn_up(x, g, wg, wu, tm, tn):
    m, k = x.shape
    n = wg.shape[1]
    return pl.pallas_call(
        _ffn_up_kernel,
        grid=(m // tm, n // tn),
        in_specs=[pl.BlockSpec((tm, k), lambda i, j: (i, 0)),
                  pl.BlockSpec((1, k), lambda i, j: (0, 0)),
                  pl.BlockSpec((k, tn), lambda i, j: (0, j)),
                  pl.BlockSpec((k, tn), lambda i, j: (0, j))],
        out_specs=pl.BlockSpec((tm, tn), lambda i, j: (i, j)),
        out_shape=jax.ShapeDtypeStruct((m, n), BF16),
        scratch_shapes=[pltpu.VMEM((tm, k), BF16)],
        compiler_params=_cparams("parallel", "arbitrary"),
        name="ffn_up",
    )(x, g, wg, wu)


def _matmul_res_kernel(a_ref, w_ref, r_ref, o_ref):
    o_ref[...] = r_ref[...] + _dot(a_ref[...], w_ref[...])


def _matmul_res(a, w, res, tm, tn, name):
    m, k = a.shape
    n = w.shape[1]
    return pl.pallas_call(
        _matmul_res_kernel,
        grid=(m // tm, n // tn),
        in_specs=[pl.BlockSpec((tm, k), lambda i, j: (i, 0)),
                  pl.BlockSpec((k, tn), lambda i, j: (0, j)),
                  pl.BlockSpec((tm, tn), lambda i, j: (i, j))],
        out_specs=pl.BlockSpec((tm, tn), lambda i, j: (i, j)),
        out_shape=jax.ShapeDtypeStruct((m, n), F32),
        compiler_params=_cparams("parallel", "parallel"),
        name=name,
    )(a, w, res)


def _merge_kernel(y_ref, o_ref, gs_ref, ga_ref, ws_ref, wa_ref, out_ref):
    ys = _dot(y_ref[...].astype(BF16), ws_ref[...])
    oa = _dot(o_ref[...].astype(BF16), wa_ref[...])
    out_ref[...] = (jax.nn.sigmoid(gs_ref[...]) * ys
                    + jax.nn.sigmoid(ga_ref[...]) * oa).astype(BF16)


def _merge(y, o, proj, ws, wa, tm, tn):
    m = y.shape[0]
    gs0, ga0 = OFF_GS // tn, OFF_GA // tn
    return pl.pallas_call(
        _merge_kernel,
        grid=(m // tm, D_MODEL // tn),
        in_specs=[pl.BlockSpec((tm, D_INNER), lambda i, j: (i, 0)),
                  pl.BlockSpec((tm, D_MODEL), lambda i, j: (i, 0)),
                  pl.BlockSpec((tm, tn), lambda i, j: (i, gs0 + j)),
                  pl.BlockSpec((tm, tn), lambda i, j: (i, ga0 + j)),
                  pl.BlockSpec((D_INNER, tn), lambda i, j: (0, j)),
                  pl.BlockSpec((D_MODEL, tn), lambda i, j: (0, j))],
        out_specs=pl.BlockSpec((tm, tn), lambda i, j: (i, j)),
        out_shape=jax.ShapeDtypeStruct((m, D_MODEL), BF16),
        compiler_params=_cparams("parallel", "parallel"),
        name="merge",
    )(y, o, proj, proj, ws, wa)


def _rmsnorm_kernel(x_ref, g_ref, o_ref):
    o_ref[...] = _rms_scale(x_ref[...], g_ref[...])


def _rmsnorm(x, g, tm):
    m, k = x.shape
    return pl.pallas_call(
        _rmsnorm_kernel,
        grid=(m // tm,),
        in_specs=[pl.BlockSpec((tm, k), lambda i: (i, 0)),
                  pl.BlockSpec((1, k), lambda i: (0, 0))],
        out_specs=pl.BlockSpec((tm, k), lambda i: (i, 0)),
        out_shape=jax.ShapeDtypeStruct((m, k), F32),
        compiler_params=_cparams("parallel"),
        name="final_norm",
    )(x, g)


def _gated_norm(y, z, gain):
    h = y * _silu(z)
    r = lax.rsqrt(jnp.mean(h * h, axis=-1, keepdims=True) + EPS)
    return h * r * gain


def _ssd_kernel(xs_ref, b_ref, c_ref, z_ref, dt_ref, cw_ref, cb_ref, dtb_ref, alog_ref,
                dsk_ref, gn_ref, y_ref, hout_ref, cout_ref, xpad_ref, xc_ref, ht_ref):
    ci = pl.program_id(0)

    @pl.when(ci == 0)
    def _():
        xpad_ref[0:SUBLANES, :] = jnp.zeros((SUBLANES, CONV_DIM), F32)
        ht_ref[...] = jnp.zeros_like(ht_ref)

    xpad_ref[SUBLANES:SUBLANES + CHUNK, 0:D_INNER] = xs_ref[...]
    xpad_ref[SUBLANES:SUBLANES + CHUNK, D_INNER:D_INNER + BC_W] = b_ref[...]
    xpad_ref[SUBLANES:SUBLANES + CHUNK, D_INNER + BC_W:CONV_DIM] = c_ref[...]

    first = SUBLANES - (CONV_W - 1)
    for s in range(CONV_DIM // GROUP_W):
        sl = slice(s * GROUP_W, (s + 1) * GROUP_W)
        acc = cb_ref[:, sl]
        for k in range(CONV_W):
            acc = acc + xpad_ref[first + k:first + k + CHUNK, sl] * cw_ref[k:k + 1, sl]
        xc_ref[:, sl] = _silu(acc)

    tail = xpad_ref[CHUNK:CHUNK + SUBLANES, :]
    xpad_ref[0:SUBLANES, :] = tail
    cout_ref[...] = tail

    dt = _softplus(dt_ref[...] + dtb_ref[...])
    a = dt * (-jnp.exp(alog_ref[...]))
    row = lax.broadcasted_iota(jnp.int32, (CHUNK, CHUNK), 0)
    col = lax.broadcasted_iota(jnp.int32, (CHUNK, CHUNK), 1)
    causal = col <= row
    tri = jnp.where(causal, 1.0, 0.0).astype(BF16)
    a_cs = _dot3_l(tri, a)
    a_cs_t = a_cs.T
    ea = jnp.exp(a_cs)
    to_end = jnp.exp(a_cs[CHUNK - 1:CHUNK, :] - a_cs)
    stack = jnp.concatenate([dt, to_end, ea], axis=0)

    lane = lax.broadcasted_iota(jnp.int32, (CHUNK, LANES), 1)
    low_half = lane < HEADDIM
    hrow = lax.broadcasted_iota(jnp.int32, (LANES, GROUP_W), 0)
    hcol = lax.broadcasted_iota(jnp.int32, (LANES, GROUP_W), 1)

    for g in range(N_GROUPS):
        sl = slice(g * GROUP_W, (g + 1) * GROUP_W)
        sel = jnp.where(hrow == g * (GROUP_W // HEADDIM) + _div_pow2(hcol, HEADDIM), 1.0, 0.0).astype(BF16)
        ex = _dot3_r(stack, sel)
        dt_e, te_e, ea_e = ex[0:CHUNK], ex[CHUNK:2 * CHUNK], ex[2 * CHUNK:3 * CHUNK]

        x_g = xc_ref[:, sl]
        b_g = xc_ref[:, D_INNER + g * D_STATE:D_INNER + (g + 1) * D_STATE]
        c_g = xc_ref[:, D_INNER + BC_W + g * D_STATE:D_INNER + BC_W + (g + 1) * D_STATE]
        xdt = x_g * dt_e
        xdt_b = xdt.astype(BF16)
        cb16, bb16 = c_g.astype(BF16), b_g.astype(BF16)
        cb = _dot_nt(cb16, bb16)

        ht_g = ht_ref[:, sl]
        y_g = _dot(cb16, ht_g.astype(BF16)) * ea_e + dsk_ref[:, sl] * x_g
        states_t = _dot(b_g.T.astype(BF16), (xdt * te_e).astype(BF16))
        ht_ref[:, sl] = ea_e[CHUNK - 1:CHUNK, :] * ht_g + states_t

        y_pairs = []
        for jj in range(GROUP_W // LANES):
            xp = xdt_b[:, jj * LANES:(jj + 1) * LANES]
            halves = []
            for hh in range(2):
                h = g * (GROUP_W // HEADDIM) + 2 * jj + hh
                seg = a_cs[:, h:h + 1] - a_cs_t[h:h + 1, :]
                decay = jnp.exp(jnp.where(causal, seg, -jnp.inf))
                halves.append(_dot((cb * decay).astype(BF16), xp))
            y_pairs.append(jnp.where(low_half, halves[0], halves[1]))
        y_g = y_g + jnp.concatenate(y_pairs, axis=1)

        y_ref[:, sl] = _gated_norm(y_g, z_ref[:, sl], gn_ref[:, sl]).astype(BF16)

    @pl.when(ci == pl.num_programs(0) - 1)
    def _():
        for i in range(D_INNER // LANES):
            hout_ref[i * LANES:(i + 1) * LANES, :] = ht_ref[:, i * LANES:(i + 1) * LANES].T


def _ssd_prompt(proj, cw, cb, dtb, alog, dsk, gn):
    nchunk = SEQ // CHUNK
    full = lambda r, c: pl.BlockSpec((r, c), lambda i: (0, 0))
    return pl.pallas_call(
        _ssd_kernel,
        grid=(nchunk,),
        in_specs=[pl.BlockSpec((CHUNK, D_INNER), lambda i: (i, OFF_X // D_INNER)),
                  pl.BlockSpec((CHUNK, BC_W), lambda i: (i, OFF_B // BC_W)),
                  pl.BlockSpec((CHUNK, BC_W), lambda i: (i, OFF_C // BC_W)),
                  pl.BlockSpec((CHUNK, D_INNER), lambda i: (i, OFF_Z // D_INNER)),
                  pl.BlockSpec((CHUNK, LANES), lambda i: (i, OFF_DT // LANES)),
                  full(CONV_W, CONV_DIM), full(1, CONV_DIM), full(1, LANES), full(1, LANES),
                  full(1, D_INNER), full(1, D_INNER)],
        out_specs=[pl.BlockSpec((CHUNK, D_INNER), lambda i: (i, 0)),
                   pl.BlockSpec((D_INNER, D_STATE), lambda i: (0, 0)),
                   pl.BlockSpec((SUBLANES, CONV_DIM), lambda i: (0, 0))],
        out_shape=[jax.ShapeDtypeStruct((SEQ, D_INNER), BF16),
                   jax.ShapeDtypeStruct((D_INNER, D_STATE), F32),
                   jax.ShapeDtypeStruct((SUBLANES, CONV_DIM), F32)],
        scratch_shapes=[pltpu.VMEM((CHUNK + SUBLANES, CONV_DIM), F32),
                        pltpu.VMEM((CHUNK, CONV_DIM), F32),
                        pltpu.VMEM((D_STATE, D_INNER), F32)],
        compiler_params=_cparams("arbitrary"),
        name="ssd_prompt",
    )(proj, proj, proj, proj, proj, cw, cb, dtb, alog, dsk, gn)


def _attn_kernel(sink_ref, q_ref, kp_ref, kc_ref, vp_ref, vc_ref, o_ref):
    i = pl.program_id(0)
    l_idx = lax.broadcasted_iota(jnp.int32, (CHUNK, 2 * CHUNK), 0)
    s_idx = lax.broadcasted_iota(jnp.int32, (CHUNK, 2 * CHUNK), 1)
    first_key = jnp.where(i > 0, 0, CHUNK)
    mask = (s_idx >= jnp.maximum(l_idx, first_key)) & (s_idx <= l_idx + WINDOW)
    lane = lax.broadcasted_iota(jnp.int32, (CHUNK, LANES), 1)
    low_half = lane < HEADDIM

    for c in range(KV_DIM // LANES):
        csl = slice(c * LANES, (c + 1) * LANES)
        k2 = jnp.concatenate([kp_ref[:, csl], kc_ref[:, csl]], axis=0).astype(BF16)
        v2 = jnp.concatenate([vp_ref[:, csl], vc_ref[:, csl]], axis=0).astype(BF16)
        for r in range(Q_PER_KV):
            j = r * (KV_DIM // LANES) + c
            q2 = q_ref[:, j * LANES:(j + 1) * LANES] * (HEADDIM ** -0.5)
            halves = []
            for hh in range(2):
                qm = jnp.where(low_half if hh == 0 else jnp.logical_not(low_half), q2, 0.0)
                s = jnp.where(mask, _dot_nt(qm.astype(BF16), k2), -jnp.inf)
                sink = sink_ref[2 * j + hh]
                m = jnp.maximum(jnp.max(s, axis=-1, keepdims=True), sink)
                p = jnp.exp(s - m)
                den = jnp.sum(p, axis=-1, keepdims=True) + jnp.exp(sink - m)
                halves.append(_dot((p / den).astype(BF16), v2))
            o_ref[:, j * LANES:(j + 1) * LANES] = jnp.where(low_half, halves[0], halves[1]).astype(BF16)


def _attn_prompt(proj, sinks):
    nb = SEQ // CHUNK
    prev = lambda i: jnp.maximum(i - 1, 0)
    return pl.pallas_call(
        _attn_kernel,
        grid=(nb,),
        in_specs=[pl.BlockSpec(memory_space=pltpu.SMEM),
                  pl.BlockSpec((CHUNK, D_MODEL), lambda i: (i, OFF_Q // D_MODEL)),
                  pl.BlockSpec((CHUNK, KV_DIM), lambda i: (prev(i), OFF_K // KV_DIM)),
                  pl.BlockSpec((CHUNK, KV_DIM), lambda i: (i, OFF_K // KV_DIM)),
                  pl.BlockSpec((CHUNK, KV_DIM), lambda i: (prev(i), OFF_V // KV_DIM)),
                  pl.BlockSpec((CHUNK, KV_DIM), lambda i: (i, OFF_V // KV_DIM))],
        out_specs=pl.BlockSpec((CHUNK, D_MODEL), lambda i: (i, 0)),
        out_shape=jax.ShapeDtypeStruct((SEQ, D_MODEL), BF16),
        compiler_params=_cparams("parallel"),
        name="attn_prompt",
    )(sinks, proj, proj, proj, proj, proj)


def _conv_step_kernel(xs_ref, b_ref, c_ref, dt_ref, cs_ref, cw_ref, cb_ref, dtb_ref, alog_ref,
                      xc_ref, csn_ref, dto_ref, dao_ref):
    for lo, hi, src in ((0, D_INNER, xs_ref), (D_INNER, D_INNER + BC_W, b_ref),
                        (D_INNER + BC_W, CONV_DIM, c_ref)):
        for s in range(lo, hi, GROUP_W):
            sl = slice(s, s + GROUP_W)
            new = src[:, s - lo:s - lo + GROUP_W]
            taps = [cs_ref[:, k * CONV_DIM + s:k * CONV_DIM + s + GROUP_W] for k in range(CONV_W - 1)]
            taps.append(new)
            acc = cb_ref[:, sl]
            for k in range(CONV_W):
                acc = acc + taps[k] * cw_ref[k:k + 1, sl]
            xc_ref[:, sl] = _silu(acc)
            for k in range(CONV_W - 1):
                csn_ref[:, k * CONV_DIM + s:k * CONV_DIM + s + GROUP_W] = taps[k + 1]
    dt = _softplus(dt_ref[...] + dtb_ref[...])
    dto_ref[...] = dt
    dao_ref[...] = jnp.exp(dt * (-jnp.exp(alog_ref[...])))


def _conv_step(proj, conv_state, layer, cw, cb, dtb, alog):
    nb = DEC_BATCH
    full = lambda r, c: pl.BlockSpec((r, c), lambda i: (0, 0))
    return pl.pallas_call(
        _conv_step_kernel,
        grid=(1,),
        in_specs=[pl.BlockSpec((nb, D_INNER), lambda i: (0, OFF_X // D_INNER)),
                  pl.BlockSpec((nb, BC_W), lambda i: (0, OFF_B // BC_W)),
                  pl.BlockSpec((nb, BC_W), lambda i: (0, OFF_C // BC_W)),
                  pl.BlockSpec((nb, LANES), lambda i: (0, OFF_DT // LANES)),
                  pl.BlockSpec((None, nb, (CONV_W - 1) * CONV_DIM), lambda i: (layer, 0, 0)),
                  full(CONV_W, CONV_DIM), full(1, CONV_DIM), full(1, LANES), full(1, LANES)],
        out_specs=[full(nb, CONV_DIM), full(nb, (CONV_W - 1) * CONV_DIM),
                   full(nb, LANES), full(nb, LANES)],
        out_shape=[jax.ShapeDtypeStruct((nb, CONV_DIM), F32),
                   jax.ShapeDtypeStruct((nb, (CONV_W - 1) * CONV_DIM), F32),
                   jax.ShapeDtypeStruct((nb, LANES), F32),
                   jax.ShapeDtypeStruct((nb, LANES), F32)],
        compiler_params=_cparams("arbitrary"),
        name="conv_step",
    )(proj, proj, proj, proj, conv_state, cw, cb, dtb, alog)


STEP_B = 8
STEP_GROUPS = 4
STEP_W = STEP_GROUPS * GROUP_W


def _ssm_step_kernel(xs_ref, b_ref, c_ref, z_ref, dt_ref, da_ref, dsk_ref, gn_ref, h_ref,
                     y_ref, hn_ref):
    part = pl.program_id(1)
    xs = xs_ref[...]
    hrow = lax.broadcasted_iota(jnp.int32, (LANES, STEP_W), 0)
    hcol = lax.broadcasted_iota(jnp.int32, (LANES, STEP_W), 1)
    sel = jnp.where(hrow == part * (STEP_W // HEADDIM) + _div_pow2(hcol, HEADDIM), 1.0, 0.0).astype(BF16)
    dt_e = _dot3_r(dt_ref[...], sel)
    da_e = _dot3_r(da_ref[...], sel)
    xdt = xs * dt_e
    lane_group = _div_pow2(lax.broadcasted_iota(jnp.int32, (STEP_B, STEP_W), 1), GROUP_W)

    pieces = [jnp.where(lane_group == g, xdt, 0.0) for g in range(STEP_GROUPS)]
    pieces += [p.astype(F32) for p in _split3(da_e)]
    n_used = len(pieces) * STEP_B
    pieces.append(jnp.zeros((LANES - n_used, STEP_W), F32))
    ut = jnp.concatenate(pieces, axis=0)
    u_big = jnp.concatenate([ut[:, i * LANES:(i + 1) * LANES].T for i in range(STEP_W // LANES)],
                            axis=0).astype(BF16)

    b_rows = jnp.concatenate([b_ref[:, g * D_STATE:(g + 1) * D_STATE] for g in range(STEP_GROUPS)]
                             + [jnp.zeros((LANES - STEP_GROUPS * STEP_B, D_STATE), F32)], axis=0)
    rrow = lax.broadcasted_iota(jnp.int32, (LANES, D_STATE), 0)
    ones_rows = jnp.where((rrow >= STEP_GROUPS * STEP_B) & (rrow < n_used), 1.0, 0.0)
    big_r = jnp.concatenate([b_rows, ones_rows], axis=1)
    r_seq = _mod_pow2(lax.broadcasted_iota(jnp.int32, (LANES, 2 * D_STATE), 0), STEP_B)

    cc = jnp.concatenate([c_ref[:, g * D_STATE:(g + 1) * D_STATE] for g in range(STEP_GROUPS)],
                         axis=0).astype(BF16)
    crow = lax.broadcasted_iota(jnp.int32, (STEP_GROUPS * STEP_B, STEP_W), 0)
    ccol_group = _div_pow2(lax.broadcasted_iota(jnp.int32, (STEP_GROUPS * STEP_B, STEP_W), 1), GROUP_W)
    sub = lax.broadcasted_iota(jnp.int32, (STEP_B, STEP_W), 0)

    y_off = jnp.zeros((STEP_B, STEP_W), F32)
    for b in range(STEP_B):
        h0 = h_ref[b]
        res = _dot(u_big, jnp.where(r_seq == b, big_r, 0.0).astype(BF16))
        hn_ref[b] = res[:, D_STATE:] * h0 + res[:, :D_STATE]
        y_all = _dot_nt(cc, h0.astype(BF16))
        pick = (crow == ccol_group * STEP_B + b)
        y_b = jnp.sum(jnp.where(pick, y_all, 0.0), axis=0, keepdims=True)
        y_off = jnp.where(sub == b, y_b, y_off)

    cb_e = jnp.zeros((STEP_B, STEP_W), F32)
    for g in range(STEP_GROUPS):
        gsl = slice(g * D_STATE, (g + 1) * D_STATE)
        cb_g = jnp.sum(c_ref[:, gsl] * b_ref[:, gsl], axis=-1, keepdims=True)
        cb_e = jnp.where(lane_group == g, cb_g, cb_e)

    y = da_e * y_off + cb_e * xdt + dsk_ref[...] * xs
    for g in range(STEP_GROUPS):
        sl = slice(g * GROUP_W, (g + 1) * GROUP_W)
        y_ref[:, sl] = _gated_norm(y[:, sl], z_ref[:, sl], gn_ref[:, sl])


def _ssm_step(xc, proj, dt, da, dsk, gn, state, layer):
    nparts = D_INNER // STEP_W
    return pl.pallas_call(
        _ssm_step_kernel,
        grid=(DEC_BATCH // STEP_B, nparts),
        in_specs=[pl.BlockSpec((STEP_B, STEP_W), lambda i, p: (i, p)),
                  pl.BlockSpec((STEP_B, STEP_GROUPS * D_STATE),
                               lambda i, p: (i, D_INNER // (STEP_GROUPS * D_STATE) + p)),
                  pl.BlockSpec((STEP_B, STEP_GROUPS * D_STATE),
                               lambda i, p: (i, (D_INNER + BC_W) // (STEP_GROUPS * D_STATE) + p)),
                  pl.BlockSpec((STEP_B, STEP_W), lambda i, p: (i, OFF_Z // STEP_W + p)),
                  pl.BlockSpec((STEP_B, LANES), lambda i, p: (i, 0)),
                  pl.BlockSpec((STEP_B, LANES), lambda i, p: (i, 0)),
                  pl.BlockSpec((1, STEP_W), lambda i, p: (0, p)),
                  pl.BlockSpec((1, STEP_W), lambda i, p: (0, p)),
                  pl.BlockSpec((None, STEP_B, STEP_W, D_STATE), lambda i, p: (layer, i, p, 0))],
        out_specs=[pl.BlockSpec((STEP_B, STEP_W), lambda i, p: (i, p)),
                   pl.BlockSpec((STEP_B, STEP_W, D_STATE), lambda i, p: (i, p, 0))],
        out_shape=[jax.ShapeDtypeStruct((DEC_BATCH, D_INNER), F32),
                   jax.ShapeDtypeStruct((DEC_BATCH, D_INNER, D_STATE), F32)],
        compiler_params=_cparams("parallel", "parallel"),
        name="ssm_step",
    )(xc, xc, xc, proj, dt, da, dsk, gn, state)


ATTN_B = 8


def _attn_step_kernel(sink_ref, q_ref, kn_ref, vn_ref, kc_ref, vc_ref, o_ref, ko_ref, vo_ref):
    lane_kv = _div_pow2(lax.broadcasted_iota(jnp.int32, (N_KV, KV_DIM), 1), HEADDIM)
    own = lane_kv == lax.broadcasted_iota(jnp.int32, (N_KV, KV_DIM), 0)
    sink = sink_ref[...]
    for b in range(ATTN_B):
        qrow = q_ref[b:b + 1, :] * (HEADDIM ** -0.5)
        qm = jnp.concatenate(
            [jnp.where(own, jnp.broadcast_to(qrow[:, r * KV_DIM:(r + 1) * KV_DIM], (N_KV, KV_DIM)), 0.0)
             for r in range(Q_PER_KV)], axis=0)
        k_new = kn_ref[b:b + 1, :]
        v_new = vn_ref[b:b + 1, :]
        s = _dot_nt(qm.astype(BF16), kc_ref[b].astype(BF16))
        s_new = jnp.sum(qm * k_new, axis=-1, keepdims=True)
        m = jnp.maximum(jnp.maximum(jnp.max(s, axis=-1, keepdims=True), s_new), sink)
        p = jnp.exp(s - m)
        p_new = jnp.exp(s_new - m)
        den = jnp.sum(p, axis=-1, keepdims=True) + p_new + jnp.exp(sink - m)
        o_full = _dot((p / den).astype(BF16), vc_ref[b].astype(BF16)) + (p_new / den) * v_new
        parts = []
        for r in range(Q_PER_KV):
            o_r = jnp.where(own, o_full[r * N_KV:(r + 1) * N_KV, :], 0.0)
            parts.append(jnp.sum(o_r, axis=0, keepdims=True))
        o_ref[b:b + 1, :] = jnp.concatenate(parts, axis=1)
        ko_ref[b, 0:WINDOW - 1, :] = kc_ref[b, 1:WINDOW, :]
        ko_ref[b, WINDOW - 1:WINDOW, :] = k_new
        vo_ref[b, 0:WINDOW - 1, :] = vc_ref[b, 1:WINDOW, :]
        vo_ref[b, WINDOW - 1:WINDOW, :] = v_new


def _attn_step(proj, sinks, cache_k, cache_v, layer):
    cache_spec = pl.BlockSpec((None, ATTN_B, WINDOW, KV_DIM), lambda i: (layer, i, 0, 0))
    out_cache = pl.BlockSpec((ATTN_B, WINDOW, KV_DIM), lambda i: (i, 0, 0))
    return pl.pallas_call(
        _attn_step_kernel,
        grid=(DEC_BATCH // ATTN_B,),
        in_specs=[pl.BlockSpec((N_HEADS, 1), lambda i: (0, 0)),
                  pl.BlockSpec((ATTN_B, D_MODEL), lambda i: (i, OFF_Q // D_MODEL)),
                  pl.BlockSpec((ATTN_B, KV_DIM), lambda i: (i, OFF_K // KV_DIM)),
                  pl.BlockSpec((ATTN_B, KV_DIM), lambda i: (i, OFF_V // KV_DIM)),
                  cache_spec, cache_spec],
        out_specs=[pl.BlockSpec((ATTN_B, D_MODEL), lambda i: (i, 0)), out_cache, out_cache],
        out_shape=[jax.ShapeDtypeStruct((DEC_BATCH, D_MODEL), F32),
                   jax.ShapeDtypeStruct((DEC_BATCH, WINDOW, KV_DIM), F32),
                   jax.ShapeDtypeStruct((DEC_BATCH, WINDOW, KV_DIM), F32)],
        compiler_params=_cparams("parallel"),
        name="attn_step",
    )(sinks, proj, proj, proj, cache_k, cache_v)


def _prep_layer(l, norm_mix, w_in, conv_w, conv_b, dt_bias, a_log, d_skip, ssm_norm, w_ssm_proj,
                w_attn_proj, attn_sinks, w_out, norm_ffn, w_gate, w_up, w_down):
    w = w_in[l]
    wq = w[:, 10304:12352].reshape(D_MODEL, N_KV, Q_PER_KV, HEADDIM)
    wq = wq.transpose(0, 2, 1, 3).reshape(D_MODEL, D_MODEL)
    w_in_p = jnp.concatenate(
        [w[:, :10240], wq, w[:, 12352:IN_DIM], w[:, 10240:10304],
         jnp.zeros((D_MODEL, PROJ_W - IN_DIM), F32)], axis=1).astype(BF16)
    wa = w_attn_proj[l].reshape(N_KV, Q_PER_KV, HEADDIM, D_MODEL)
    wa = wa.transpose(1, 0, 2, 3).reshape(D_MODEL, D_MODEL).astype(BF16)
    pad = lambda v: jnp.pad(v, (0, LANES - N_SSM_HEADS)).reshape(1, LANES)
    return dict(
        norm_mix=norm_mix[l].reshape(1, D_MODEL), w_in=w_in_p,
        conv_w=conv_w[l], conv_b=conv_b[l].reshape(1, CONV_DIM),
        dt_bias=pad(dt_bias[l]), a_log=pad(a_log[l]),
        d_skip=jnp.repeat(d_skip[l], HEADDIM).reshape(1, D_INNER),
        ssm_norm=ssm_norm[l].reshape(1, D_INNER),
        w_ssm=w_ssm_proj[l].astype(BF16), w_attn=wa,
        sinks=attn_sinks[l].reshape(N_KV, Q_PER_KV).T.reshape(N_HEADS),
        w_out=w_out[l].astype(BF16), norm_ffn=norm_ffn[l].reshape(1, D_MODEL),
        w_gate=w_gate[l].astype(BF16), w_up=w_up[l].astype(BF16), w_down=w_down[l].astype(BF16))


def _dense_tail(x, y, o, proj, p, tm):
    merged = _merge(y, o, proj, p["w_ssm"], p["w_attn"], tm, 512)
    x1 = _matmul_res(merged, p["w_out"], x, tm, 1024, "out_proj")
    hmid = _ffn_up(x1, p["norm_ffn"], p["w_gate"], p["w_up"], tm, 512)
    return _matmul_res(hmid, p["w_down"], x1, tm, 512, "ffn_down")


PROMPT_TM = 512
PROJ_TN = 1280


def kernel(x_prompt, x_sample, state_ssm, state_conv, cache_k, cache_v, norm_mix, w_in, conv_w,
           conv_b, dt_bias, a_log, d_skip, ssm_norm, w_ssm_proj, w_attn_proj, attn_sinks, w_out,
           norm_ffn, w_gate, w_up, w_down, norm_final):
    xp = x_prompt.reshape(SEQ, D_MODEL)
    xs = x_sample.reshape(DEC_BATCH, D_MODEL)
    state4 = state_ssm.reshape(DEPTH, DEC_BATCH, D_INNER, D_STATE)
    conv3 = state_conv.reshape(DEPTH, DEC_BATCH, (CONV_W - 1) * CONV_DIM)
    ck = cache_k.reshape(DEPTH, DEC_BATCH, WINDOW, KV_DIM)
    cv = cache_v.reshape(DEPTH, DEC_BATCH, WINDOW, KV_DIM)
    outs = {k: [] for k in ("p_ssm", "p_conv", "p_k", "p_v", "s_ssm", "s_conv", "s_k", "s_v")}

    for l in range(DEPTH):
        p = _prep_layer(l, norm_mix, w_in, conv_w, conv_b, dt_bias, a_log, d_skip, ssm_norm,
                        w_ssm_proj, w_attn_proj, attn_sinks, w_out, norm_ffn, w_gate, w_up, w_down)

        proj = _in_proj(xp, p["norm_mix"], p["w_in"], PROMPT_TM, PROJ_TN)
        y, h_p, tail = _ssd_prompt(proj, p["conv_w"], p["conv_b"], p["dt_bias"], p["a_log"],
                                   p["d_skip"], p["ssm_norm"])
        o = _attn_prompt(proj, p["sinks"])
        outs["p_ssm"].append(h_p.reshape(1, N_SSM_HEADS, HEADDIM, D_STATE))
        outs["p_conv"].append(tail[SUBLANES - (CONV_W - 1):].reshape(1, CONV_W - 1, CONV_DIM))
        outs["p_k"].append(proj[SEQ - WINDOW:, OFF_K:OFF_K + KV_DIM].reshape(1, WINDOW, N_KV, HEADDIM))
        outs["p_v"].append(proj[SEQ - WINDOW:, OFF_V:OFF_V + KV_DIM].reshape(1, WINDOW, N_KV, HEADDIM))
        xp = _dense_tail(xp, y, o, proj, p, PROMPT_TM)

        proj_s = _in_proj(xs, p["norm_mix"], p["w_in"], DEC_BATCH, PROJ_TN)
        xc, conv_new, dt, da = _conv_step(proj_s, conv3, l, p["conv_w"], p["conv_b"],
                                          p["dt_bias"], p["a_log"])
        y_s, h_s = _ssm_step(xc, proj_s, dt, da, p["d_skip"], p["ssm_norm"], state4, l)
        o_s, k_s, v_s = _attn_step(proj_s, p["sinks"].reshape(N_HEADS, 1), ck, cv, l)
        outs["s_ssm"].append(h_s.reshape(DEC_BATCH, N_SSM_HEADS, HEADDIM, D_STATE))
        outs["s_conv"].append(conv_new.reshape(DEC_BATCH, CONV_W - 1, CONV_DIM))
        outs["s_k"].append(k_s.reshape(DEC_BATCH, WINDOW, N_KV, HEADDIM))
        outs["s_v"].append(v_s.reshape(DEC_BATCH, WINDOW, N_KV, HEADDIM))
        xs = _dense_tail(xs, y_s, o_s, proj_s, p, DEC_BATCH)

    y_prompt = _rmsnorm(xp, norm_final.reshape(1, D_MODEL), PROMPT_TM).reshape(1, SEQ, D_MODEL)
    y_sample = _rmsnorm(xs, norm_final.reshape(1, D_MODEL), DEC_BATCH).reshape(DEC_BATCH, 1, D_MODEL)
    st = lambda k: jnp.stack(outs[k])
    return (y_prompt, y_sample, st("p_ssm"), st("p_conv"), st("p_k"), st("p_v"),
            st("s_ssm"), st("s_conv"), st("s_k"), st("s_v"))
```

```python
import functools

import jax
import jax.numpy as jnp
from jax import lax
from jax.experimental import pallas as pl
from jax.experimental.pallas import tpu as pltpu

F32 = jnp.float32
BF16 = jnp.bfloat16

D_MODEL = 2048
SEQ = 8192
DEPTH = 2
DEC_BATCH = 128
D_INNER = 4096
HEADDIM = 64
N_SSM_HEADS = 64
D_STATE = 128
N_GROUPS = 8
GROUP_W = D_INNER // N_GROUPS
CONV_W = 4
BC_W = N_GROUPS * D_STATE
CONV_DIM = D_INNER + 2 * BC_W
CHUNK = 128
N_HEADS = 32
N_KV = 8
Q_PER_KV = 4
KV_DIM = N_KV * HEADDIM
WINDOW = 128
D_FF = 5632
EPS = 1e-6
IN_DIM = 17472

OFF_Z = 0
OFF_X = 4096
OFF_B = OFF_X + D_INNER
OFF_C = OFF_B + BC_W
OFF_Q = 10240
OFF_K = 12288
OFF_V = 12800
OFF_GS = 13312
OFF_GA = 15360
OFF_DT = 17408
PROJ_W = 17920

LANES = 128
SUBLANES = 8
VMEM_LIMIT = 56 * 1024 * 1024


def _cparams(*sem):
    return pltpu.CompilerParams(dimension_semantics=sem, vmem_limit_bytes=VMEM_LIMIT)


def _dot(a, b):
    return jnp.dot(a, b, preferred_element_type=F32)


def _dot_nt(a, b):
    return lax.dot_general(a, b, (((1,), (1,)), ((), ())), preferred_element_type=F32)


def _div_pow2(v, d):
    return jnp.right_shift(v, d.bit_length() - 1)


def _mod_pow2(v, d):
    return jnp.bitwise_and(v, d - 1)


def _split3(x):
    hi = x.astype(BF16)
    r1 = x - hi.astype(F32)
    mid = r1.astype(BF16)
    lo = (r1 - mid.astype(F32)).astype(BF16)
    return hi, mid, lo


def _dot3_r(x, sel):
    hi, mid, lo = _split3(x)
    return _dot(hi, sel) + _dot(mid, sel) + _dot(lo, sel)


def _dot3_l(sel, x):
    hi, mid, lo = _split3(x)
    return _dot(sel, hi) + _dot(sel, mid) + _dot(sel, lo)


def _silu(x):
    h = 0.5 * x
    return h + h * jnp.tanh(h)


def _softplus(x):
    return jnp.maximum(x, 0.0) + jnp.log1p(jnp.exp(-jnp.abs(x)))


def _rms_scale(x, g):
    r = lax.rsqrt(jnp.mean(x * x, axis=-1, keepdims=True) + EPS)
    return x * r * g


def _norm_matmul_kernel(x_ref, g_ref, w_ref, o_ref, xn_ref):
    @pl.when(pl.program_id(1) == 0)
    def _():
        xn_ref[...] = _rms_scale(x_ref[...], g_ref[...]).astype(BF16)

    o_ref[...] = _dot(xn_ref[...], w_ref[...])


def _in_proj(x, g, w, layer, tm, tn):
    m, k = x.shape
    n = w.shape[2]
    return pl.pallas_call(
        _norm_matmul_kernel,
        grid=(m // tm, n // tn),
        in_specs=[pl.BlockSpec((tm, k), lambda i, j: (i, 0)),
                  pl.BlockSpec((1, k), lambda i, j: (0, 0)),
                  pl.BlockSpec((None, k, tn), lambda i, j: (layer, 0, j))],
        out_specs=pl.BlockSpec((tm, tn), lambda i, j: (i, j)),
        out_shape=jax.ShapeDtypeStruct((m, n), F32),
        scratch_shapes=[pltpu.VMEM((tm, k), BF16)],
        compiler_params=_cparams("parallel", "arbitrary"),
        name="in_proj",
    )(x, g, w)


def _ffn_up_kernel(x_ref, g_ref, wg_ref, wu_ref, o_ref, xn_ref):
    @pl.when(pl.program_id(1) == 0)
    def _():
        xn_ref[...] = _rms_scale(x_ref[...], g_ref[...]).astype(BF16)

    xn = xn_ref[...]
    o_ref[...] = (_silu(_dot(xn, wg_ref[...])) * _dot(xn, wu_ref[...])).astype(BF16)


def _ffn_up(x, g, wg, wu, layer, tm, tn):
    m, k = x.shape
    n = wg.shape[2]
    return pl.pallas_call(
        _ffn_up_kernel,
        grid=(m // tm, n // tn),
        in_specs=[pl.BlockSpec((tm, k), lambda i, j: (i, 0)),
                  pl.BlockSpec((1, k), lambda i, j: (0, 0)),
                  pl.BlockSpec((None, k, tn), lambda i, j: (layer, 0, j)),
                  pl.BlockSpec((None, k, tn), lambda i, j: (layer, 0, j))],
        out_specs=pl.BlockSpec((tm, tn), lambda i, j: (i, j)),
        out_shape=jax.ShapeDtypeStruct((m, n), BF16),
        scratch_shapes=[pltpu.VMEM((tm, k), BF16)],
        compiler_params=_cparams("parallel", "arbitrary"),
        name="ffn_up",
    )(x, g, wg, wu)


def _matmul_res_kernel(a_ref, w_ref, r_ref, o_ref):
    o_ref[...] = r_ref[...] + _dot(a_ref[...], w_ref[...])


def _matmul_res(a, w, layer, res, tm, tn, name):
    m, k = a.shape
    n = w.shape[2]
    return pl.pallas_call(
        _matmul_res_kernel,
        grid=(m // tm, n // tn),
        in_specs=[pl.BlockSpec((tm, k), lambda i, j: (i, 0)),
                  pl.BlockSpec((None, k, tn), lambda i, j: (layer, 0, j)),
                  pl.BlockSpec((tm, tn), lambda i, j: (i, j))],
        out_specs=pl.BlockSpec((tm, tn), lambda i, j: (i, j)),
        out_shape=jax.ShapeDtypeStruct((m, n), F32),
        compiler_params=_cparams("parallel", "parallel"),
        name=name,
    )(a, w, res)


def _merge_kernel(y_ref, o_ref, gs_ref, ga_ref, ws_ref, wa_ref, out_ref):
    ys = _dot(y_ref[...].astype(BF16), ws_ref[...])
    oa = _dot(o_ref[...].astype(BF16), wa_ref[...])
    out_ref[...] = (jax.nn.sigmoid(gs_ref[...]) * ys
                    + jax.nn.sigmoid(ga_ref[...]) * oa).astype(BF16)


def _merge(y, o, proj, ws, wa, layer, tm, tn):
    m = y.shape[0]
    gs0, ga0 = OFF_GS // tn, OFF_GA // tn
    return pl.pallas_call(
        _merge_kernel,
        grid=(m // tm, D_MODEL // tn),
        in_specs=[pl.BlockSpec((tm, D_INNER), lambda i, j: (i, 0)),
                  pl.BlockSpec((tm, D_MODEL), lambda i, j: (i, 0)),
                  pl.BlockSpec((tm, tn), lambda i, j: (i, gs0 + j)),
                  pl.BlockSpec((tm, tn), lambda i, j: (i, ga0 + j)),
                  pl.BlockSpec((None, D_INNER, tn), lambda i, j: (layer, 0, j)),
                  pl.BlockSpec((None, D_MODEL, tn), lambda i, j: (layer, 0, j))],
        out_specs=pl.BlockSpec((tm, tn), lambda i, j: (i, j)),
        out_shape=jax.ShapeDtypeStruct((m, D_MODEL), BF16),
        compiler_params=_cparams("parallel", "parallel"),
        name="merge",
    )(y, o, proj, proj, ws, wa)


def _rmsnorm_kernel(x_ref, g_ref, o_ref):
    o_ref[...] = _rms_scale(x_ref[...], g_ref[...])


def _rmsnorm(x, g, tm):
    m, k = x.shape
    return pl.pallas_call(
        _rmsnorm_kernel,
        grid=(m // tm,),
        in_specs=[pl.BlockSpec((tm, k), lambda i: (i, 0)),
                  pl.BlockSpec((1, k), lambda i: (0, 0))],
        out_specs=pl.BlockSpec((tm, k), lambda i: (i, 0)),
        out_shape=jax.ShapeDtypeStruct((m, k), F32),
        compiler_params=_cparams("parallel"),
        name="final_norm",
    )(x, g)


def _gated_norm(y, z, gain):
    h = y * _silu(z)
    r = lax.rsqrt(jnp.mean(h * h, axis=-1, keepdims=True) + EPS)
    return h * r * gain


def _ssd_kernel(xs_ref, b_ref, c_ref, z_ref, dt_ref, cw_ref, cb_ref, dtb_ref, alog_ref,
                dsk_ref, gn_ref, y_ref, hout_ref, cout_ref, xpad_ref, xc_ref, ht_ref):
    ci = pl.program_id(0)

    @pl.when(ci == 0)
    def _():
        xpad_ref[0:SUBLANES, :] = jnp.zeros((SUBLANES, CONV_DIM), F32)
        ht_ref[...] = jnp.zeros_like(ht_ref)

    xpad_ref[SUBLANES:SUBLANES + CHUNK, 0:D_INNER] = xs_ref[...]
    xpad_ref[SUBLANES:SUBLANES + CHUNK, D_INNER:D_INNER + BC_W] = b_ref[...]
    xpad_ref[SUBLANES:SUBLANES + CHUNK, D_INNER + BC_W:CONV_DIM] = c_ref[...]

    first = SUBLANES - (CONV_W - 1)
    for s in range(CONV_DIM // GROUP_W):
        sl = slice(s * GROUP_W, (s + 1) * GROUP_W)
        acc = cb_ref[:, sl]
        for k in range(CONV_W):
            acc = acc + xpad_ref[first + k:first + k + CHUNK, sl] * cw_ref[k:k + 1, sl]
        xc_ref[:, sl] = _silu(acc)

    tail = xpad_ref[CHUNK:CHUNK + SUBLANES, :]
    xpad_ref[0:SUBLANES, :] = tail
    cout_ref[...] = tail

    dt = _softplus(dt_ref[...] + dtb_ref[...])
    a = dt * (-jnp.exp(alog_ref[...]))
    row = lax.broadcasted_iota(jnp.int32, (CHUNK, CHUNK), 0)
    col = lax.broadcasted_iota(jnp.int32, (CHUNK, CHUNK), 1)
    causal = col <= row
    tri = jnp.where(causal, 1.0, 0.0).astype(BF16)
    a_cs = _dot3_l(tri, a)
    a_cs_t = a_cs.T
    ea = jnp.exp(a_cs)
    to_end = jnp.exp(a_cs[CHUNK - 1:CHUNK, :] - a_cs)
    stack = jnp.concatenate([dt, to_end, ea], axis=0)

    lane = lax.broadcasted_iota(jnp.int32, (CHUNK, LANES), 1)
    low_half = lane < HEADDIM
    hrow = lax.broadcasted_iota(jnp.int32, (LANES, GROUP_W), 0)
    hcol = lax.broadcasted_iota(jnp.int32, (LANES, GROUP_W), 1)

    for g in range(N_GROUPS):
        sl = slice(g * GROUP_W, (g + 1) * GROUP_W)
        sel = jnp.where(hrow == g * (GROUP_W // HEADDIM) + _div_pow2(hcol, HEADDIM), 1.0, 0.0).astype(BF16)
        ex = _dot3_r(stack, sel)
        dt_e, te_e, ea_e = ex[0:CHUNK], ex[CHUNK:2 * CHUNK], ex[2 * CHUNK:3 * CHUNK]

        x_g = xc_ref[:, sl]
        b_g = xc_ref[:, D_INNER + g * D_STATE:D_INNER + (g + 1) * D_STATE]
        c_g = xc_ref[:, D_INNER + BC_W + g * D_STATE:D_INNER + BC_W + (g + 1) * D_STATE]
        xdt = x_g * dt_e
        xdt_b = xdt.astype(BF16)
        cb16, bb16 = c_g.astype(BF16), b_g.astype(BF16)
        cb = _dot_nt(cb16, bb16)

        ht_g = ht_ref[:, sl]
        y_g = _dot(cb16, ht_g.astype(BF16)) * ea_e + dsk_ref[:, sl] * x_g
        states_t = _dot(b_g.T.astype(BF16), (xdt * te_e).astype(BF16))
        ht_ref[:, sl] = ea_e[CHUNK - 1:CHUNK, :] * ht_g + states_t

        y_pairs = []
        for jj in range(GROUP_W // LANES):
            xp = xdt_b[:, jj * LANES:(jj + 1) * LANES]
            halves = []
            for hh in range(2):
                h = g * (GROUP_W // HEADDIM) + 2 * jj + hh
                seg = a_cs[:, h:h + 1] - a_cs_t[h:h + 1, :]
                decay = jnp.exp(jnp.where(causal, seg, -jnp.inf))
                halves.append(_dot((cb * decay).astype(BF16), xp))
            y_pairs.append(jnp.where(low_half, halves[0], halves[1]))
        y_g = y_g + jnp.concatenate(y_pairs, axis=1)

        y_ref[:, sl] = _gated_norm(y_g, z_ref[:, sl], gn_ref[:, sl]).astype(BF16)

    @pl.when(ci == pl.num_programs(0) - 1)
    def _():
        for i in range(D_INNER // LANES):
            hout_ref[i * LANES:(i + 1) * LANES, :] = ht_ref[:, i * LANES:(i + 1) * LANES].T


def _ssd_prompt(proj, cw, cb, dtb, alog, dsk, gn):
    nchunk = SEQ // CHUNK
    full = lambda r, c: pl.BlockSpec((r, c), lambda i: (0, 0))
    return pl.pallas_call(
        _ssd_kernel,
        grid=(nchunk,),
        in_specs=[pl.BlockSpec((CHUNK, D_INNER), lambda i: (i, OFF_X // D_INNER)),
                  pl.BlockSpec((CHUNK, BC_W), lambda i: (i, OFF_B // BC_W)),
                  pl.BlockSpec((CHUNK, BC_W), lambda i: (i, OFF_C // BC_W)),
                  pl.BlockSpec((CHUNK, D_INNER), lambda i: (i, OFF_Z // D_INNER)),
                  pl.BlockSpec((CHUNK, LANES), lambda i: (i, OFF_DT // LANES)),
                  full(CONV_W, CONV_DIM), full(1, CONV_DIM), full(1, LANES), full(1, LANES),
                  full(1, D_INNER), full(1, D_INNER)],
        out_specs=[pl.BlockSpec((CHUNK, D_INNER), lambda i: (i, 0)),
                   pl.BlockSpec((D_INNER, D_STATE), lambda i: (0, 0)),
                   pl.BlockSpec((SUBLANES, CONV_DIM), lambda i: (0, 0))],
        out_shape=[jax.ShapeDtypeStruct((SEQ, D_INNER), BF16),
                   jax.ShapeDtypeStruct((D_INNER, D_STATE), F32),
                   jax.ShapeDtypeStruct((SUBLANES, CONV_DIM), F32)],
        scratch_shapes=[pltpu.VMEM((CHUNK + SUBLANES, CONV_DIM), F32),
                        pltpu.VMEM((CHUNK, CONV_DIM), F32),
                        pltpu.VMEM((D_STATE, D_INNER), F32)],
        compiler_params=_cparams("arbitrary"),
        name="ssd_prompt",
    )(proj, proj, proj, proj, proj, cw, cb, dtb, alog, dsk, gn)


def _attn_kernel(sink_ref, q_ref, kp_ref, kc_ref, vp_ref, vc_ref, o_ref):
    i = pl.program_id(0)
    l_idx = lax.broadcasted_iota(jnp.int32, (CHUNK, 2 * CHUNK), 0)
    s_idx = lax.broadcasted_iota(jnp.int32, (CHUNK, 2 * CHUNK), 1)
    first_key = jnp.where(i > 0, 0, CHUNK)
    mask = (s_idx >= jnp.maximum(l_idx, first_key)) & (s_idx <= l_idx + WINDOW)
    lane = lax.broadcasted_iota(jnp.int32, (CHUNK, LANES), 1)
    low_half = lane < HEADDIM

    for c in range(KV_DIM // LANES):
        csl = slice(c * LANES, (c + 1) * LANES)
        k2 = jnp.concatenate([kp_ref[:, csl], kc_ref[:, csl]], axis=0).astype(BF16)
        v2 = jnp.concatenate([vp_ref[:, csl], vc_ref[:, csl]], axis=0).astype(BF16)
        for r in range(Q_PER_KV):
            j = r * (KV_DIM // LANES) + c
            q2 = q_ref[:, j * LANES:(j + 1) * LANES] * (HEADDIM ** -0.5)
            halves = []
            for hh in range(2):
                qm = jnp.where(low_half if hh == 0 else jnp.logical_not(low_half), q2, 0.0)
                s = jnp.where(mask, _dot_nt(qm.astype(BF16), k2), -jnp.inf)
                sink = sink_ref[2 * j + hh]
                m = jnp.maximum(jnp.max(s, axis=-1, keepdims=True), sink)
                p = jnp.exp(s - m)
                den = jnp.sum(p, axis=-1, keepdims=True) + jnp.exp(sink - m)
                halves.append(_dot((p / den).astype(BF16), v2))
            o_ref[:, j * LANES:(j + 1) * LANES] = jnp.where(low_half, halves[0], halves[1]).astype(BF16)


def _attn_prompt(proj, sinks):
    nb = SEQ // CHUNK
    prev = lambda i: jnp.maximum(i - 1, 0)
    return pl.pallas_call(
        _attn_kernel,
        grid=(nb,),
        in_specs=[pl.BlockSpec(memory_space=pltpu.SMEM),
                  pl.BlockSpec((CHUNK, D_MODEL), lambda i: (i, OFF_Q // D_MODEL)),
                  pl.BlockSpec((CHUNK, KV_DIM), lambda i: (prev(i), OFF_K // KV_DIM)),
                  pl.BlockSpec((CHUNK, KV_DIM), lambda i: (i, OFF_K // KV_DIM)),
                  pl.BlockSpec((CHUNK, KV_DIM), lambda i: (prev(i), OFF_V // KV_DIM)),
                  pl.BlockSpec((CHUNK, KV_DIM), lambda i: (i, OFF_V // KV_DIM))],
        out_specs=pl.BlockSpec((CHUNK, D_MODEL), lambda i: (i, 0)),
        out_shape=jax.ShapeDtypeStruct((SEQ, D_MODEL), BF16),
        compiler_params=_cparams("parallel"),
        name="attn_prompt",
    )(sinks, proj, proj, proj, proj, proj)


def _alias_prev(prev, n_in, first_out):
    if prev is None:
        return [], {}, ()
    specs = [pl.BlockSpec(memory_space=pl.ANY)] * len(prev)
    return specs, {n_in + t: first_out + t for t in range(len(prev))}, tuple(prev)


def _conv_step_kernel(x_ref, dt_ref, cs_ref, cw_ref, cb_ref, dtb_ref, alog_ref, *rest):
    xc_ref, csn_ref, dto_ref, dao_ref = rest[-4:]
    taps = [cs_ref[:, k, :] for k in range(CONV_W - 1)] + [x_ref[...]]
    acc = cb_ref[...]
    for k in range(CONV_W):
        acc = acc + taps[k] * cw_ref[k:k + 1, :]
    xc_ref[...] = _silu(acc)
    for k in range(CONV_W - 1):
        csn_ref[:, k, :] = taps[k + 1]
    dt = _softplus(dt_ref[...] + dtb_ref[...])
    dto_ref[...] = dt
    dao_ref[...] = jnp.exp(dt * (-jnp.exp(alog_ref[...])))


def _conv_step(proj, conv_state, layer, cw, cb, dtb, alog, prev):
    nb = DEC_BATCH
    small = pl.BlockSpec((1, LANES), lambda s: (0, 0))
    head = pl.BlockSpec((nb, LANES), lambda s: (0, 0))
    state = pl.BlockSpec((None, nb, CONV_W - 1, GROUP_W), lambda s: (layer, 0, 0, s))
    in_specs = [pl.BlockSpec((nb, GROUP_W), lambda s: (0, OFF_X // GROUP_W + s)),
                pl.BlockSpec((nb, LANES), lambda s: (0, OFF_DT // LANES)),
                state,
                pl.BlockSpec((CONV_W, GROUP_W), lambda s: (0, s)),
                pl.BlockSpec((1, GROUP_W), lambda s: (0, s)),
                small, small]
    alias_specs, aliases, alias_args = _alias_prev(prev, len(in_specs), 1)
    return pl.pallas_call(
        _conv_step_kernel,
        grid=(CONV_DIM // GROUP_W,),
        in_specs=in_specs + alias_specs,
        out_specs=[pl.BlockSpec((nb, GROUP_W), lambda s: (0, s)), state, head, head],
        out_shape=[jax.ShapeDtypeStruct((nb, CONV_DIM), F32),
                   jax.ShapeDtypeStruct(conv_state.shape, F32),
                   jax.ShapeDtypeStruct((nb, LANES), F32),
                   jax.ShapeDtypeStruct((nb, LANES), F32)],
        input_output_aliases=aliases,
        compiler_params=_cparams("arbitrary"),
        name="conv_step",
    )(proj, proj, conv_state, cw, cb, dtb, alog, *alias_args)


STEP_B = 8
STEP_GROUPS = 4
STEP_W = STEP_GROUPS * GROUP_W


def _ssm_step_kernel(xs_ref, b_ref, c_ref, z_ref, dt_ref, da_ref, dsk_ref, gn_ref, h_ref, *rest):
    y_ref, hn_ref = rest[-2:]
    part = pl.program_id(1)
    xs = xs_ref[...]
    hrow = lax.broadcasted_iota(jnp.int32, (LANES, STEP_W), 0)
    hcol = lax.broadcasted_iota(jnp.int32, (LANES, STEP_W), 1)
    sel = jnp.where(hrow == part * (STEP_W // HEADDIM) + _div_pow2(hcol, HEADDIM), 1.0, 0.0).astype(BF16)
    dt_e = _dot3_r(dt_ref[...], sel)
    da_e = _dot3_r(da_ref[...], sel)
    xdt = xs * dt_e
    lane_group = _div_pow2(lax.broadcasted_iota(jnp.int32, (STEP_B, STEP_W), 1), GROUP_W)

    pieces = [jnp.where(lane_group == g, xdt, 0.0) for g in range(STEP_GROUPS)]
    pieces += [p.astype(F32) for p in _split3(da_e)]
    n_used = len(pieces) * STEP_B
    pieces.append(jnp.zeros((LANES - n_used, STEP_W), F32))
    ut = jnp.concatenate(pieces, axis=0)
    u_big = jnp.concatenate([ut[:, i * LANES:(i + 1) * LANES].T for i in range(STEP_W // LANES)],
                            axis=0).astype(BF16)

    b_rows = jnp.concatenate([b_ref[:, g * D_STATE:(g + 1) * D_STATE] for g in range(STEP_GROUPS)]
                             + [jnp.zeros((LANES - STEP_GROUPS * STEP_B, D_STATE), F32)], axis=0)
    rrow = lax.broadcasted_iota(jnp.int32, (LANES, D_STATE), 0)
    ones_rows = jnp.where((rrow >= STEP_GROUPS * STEP_B) & (rrow < n_used), 1.0, 0.0)
    big_r = jnp.concatenate([b_rows, ones_rows], axis=1)
    r_seq = _mod_pow2(lax.broadcasted_iota(jnp.int32, (LANES, 2 * D_STATE), 0), STEP_B)

    cc = jnp.concatenate([c_ref[:, g * D_STATE:(g + 1) * D_STATE] for g in range(STEP_GROUPS)],
                         axis=0).astype(BF16)
    crow = lax.broadcasted_iota(jnp.int32, (STEP_GROUPS * STEP_B, STEP_W), 0)
    ccol_group = _div_pow2(lax.broadcasted_iota(jnp.int32, (STEP_GROUPS * STEP_B, STEP_W), 1), GROUP_W)
    sub = lax.broadcasted_iota(jnp.int32, (STEP_B, STEP_W), 0)

    y_off = jnp.zeros((STEP_B, STEP_W), F32)
    for b in range(STEP_B):
        h0 = h_ref[b]
        res = _dot(u_big, jnp.where(r_seq == b, big_r, 0.0).astype(BF16))
        hn_ref[b] = res[:, D_STATE:] * h0 + res[:, :D_STATE]
        y_all = _dot_nt(cc, h0.astype(BF16))
        pick = (crow == ccol_group * STEP_B + b)
        y_b = jnp.sum(jnp.where(pick, y_all, 0.0), axis=0, keepdims=True)
        y_off = jnp.where(sub == b, y_b, y_off)

    cb_e = jnp.zeros((STEP_B, STEP_W), F32)
    for g in range(STEP_GROUPS):
        gsl = slice(g * D_STATE, (g + 1) * D_STATE)
        cb_g = jnp.sum(c_ref[:, gsl] * b_ref[:, gsl], axis=-1, keepdims=True)
        cb_e = jnp.where(lane_group == g, cb_g, cb_e)

    y = da_e * y_off + cb_e * xdt + dsk_ref[...] * xs
    for g in range(STEP_GROUPS):
        sl = slice(g * GROUP_W, (g + 1) * GROUP_W)
        y_ref[:, sl] = _gated_norm(y[:, sl], z_ref[:, sl], gn_ref[:, sl])


def _ssm_step(xc, proj, dt, da, dsk, gn, state, layer, prev):
    nparts = D_INNER // STEP_W
    state_spec = pl.BlockSpec((None, STEP_B, STEP_W, D_STATE), lambda i, p: (layer, i, p, 0))
    in_specs = [pl.BlockSpec((STEP_B, STEP_W), lambda i, p: (i, p)),
                pl.BlockSpec((STEP_B, STEP_GROUPS * D_STATE),
                             lambda i, p: (i, D_INNER // (STEP_GROUPS * D_STATE) + p)),
                pl.BlockSpec((STEP_B, STEP_GROUPS * D_STATE),
                             lambda i, p: (i, (D_INNER + BC_W) // (STEP_GROUPS * D_STATE) + p)),
                pl.BlockSpec((STEP_B, STEP_W), lambda i, p: (i, OFF_Z // STEP_W + p)),
                pl.BlockSpec((STEP_B, LANES), lambda i, p: (i, 0)),
                pl.BlockSpec((STEP_B, LANES), lambda i, p: (i, 0)),
                pl.BlockSpec((1, STEP_W), lambda i, p: (0, p)),
                pl.BlockSpec((1, STEP_W), lambda i, p: (0, p)),
                state_spec]
    alias_specs, aliases, alias_args = _alias_prev(prev, len(in_specs), 1)
    return pl.pallas_call(
        _ssm_step_kernel,
        grid=(DEC_BATCH // STEP_B, nparts),
        in_specs=in_specs + alias_specs,
        out_specs=[pl.BlockSpec((STEP_B, STEP_W), lambda i, p: (i, p)), state_spec],
        out_shape=[jax.ShapeDtypeStruct((DEC_BATCH, D_INNER), F32),
                   jax.ShapeDtypeStruct(state.shape, F32)],
        input_output_aliases=aliases,
        compiler_params=_cparams("parallel", "parallel"),
        name="ssm_step",
    )(xc, xc, xc, proj, dt, da, dsk, gn, state, *alias_args)


ATTN_B = 8


def _attn_step_kernel(sink_ref, q_ref, kn_ref, vn_ref, kc_ref, vc_ref, *rest):
    o_ref, ko_ref, vo_ref = rest[-3:]
    lane_kv = _div_pow2(lax.broadcasted_iota(jnp.int32, (N_KV, KV_DIM), 1), HEADDIM)
    own = lane_kv == lax.broadcasted_iota(jnp.int32, (N_KV, KV_DIM), 0)
    sink = sink_ref[...]
    for b in range(ATTN_B):
        qrow = q_ref[b:b + 1, :] * (HEADDIM ** -0.5)
        qm = jnp.concatenate(
            [jnp.where(own, jnp.broadcast_to(qrow[:, r * KV_DIM:(r + 1) * KV_DIM], (N_KV, KV_DIM)), 0.0)
             for r in range(Q_PER_KV)], axis=0)
        k_new = kn_ref[b:b + 1, :]
        v_new = vn_ref[b:b + 1, :]
        s = _dot_nt(qm.astype(BF16), kc_ref[b].astype(BF16))
        s_new = jnp.sum(qm * k_new, axis=-1, keepdims=True)
        m = jnp.maximum(jnp.maximum(jnp.max(s, axis=-1, keepdims=True), s_new), sink)
        p = jnp.exp(s - m)
        p_new = jnp.exp(s_new - m)
        den = jnp.sum(p, axis=-1, keepdims=True) + p_new + jnp.exp(sink - m)
        o_full = _dot((p / den).astype(BF16), vc_ref[b].astype(BF16)) + (p_new / den) * v_new
        parts = []
        for r in range(Q_PER_KV):
            o_r = jnp.where(own, o_full[r * N_KV:(r + 1) * N_KV, :], 0.0)
            parts.append(jnp.sum(o_r, axis=0, keepdims=True))
        o_ref[b:b + 1, :] = jnp.concatenate(parts, axis=1)
        ko_ref[b, 0:WINDOW - 1, :] = kc_ref[b, 1:WINDOW, :]
        ko_ref[b, WINDOW - 1:WINDOW, :] = k_new
        vo_ref[b, 0:WINDOW - 1, :] = vc_ref[b, 1:WINDOW, :]
        vo_ref[b, WINDOW - 1:WINDOW, :] = v_new


def _attn_step(proj, sinks, cache_k, cache_v, layer, prev):
    cache_spec = pl.BlockSpec((None, ATTN_B, WINDOW, KV_DIM), lambda i: (layer, i, 0, 0))
    in_specs = [pl.BlockSpec((N_HEADS, 1), lambda i: (0, 0)),
                pl.BlockSpec((ATTN_B, D_MODEL), lambda i: (i, OFF_Q // D_MODEL)),
                pl.BlockSpec((ATTN_B, KV_DIM), lambda i: (i, OFF_K // KV_DIM)),
                pl.BlockSpec((ATTN_B, KV_DIM), lambda i: (i, OFF_V // KV_DIM)),
                cache_spec, cache_spec]
    alias_specs, aliases, alias_args = _alias_prev(prev, len(in_specs), 1)
    return pl.pallas_call(
        _attn_step_kernel,
        grid=(DEC_BATCH // ATTN_B,),
        in_specs=in_specs + alias_specs,
        out_specs=[pl.BlockSpec((ATTN_B, D_MODEL), lambda i: (i, 0)), cache_spec, cache_spec],
        out_shape=[jax.ShapeDtypeStruct((DEC_BATCH, D_MODEL), F32),
                   jax.ShapeDtypeStruct(cache_k.shape, F32),
                   jax.ShapeDtypeStruct(cache_v.shape, F32)],
        input_output_aliases=aliases,
        compiler_params=_cparams("parallel"),
        name="attn_step",
    )(sinks, proj, proj, proj, cache_k, cache_v, *alias_args)


def _cast_kernel(x_ref, o_ref):
    o_ref[...] = x_ref[...].astype(BF16)


def _cast_bf16(w, tk, name):
    d, k, n = w.shape
    spec = pl.BlockSpec((None, tk, n), lambda l, i: (l, i, 0))
    return pl.pallas_call(
        _cast_kernel,
        grid=(d, k // tk),
        in_specs=[spec],
        out_specs=spec,
        out_shape=jax.ShapeDtypeStruct(w.shape, BF16),
        compiler_params=_cparams("parallel", "parallel"),
        name=name,
    )(w)


HALF = LANES // 2
SRC_DT = 10240
SRC_Q = SRC_DT + N_SSM_HEADS
SRC_K = SRC_Q + D_MODEL


def _w_in_layout_kernel(x_ref, o_ref):
    rows = x_ref.shape[0]

    def half(c):
        base = c // LANES * LANES
        if base + LANES > IN_DIM:
            return x_ref[:, base:IN_DIM]
        tile = x_ref[:, base:base + LANES]
        return tile[:, c - base:c - base + HALF]

    def put(dst, c0, c1):
        o_ref[:, dst:dst + LANES] = jnp.concatenate([half(c0), half(c1)], axis=1).astype(BF16)

    o_ref[:, 0:OFF_Q] = x_ref[:, 0:OFF_Q].astype(BF16)
    for r in range(Q_PER_KV):
        for mp in range(N_KV // 2):
            src = [SRC_Q + ((2 * mp + t) * Q_PER_KV + r) * HEADDIM for t in range(2)]
            put(OFF_Q + r * KV_DIM + mp * LANES, src[0], src[1])
    for t in range((OFF_DT - OFF_K) // LANES):
        put(OFF_K + t * LANES, SRC_K + t * LANES, SRC_K + t * LANES + HALF)
    o_ref[:, OFF_DT:OFF_DT + LANES] = jnp.concatenate(
        [half(SRC_DT), jnp.zeros((rows, HALF), F32)], axis=1).astype(BF16)
    o_ref[:, OFF_DT + LANES:PROJ_W] = jnp.zeros((rows, PROJ_W - OFF_DT - LANES), BF16)


def _w_in_layout(w, tk):
    d, k, n = w.shape
    return pl.pallas_call(
        _w_in_layout_kernel,
        grid=(d, k // tk),
        in_specs=[pl.BlockSpec((None, tk, n), lambda l, i: (l, i, 0))],
        out_specs=pl.BlockSpec((None, tk, PROJ_W), lambda l, i: (l, i, 0)),
        out_shape=jax.ShapeDtypeStruct((d, k, PROJ_W), BF16),
        compiler_params=_cparams("parallel", "parallel"),
        name="w_in_layout",
    )(w)


def _w_attn_layout(w):
    d, k, n = w.shape
    src = lambda l, t: (l, lax.rem(t, N_KV) * Q_PER_KV + t // N_KV, 0)
    return pl.pallas_call(
        _cast_kernel,
        grid=(d, N_HEADS),
        in_specs=[pl.BlockSpec((None, HEADDIM, n), src)],
        out_specs=pl.BlockSpec((None, HEADDIM, n), lambda l, t: (l, t, 0)),
        out_shape=jax.ShapeDtypeStruct(w.shape, BF16),
        compiler_params=_cparams("parallel", "parallel"),
        name="w_attn_layout",
    )(w)


def _small_params(l, norm_mix, conv_w, conv_b, dt_bias, a_log, d_skip, ssm_norm, attn_sinks, norm_ffn):
    pad = lambda v: jnp.pad(v, (0, LANES - N_SSM_HEADS)).reshape(1, LANES)
    return dict(
        norm_mix=norm_mix[l].reshape(1, D_MODEL),
        conv_w=conv_w[l], conv_b=conv_b[l].reshape(1, CONV_DIM),
        dt_bias=pad(dt_bias[l]), a_log=pad(a_log[l]),
        d_skip=jnp.repeat(d_skip[l], HEADDIM).reshape(1, D_INNER),
        ssm_norm=ssm_norm[l].reshape(1, D_INNER),
        sinks=attn_sinks[l].reshape(N_KV, Q_PER_KV).T.reshape(N_HEADS),
        norm_ffn=norm_ffn[l].reshape(1, D_MODEL))


def _dense_tail(x, y, o, proj, p, w, l, tm):
    merged = _merge(y, o, proj, w["w_ssm"], w["w_attn"], l, tm, 512)
    x1 = _matmul_res(merged, w["w_out"], l, x, tm, 1024, "out_proj")
    hmid = _ffn_up(x1, p["norm_ffn"], w["w_gate"], w["w_up"], l, tm, 512)
    return _matmul_res(hmid, w["w_down"], l, x1, tm, 512, "ffn_down")


PROMPT_TM = 1024
PROJ_TN = 1280


def kernel(x_prompt, x_sample, state_ssm, state_conv, cache_k, cache_v, norm_mix, w_in, conv_w,
           conv_b, dt_bias, a_log, d_skip, ssm_norm, w_ssm_proj, w_attn_proj, attn_sinks, w_out,
           norm_ffn, w_gate, w_up, w_down, norm_final):
    xp = x_prompt.reshape(SEQ, D_MODEL)
    xs = x_sample.reshape(DEC_BATCH, D_MODEL)
    state4 = state_ssm.reshape(DEPTH, DEC_BATCH, D_INNER, D_STATE)
    ck = cache_k.reshape(DEPTH, DEC_BATCH, WINDOW, KV_DIM)
    cv = cache_v.reshape(DEPTH, DEC_BATCH, WINDOW, KV_DIM)
    outs = {k: [] for k in ("p_ssm", "p_conv", "p_k", "p_v")}

    w = dict(w_in=_w_in_layout(w_in, 128), w_attn=_w_attn_layout(w_attn_proj),
             w_ssm=_cast_bf16(w_ssm_proj, 1024, "cast_w_ssm"),
             w_out=_cast_bf16(w_out, 1024, "cast_w_out"),
             w_gate=_cast_bf16(w_gate, 256, "cast_w_gate"),
             w_up=_cast_bf16(w_up, 256, "cast_w_up"),
             w_down=_cast_bf16(w_down, 512, "cast_w_down"))

    h_all = conv_all = kv_all = None
    for l in range(DEPTH):
        p = _small_params(l, norm_mix, conv_w, conv_b, dt_bias, a_log, d_skip, ssm_norm,
                          attn_sinks, norm_ffn)

        proj = _in_proj(xp, p["norm_mix"], w["w_in"], l, PROMPT_TM, PROJ_TN)
        y, h_p, tail = _ssd_prompt(proj, p["conv_w"], p["conv_b"], p["dt_bias"], p["a_log"],
                                   p["d_skip"], p["ssm_norm"])
        o = _attn_prompt(proj, p["sinks"])
        outs["p_ssm"].append(h_p.reshape(1, N_SSM_HEADS, HEADDIM, D_STATE))
        outs["p_conv"].append(tail[SUBLANES - (CONV_W - 1):].reshape(1, CONV_W - 1, CONV_DIM))
        outs["p_k"].append(proj[SEQ - WINDOW:, OFF_K:OFF_K + KV_DIM].reshape(1, WINDOW, N_KV, HEADDIM))
        outs["p_v"].append(proj[SEQ - WINDOW:, OFF_V:OFF_V + KV_DIM].reshape(1, WINDOW, N_KV, HEADDIM))
        xp = _dense_tail(xp, y, o, proj, p, w, l, PROMPT_TM)

        proj_s = _in_proj(xs, p["norm_mix"], w["w_in"], l, DEC_BATCH, PROJ_TN)
        xc, conv_all, dt, da = _conv_step(proj_s, state_conv, l, p["conv_w"], p["conv_b"],
                                          p["dt_bias"], p["a_log"],
                                          None if conv_all is None else (conv_all,))
        y_s, h_all = _ssm_step(xc, proj_s, dt, da, p["d_skip"], p["ssm_norm"], state4, l,
                               None if h_all is None else (h_all,))
        o_s, k_all, v_all = _attn_step(proj_s, p["sinks"].reshape(N_HEADS, 1), ck, cv, l, kv_all)
        kv_all = (k_all, v_all)
        xs = _dense_tail(xs, y_s, o_s, proj_s, p, w, l, DEC_BATCH)

    y_prompt = _rmsnorm(xp, norm_final.reshape(1, D_MODEL), PROMPT_TM).reshape(1, SEQ, D_MODEL)
    y_sample = _rmsnorm(xs, norm_final.reshape(1, D_MODEL), DEC_BATCH).reshape(DEC_BATCH, 1, D_MODEL)
    st = lambda k: jnp.stack(outs[k])
    cache_shape = (DEPTH, DEC_BATCH, WINDOW, N_KV, HEADDIM)
    return (y_prompt, y_sample, st("p_ssm"), st("p_conv"), st("p_k"), st("p_v"),
            h_all.reshape(DEPTH, DEC_BATCH, N_SSM_HEADS, HEADDIM, D_STATE), conv_all,
            kv_all[0].reshape(cache_shape), kv_all[1].reshape(cache_shape))
```

```python
import functools

import jax
import jax.numpy as jnp
from jax import lax
from jax.experimental import pallas as pl
from jax.experimental.pallas import tpu as pltpu

F32 = jnp.float32
BF16 = jnp.bfloat16

D_MODEL = 2048
SEQ = 8192
DEPTH = 2
DEC_BATCH = 128
D_INNER = 4096
HEADDIM = 64
N_SSM_HEADS = 64
D_STATE = 128
N_GROUPS = 8
GROUP_W = D_INNER // N_GROUPS
CONV_W = 4
BC_W = N_GROUPS * D_STATE
CONV_DIM = D_INNER + 2 * BC_W
CHUNK = 128
N_HEADS = 32
N_KV = 8
Q_PER_KV = 4
KV_DIM = N_KV * HEADDIM
WINDOW = 128
D_FF = 5632
EPS = 1e-6
IN_DIM = 17472

OFF_Z = 0
OFF_X = 4096
OFF_B = OFF_X + D_INNER
OFF_C = OFF_B + BC_W
OFF_Q = 10240
OFF_K = 12288
OFF_V = 12800
OFF_GS = 13312
OFF_GA = 15360
OFF_DT = 17408
PROJ_W = 17920

LANES = 128
SUBLANES = 8
VMEM_LIMIT = 56 * 1024 * 1024


def _cparams(*sem):
    return pltpu.CompilerParams(dimension_semantics=sem, vmem_limit_bytes=VMEM_LIMIT)


def _dot(a, b):
    return jnp.dot(a, b, preferred_element_type=F32)


def _dot_nt(a, b):
    return lax.dot_general(a, b, (((1,), (1,)), ((), ())), preferred_element_type=F32)


def _div_pow2(v, d):
    return jnp.right_shift(v, d.bit_length() - 1)


def _mod_pow2(v, d):
    return jnp.bitwise_and(v, d - 1)


def _split3(x):
    hi = x.astype(BF16)
    r1 = x - hi.astype(F32)
    mid = r1.astype(BF16)
    lo = (r1 - mid.astype(F32)).astype(BF16)
    return hi, mid, lo


def _dot3_r(x, sel):
    hi, mid, lo = _split3(x)
    return _dot(hi, sel) + _dot(mid, sel) + _dot(lo, sel)


def _dot3_l(sel, x):
    hi, mid, lo = _split3(x)
    return _dot(sel, hi) + _dot(sel, mid) + _dot(sel, lo)


def _silu(x):
    h = 0.5 * x
    return h + h * jnp.tanh(h)


def _softplus(x):
    return jnp.maximum(x, 0.0) + jnp.log1p(jnp.exp(-jnp.abs(x)))


def _rms_scale(x, g):
    r = lax.rsqrt(jnp.mean(x * x, axis=-1, keepdims=True) + EPS)
    return x * r * g


def _norm_matmul_kernel(x_ref, g_ref, w_ref, o_ref, xn_ref):
    @pl.when(pl.program_id(1) == 0)
    def _():
        xn_ref[...] = _rms_scale(x_ref[...], g_ref[...]).astype(BF16)

    o_ref[...] = _dot(xn_ref[...], w_ref[...])


def _in_proj(x, g, w, layer, tm, tn):
    m, k = x.shape
    n = w.shape[2]
    return pl.pallas_call(
        _norm_matmul_kernel,
        grid=(m // tm, n // tn),
        in_specs=[pl.BlockSpec((tm, k), lambda i, j: (i, 0)),
                  pl.BlockSpec((1, k), lambda i, j: (0, 0)),
                  pl.BlockSpec((None, k, tn), lambda i, j: (layer, 0, j))],
        out_specs=pl.BlockSpec((tm, tn), lambda i, j: (i, j)),
        out_shape=jax.ShapeDtypeStruct((m, n), F32),
        scratch_shapes=[pltpu.VMEM((tm, k), BF16)],
        compiler_params=_cparams("parallel", "arbitrary"),
        name="in_proj",
    )(x, g, w)


def _ffn_up_kernel(x_ref, g_ref, wg_ref, wu_ref, o_ref, xn_ref):
    @pl.when(pl.program_id(1) == 0)
    def _():
        xn_ref[...] = _rms_scale(x_ref[...], g_ref[...]).astype(BF16)

    xn = xn_ref[...]
    o_ref[...] = (_silu(_dot(xn, wg_ref[...])) * _dot(xn, wu_ref[...])).astype(BF16)


def _ffn_up(x, g, wg, wu, layer, tm, tn):
    m, k = x.shape
    n = wg.shape[2]
    return pl.pallas_call(
        _ffn_up_kernel,
        grid=(m // tm, n // tn),
        in_specs=[pl.BlockSpec((tm, k), lambda i, j: (i, 0)),
                  pl.BlockSpec((1, k), lambda i, j: (0, 0)),
                  pl.BlockSpec((None, k, tn), lambda i, j: (layer, 0, j)),
                  pl.BlockSpec((None, k, tn), lambda i, j: (layer, 0, j))],
        out_specs=pl.BlockSpec((tm, tn), lambda i, j: (i, j)),
        out_shape=jax.ShapeDtypeStruct((m, n), BF16),
        scratch_shapes=[pltpu.VMEM((tm, k), BF16)],
        compiler_params=_cparams("parallel", "arbitrary"),
        name="ffn_up",
    )(x, g, wg, wu)


def _matmul_res_kernel(a_ref, w_ref, r_ref, o_ref):
    o_ref[...] = r_ref[...] + _dot(a_ref[...], w_ref[...])


def _matmul_res(a, w, layer, res, tm, tn, name):
    m, k = a.shape
    n = w.shape[2]
    return pl.pallas_call(
        _matmul_res_kernel,
        grid=(m // tm, n // tn),
        in_specs=[pl.BlockSpec((tm, k), lambda i, j: (i, 0)),
                  pl.BlockSpec((None, k, tn), lambda i, j: (layer, 0, j)),
                  pl.BlockSpec((tm, tn), lambda i, j: (i, j))],
        out_specs=pl.BlockSpec((tm, tn), lambda i, j: (i, j)),
        out_shape=jax.ShapeDtypeStruct((m, n), F32),
        compiler_params=_cparams("parallel", "parallel"),
        name=name,
    )(a, w, res)


def _merge_kernel(y_ref, o_ref, gs_ref, ga_ref, ws_ref, wa_ref, out_ref):
    ys = _dot(y_ref[...].astype(BF16), ws_ref[...])
    oa = _dot(o_ref[...].astype(BF16), wa_ref[...])
    out_ref[...] = (jax.nn.sigmoid(gs_ref[...]) * ys
                    + jax.nn.sigmoid(ga_ref[...]) * oa).astype(BF16)


def _merge(y, o, proj, ws, wa, layer, tm, tn):
    m = y.shape[0]
    gs0, ga0 = OFF_GS // tn, OFF_GA // tn
    return pl.pallas_call(
        _merge_kernel,
        grid=(m // tm, D_MODEL // tn),
        in_specs=[pl.BlockSpec((tm, D_INNER), lambda i, j: (i, 0)),
                  pl.BlockSpec((tm, D_MODEL), lambda i, j: (i, 0)),
                  pl.BlockSpec((tm, tn), lambda i, j: (i, gs0 + j)),
                  pl.BlockSpec((tm, tn), lambda i, j: (i, ga0 + j)),
                  pl.BlockSpec((None, D_INNER, tn), lambda i, j: (layer, 0, j)),
                  pl.BlockSpec((None, D_MODEL, tn), lambda i, j: (layer, 0, j))],
        out_specs=pl.BlockSpec((tm, tn), lambda i, j: (i, j)),
        out_shape=jax.ShapeDtypeStruct((m, D_MODEL), BF16),
        compiler_params=_cparams("parallel", "parallel"),
        name="merge",
    )(y, o, proj, proj, ws, wa)


def _rmsnorm_kernel(x_ref, g_ref, o_ref):
    o_ref[...] = _rms_scale(x_ref[...], g_ref[...])


def _rmsnorm(x, g, tm):
    m, k = x.shape
    return pl.pallas_call(
        _rmsnorm_kernel,
        grid=(m // tm,),
        in_specs=[pl.BlockSpec((tm, k), lambda i: (i, 0)),
                  pl.BlockSpec((1, k), lambda i: (0, 0))],
        out_specs=pl.BlockSpec((tm, k), lambda i: (i, 0)),
        out_shape=jax.ShapeDtypeStruct((m, k), F32),
        compiler_params=_cparams("parallel"),
        name="final_norm",
    )(x, g)


def _gated_norm(y, z, gain):
    h = y * _silu(z)
    r = lax.rsqrt(jnp.mean(h * h, axis=-1, keepdims=True) + EPS)
    return h * r * gain


def _ssd_kernel(xs_ref, b_ref, c_ref, z_ref, dt_ref, cw_ref, cb_ref, dtb_ref, alog_ref,
                dsk_ref, gn_ref, y_ref, hout_ref, cout_ref, xpad_ref, xc_ref, ht_ref):
    ci = pl.program_id(0)

    @pl.when(ci == 0)
    def _():
        xpad_ref[0:SUBLANES, :] = jnp.zeros((SUBLANES, CONV_DIM), F32)
        ht_ref[...] = jnp.zeros_like(ht_ref)

    xpad_ref[SUBLANES:SUBLANES + CHUNK, 0:D_INNER] = xs_ref[...]
    xpad_ref[SUBLANES:SUBLANES + CHUNK, D_INNER:D_INNER + BC_W] = b_ref[...]
    xpad_ref[SUBLANES:SUBLANES + CHUNK, D_INNER + BC_W:CONV_DIM] = c_ref[...]

    first = SUBLANES - (CONV_W - 1)
    for s in range(CONV_DIM // GROUP_W):
        sl = slice(s * GROUP_W, (s + 1) * GROUP_W)
        acc = cb_ref[:, sl]
        for k in range(CONV_W):
            acc = acc + xpad_ref[first + k:first + k + CHUNK, sl] * cw_ref[k:k + 1, sl]
        xc_ref[:, sl] = _silu(acc)

    tail = xpad_ref[CHUNK:CHUNK + SUBLANES, :]
    xpad_ref[0:SUBLANES, :] = tail
    cout_ref[...] = tail

    dt = _softplus(dt_ref[...] + dtb_ref[...])
    a = dt * (-jnp.exp(alog_ref[...]))
    row = lax.broadcasted_iota(jnp.int32, (CHUNK, CHUNK), 0)
    col = lax.broadcasted_iota(jnp.int32, (CHUNK, CHUNK), 1)
    causal = col <= row
    tri = jnp.where(causal, 1.0, 0.0).astype(BF16)
    a_cs = _dot3_l(tri, a)
    a_cs_t = a_cs.T
    ea = jnp.exp(a_cs)
    to_end = jnp.exp(a_cs[CHUNK - 1:CHUNK, :] - a_cs)
    stack = jnp.concatenate([dt, to_end, ea], axis=0)

    lane = lax.broadcasted_iota(jnp.int32, (CHUNK, LANES), 1)
    low_half = lane < HEADDIM
    hrow = lax.broadcasted_iota(jnp.int32, (LANES, GROUP_W), 0)
    hcol = lax.broadcasted_iota(jnp.int32, (LANES, GROUP_W), 1)

    for g in range(N_GROUPS):
        sl = slice(g * GROUP_W, (g + 1) * GROUP_W)
        sel = jnp.where(hrow == g * (GROUP_W // HEADDIM) + _div_pow2(hcol, HEADDIM), 1.0, 0.0).astype(BF16)
        ex = _dot3_r(stack, sel)
        dt_e, te_e, ea_e = ex[0:CHUNK], ex[CHUNK:2 * CHUNK], ex[2 * CHUNK:3 * CHUNK]

        x_g = xc_ref[:, sl]
        b_g = xc_ref[:, D_INNER + g * D_STATE:D_INNER + (g + 1) * D_STATE]
        c_g = xc_ref[:, D_INNER + BC_W + g * D_STATE:D_INNER + BC_W + (g + 1) * D_STATE]
        xdt = x_g * dt_e
        xdt_b = xdt.astype(BF16)
        cb16, bb16 = c_g.astype(BF16), b_g.astype(BF16)
        cb = _dot_nt(cb16, bb16)

        ht_g = ht_ref[:, sl]
        y_g = _dot(cb16, ht_g.astype(BF16)) * ea_e + dsk_ref[:, sl] * x_g
        states_t = _dot(b_g.T.astype(BF16), (xdt * te_e).astype(BF16))
        ht_ref[:, sl] = ea_e[CHUNK - 1:CHUNK, :] * ht_g + states_t

        y_pairs = []
        for jj in range(GROUP_W // LANES):
            xp = xdt_b[:, jj * LANES:(jj + 1) * LANES]
            halves = []
            for hh in range(2):
                h = g * (GROUP_W // HEADDIM) + 2 * jj + hh
                seg = a_cs[:, h:h + 1] - a_cs_t[h:h + 1, :]
                decay = jnp.exp(jnp.where(causal, seg, -jnp.inf))
                halves.append(_dot((cb * decay).astype(BF16), xp))
            y_pairs.append(jnp.where(low_half, halves[0], halves[1]))
        y_g = y_g + jnp.concatenate(y_pairs, axis=1)

        y_ref[:, sl] = _gated_norm(y_g, z_ref[:, sl], gn_ref[:, sl]).astype(BF16)

    @pl.when(ci == pl.num_programs(0) - 1)
    def _():
        for i in range(D_INNER // LANES):
            hout_ref[i * LANES:(i + 1) * LANES, :] = ht_ref[:, i * LANES:(i + 1) * LANES].T


def _ssd_prompt(proj, cw, cb, dtb, alog, dsk, gn):
    nchunk = SEQ // CHUNK
    full = lambda r, c: pl.BlockSpec((r, c), lambda i: (0, 0))
    return pl.pallas_call(
        _ssd_kernel,
        grid=(nchunk,),
        in_specs=[pl.BlockSpec((CHUNK, D_INNER), lambda i: (i, OFF_X // D_INNER)),
                  pl.BlockSpec((CHUNK, BC_W), lambda i: (i, OFF_B // BC_W)),
                  pl.BlockSpec((CHUNK, BC_W), lambda i: (i, OFF_C // BC_W)),
                  pl.BlockSpec((CHUNK, D_INNER), lambda i: (i, OFF_Z // D_INNER)),
                  pl.BlockSpec((CHUNK, LANES), lambda i: (i, OFF_DT // LANES)),
                  full(CONV_W, CONV_DIM), full(1, CONV_DIM), full(1, LANES), full(1, LANES),
                  full(1, D_INNER), full(1, D_INNER)],
        out_specs=[pl.BlockSpec((CHUNK, D_INNER), lambda i: (i, 0)),
                   pl.BlockSpec((D_INNER, D_STATE), lambda i: (0, 0)),
                   pl.BlockSpec((SUBLANES, CONV_DIM), lambda i: (0, 0))],
        out_shape=[jax.ShapeDtypeStruct((SEQ, D_INNER), BF16),
                   jax.ShapeDtypeStruct((D_INNER, D_STATE), F32),
                   jax.ShapeDtypeStruct((SUBLANES, CONV_DIM), F32)],
        scratch_shapes=[pltpu.VMEM((CHUNK + SUBLANES, CONV_DIM), F32),
                        pltpu.VMEM((CHUNK, CONV_DIM), F32),
                        pltpu.VMEM((D_STATE, D_INNER), F32)],
        compiler_params=_cparams("arbitrary"),
        name="ssd_prompt",
    )(proj, proj, proj, proj, proj, cw, cb, dtb, alog, dsk, gn)


def _attn_kernel(sink_ref, q_ref, kp_ref, kc_ref, vp_ref, vc_ref, o_ref):
    i = pl.program_id(0)
    band = 2 * CHUNK
    l_idx = lax.broadcasted_iota(jnp.int32, (CHUNK, 2 * band), 0)
    s_idx = _mod_pow2(lax.broadcasted_iota(jnp.int32, (CHUNK, 2 * band), 1), band)
    first_key = jnp.where(i > 0, 0, CHUNK)
    mask = (s_idx >= jnp.maximum(l_idx, first_key)) & (s_idx <= l_idx + WINDOW)
    low_half = lax.broadcasted_iota(jnp.int32, (band, LANES), 1) < HEADDIM

    def softmax_sink(s, sink):
        m = jnp.maximum(jnp.max(s, axis=-1, keepdims=True), sink)
        p = jnp.exp(s - m)
        den = jnp.sum(p, axis=-1, keepdims=True) + jnp.exp(sink - m)
        return (p / den).astype(BF16)

    def stack_halves(t2, t2r, u):
        lo_src, hi_src = (t2, t2r) if u == 0 else (t2r, t2)
        return jnp.concatenate([jnp.where(low_half, lo_src, 0.0), jnp.where(low_half, 0.0, hi_src)],
                               axis=0).astype(BF16)

    for c in range(KV_DIM // LANES):
        csl = slice(c * LANES, (c + 1) * LANES)
        k2 = jnp.concatenate([kp_ref[:, csl], kc_ref[:, csl]], axis=0)
        v2 = jnp.concatenate([vp_ref[:, csl], vc_ref[:, csl]], axis=0)
        k2r = pltpu.roll(k2, HEADDIM, axis=1)
        v2r = pltpu.roll(v2, HEADDIM, axis=1)
        for u in range(2):
            kv_head = 2 * c + u
            kst = stack_halves(k2, k2r, u)
            vst = stack_halves(v2, v2r, u)
            for jj in range(Q_PER_KV // 2):
                j = kv_head * (Q_PER_KV // 2) + jj
                q2 = (q_ref[:, j * LANES:(j + 1) * LANES] * (HEADDIM ** -0.5)).astype(BF16)
                s = jnp.where(mask, _dot_nt(q2, kst), -jnp.inf)
                pr = jnp.concatenate([softmax_sink(s[:, :band], sink_ref[2 * j]),
                                      softmax_sink(s[:, band:], sink_ref[2 * j + 1])], axis=1)
                o_ref[:, j * LANES:(j + 1) * LANES] = _dot(pr, vst).astype(BF16)


def _attn_prompt(proj, sinks):
    nb = SEQ // CHUNK
    prev = lambda i: jnp.maximum(i - 1, 0)
    return pl.pallas_call(
        _attn_kernel,
        grid=(nb,),
        in_specs=[pl.BlockSpec(memory_space=pltpu.SMEM),
                  pl.BlockSpec((CHUNK, D_MODEL), lambda i: (i, OFF_Q // D_MODEL)),
                  pl.BlockSpec((CHUNK, KV_DIM), lambda i: (prev(i), OFF_K // KV_DIM)),
                  pl.BlockSpec((CHUNK, KV_DIM), lambda i: (i, OFF_K // KV_DIM)),
                  pl.BlockSpec((CHUNK, KV_DIM), lambda i: (prev(i), OFF_V // KV_DIM)),
                  pl.BlockSpec((CHUNK, KV_DIM), lambda i: (i, OFF_V // KV_DIM))],
        out_specs=pl.BlockSpec((CHUNK, D_MODEL), lambda i: (i, 0)),
        out_shape=jax.ShapeDtypeStruct((SEQ, D_MODEL), BF16),
        compiler_params=_cparams("parallel"),
        name="attn_prompt",
    )(sinks, proj, proj, proj, proj, proj)


def _alias_prev(prev, n_in, first_out):
    if prev is None:
        return [], {}, ()
    specs = [pl.BlockSpec(memory_space=pl.ANY)] * len(prev)
    return specs, {n_in + t: first_out + t for t in range(len(prev))}, tuple(prev)


def _conv_step_kernel(x_ref, dt_ref, cs_ref, cw_ref, cb_ref, dtb_ref, alog_ref, *rest):
    xc_ref, csn_ref, dto_ref, dao_ref = rest[-4:]
    taps = [cs_ref[k] for k in range(CONV_W - 1)] + [x_ref[...]]
    acc = cb_ref[...]
    for k in range(CONV_W):
        acc = acc + taps[k] * cw_ref[k:k + 1, :]
    xc_ref[...] = _silu(acc)
    for k in range(CONV_W - 1):
        csn_ref[k] = taps[k + 1]
    dt = _softplus(dt_ref[...] + dtb_ref[...])
    dto_ref[...] = dt
    dao_ref[...] = jnp.exp(dt * (-jnp.exp(alog_ref[...])))


def _conv_step(proj, conv_state, layer, cw, cb, dtb, alog, prev):
    nb = DEC_BATCH
    small = pl.BlockSpec((1, LANES), lambda s: (0, 0))
    head = pl.BlockSpec((nb, LANES), lambda s: (0, 0))
    state = pl.BlockSpec((None, CONV_W - 1, nb, GROUP_W), lambda s: (layer, 0, 0, s))
    in_specs = [pl.BlockSpec((nb, GROUP_W), lambda s: (0, OFF_X // GROUP_W + s)),
                pl.BlockSpec((nb, LANES), lambda s: (0, OFF_DT // LANES)),
                state,
                pl.BlockSpec((CONV_W, GROUP_W), lambda s: (0, s)),
                pl.BlockSpec((1, GROUP_W), lambda s: (0, s)),
                small, small]
    alias_specs, aliases, alias_args = _alias_prev(prev, len(in_specs), 1)
    return pl.pallas_call(
        _conv_step_kernel,
        grid=(CONV_DIM // GROUP_W,),
        in_specs=in_specs + alias_specs,
        out_specs=[pl.BlockSpec((nb, GROUP_W), lambda s: (0, s)), state, head, head],
        out_shape=[jax.ShapeDtypeStruct((nb, CONV_DIM), F32),
                   jax.ShapeDtypeStruct(conv_state.shape, F32),
                   jax.ShapeDtypeStruct((nb, LANES), F32),
                   jax.ShapeDtypeStruct((nb, LANES), F32)],
        input_output_aliases=aliases,
        compiler_params=_cparams("arbitrary"),
        name="conv_step",
    )(proj, proj, conv_state, cw, cb, dtb, alog, *alias_args)


STEP_B = 8
STEP_GROUPS = 4
STEP_W = STEP_GROUPS * GROUP_W


def _ssm_step_kernel(xs_ref, b_ref, c_ref, z_ref, dt_ref, da_ref, dsk_ref, gn_ref, h_ref, *rest):
    y_ref, hn_ref = rest[-2:]
    part = pl.program_id(1)
    xs = xs_ref[...]
    hrow = lax.broadcasted_iota(jnp.int32, (LANES, STEP_W), 0)
    hcol = lax.broadcasted_iota(jnp.int32, (LANES, STEP_W), 1)
    sel = jnp.where(hrow == part * (STEP_W // HEADDIM) + _div_pow2(hcol, HEADDIM), 1.0, 0.0).astype(BF16)
    dt_e = _dot3_r(dt_ref[...], sel)
    da_e = _dot3_r(da_ref[...], sel)
    xdt = xs * dt_e
    lane_group = _div_pow2(lax.broadcasted_iota(jnp.int32, (STEP_B, STEP_W), 1), GROUP_W)

    pieces = [jnp.where(lane_group == g, xdt, 0.0) for g in range(STEP_GROUPS)]
    pieces += [p.astype(F32) for p in _split3(da_e)]
    n_used = len(pieces) * STEP_B
    pieces.append(jnp.zeros((LANES - n_used, STEP_W), F32))
    ut = jnp.concatenate(pieces, axis=0)
    u_big = jnp.concatenate([ut[:, i * LANES:(i + 1) * LANES].T for i in range(STEP_W // LANES)],
                            axis=0).astype(BF16)

    b_rows = jnp.concatenate([b_ref[:, g * D_STATE:(g + 1) * D_STATE] for g in range(STEP_GROUPS)]
                             + [jnp.zeros((LANES - STEP_GROUPS * STEP_B, D_STATE), F32)], axis=0)
    rrow = lax.broadcasted_iota(jnp.int32, (LANES, D_STATE), 0)
    ones_rows = jnp.where((rrow >= STEP_GROUPS * STEP_B) & (rrow < n_used), 1.0, 0.0)
    big_r = jnp.concatenate([b_rows, ones_rows], axis=1)
    r_seq = _mod_pow2(lax.broadcasted_iota(jnp.int32, (LANES, 2 * D_STATE), 0), STEP_B)

    cc = jnp.concatenate([c_ref[:, g * D_STATE:(g + 1) * D_STATE] for g in range(STEP_GROUPS)],
                         axis=0).astype(BF16)
    crow = lax.broadcasted_iota(jnp.int32, (STEP_GROUPS * STEP_B, STEP_W), 0)
    ccol_group = _div_pow2(lax.broadcasted_iota(jnp.int32, (STEP_GROUPS * STEP_B, STEP_W), 1), GROUP_W)
    sub = lax.broadcasted_iota(jnp.int32, (STEP_B, STEP_W), 0)

    y_off = jnp.zeros((STEP_B, STEP_W), F32)
    for b in range(STEP_B):
        h0 = h_ref[b]
        res = _dot(u_big, jnp.where(r_seq == b, big_r, 0.0).astype(BF16))
        hn_ref[b] = res[:, D_STATE:] * h0 + res[:, :D_STATE]
        y_all = _dot_nt(cc, h0.astype(BF16))
        pick = (crow == ccol_group * STEP_B + b)
        y_b = jnp.sum(jnp.where(pick, y_all, 0.0), axis=0, keepdims=True)
        y_off = jnp.where(sub == b, y_b, y_off)

    cb_e = jnp.zeros((STEP_B, STEP_W), F32)
    for g in range(STEP_GROUPS):
        gsl = slice(g * D_STATE, (g + 1) * D_STATE)
        cb_g = jnp.sum(c_ref[:, gsl] * b_ref[:, gsl], axis=-1, keepdims=True)
        cb_e = jnp.where(lane_group == g, cb_g, cb_e)

    y = da_e * y_off + cb_e * xdt + dsk_ref[...] * xs
    for g in range(STEP_GROUPS):
        sl = slice(g * GROUP_W, (g + 1) * GROUP_W)
        y_ref[:, sl] = _gated_norm(y[:, sl], z_ref[:, sl], gn_ref[:, sl])


def _ssm_step(xc, proj, dt, da, dsk, gn, state, layer, prev):
    nparts = D_INNER // STEP_W
    state_spec = pl.BlockSpec((None, STEP_B, STEP_W, D_STATE), lambda i, p: (layer, i, p, 0))
    in_specs = [pl.BlockSpec((STEP_B, STEP_W), lambda i, p: (i, p)),
                pl.BlockSpec((STEP_B, STEP_GROUPS * D_STATE),
                             lambda i, p: (i, D_INNER // (STEP_GROUPS * D_STATE) + p)),
                pl.BlockSpec((STEP_B, STEP_GROUPS * D_STATE),
                             lambda i, p: (i, (D_INNER + BC_W) // (STEP_GROUPS * D_STATE) + p)),
                pl.BlockSpec((STEP_B, STEP_W), lambda i, p: (i, OFF_Z // STEP_W + p)),
                pl.BlockSpec((STEP_B, LANES), lambda i, p: (i, 0)),
                pl.BlockSpec((STEP_B, LANES), lambda i, p: (i, 0)),
                pl.BlockSpec((1, STEP_W), lambda i, p: (0, p)),
                pl.BlockSpec((1, STEP_W), lambda i, p: (0, p)),
                state_spec]
    alias_specs, aliases, alias_args = _alias_prev(prev, len(in_specs), 1)
    return pl.pallas_call(
        _ssm_step_kernel,
        grid=(DEC_BATCH // STEP_B, nparts),
        in_specs=in_specs + alias_specs,
        out_specs=[pl.BlockSpec((STEP_B, STEP_W), lambda i, p: (i, p)), state_spec],
        out_shape=[jax.ShapeDtypeStruct((DEC_BATCH, D_INNER), F32),
                   jax.ShapeDtypeStruct(state.shape, F32)],
        input_output_aliases=aliases,
        compiler_params=_cparams("parallel", "parallel"),
        name="ssm_step",
    )(xc, xc, xc, proj, dt, da, dsk, gn, state, *alias_args)


ATTN_B = 8


def _attn_step_kernel(sink_ref, q_ref, kn_ref, vn_ref, kc_ref, vc_ref, *rest):
    o_ref, ko_ref, vo_ref = rest[-3:]
    lane_kv = _div_pow2(lax.broadcasted_iota(jnp.int32, (N_KV, KV_DIM), 1), HEADDIM)
    own = lane_kv == lax.broadcasted_iota(jnp.int32, (N_KV, KV_DIM), 0)
    newest = lax.broadcasted_iota(jnp.int32, (KV_DIM, WINDOW), 1) == WINDOW - 1
    sink = sink_ref[...]
    pad = jnp.zeros((LANES - ATTN_B, KV_DIM), F32)
    kn_t = jnp.concatenate([kn_ref[...], pad], axis=0).T
    vn_t = jnp.concatenate([vn_ref[...], pad], axis=0).T
    for b in range(ATTN_B):
        qrow = q_ref[b:b + 1, :] * (HEADDIM ** -0.5)
        qm = jnp.concatenate(
            [jnp.where(own, jnp.broadcast_to(qrow[:, r * KV_DIM:(r + 1) * KV_DIM], (N_KV, KV_DIM)), 0.0)
             for r in range(Q_PER_KV)], axis=0)
        k_new = kn_ref[b:b + 1, :]
        v_new = vn_ref[b:b + 1, :]
        kt = kc_ref[b]
        vt = vc_ref[b]
        s = _dot(qm.astype(BF16), kt.astype(BF16))
        s_new = jnp.sum(qm * k_new, axis=-1, keepdims=True)
        m = jnp.maximum(jnp.maximum(jnp.max(s, axis=-1, keepdims=True), s_new), sink)
        p = jnp.exp(s - m)
        p_new = jnp.exp(s_new - m)
        den = jnp.sum(p, axis=-1, keepdims=True) + p_new + jnp.exp(sink - m)
        o_full = _dot_nt((p / den).astype(BF16), vt.astype(BF16)) + (p_new / den) * v_new
        parts = []
        for r in range(Q_PER_KV):
            o_r = jnp.where(own, o_full[r * N_KV:(r + 1) * N_KV, :], 0.0)
            parts.append(jnp.sum(o_r, axis=0, keepdims=True))
        o_ref[b:b + 1, :] = jnp.concatenate(parts, axis=1)
        ko_ref[b] = jnp.where(newest, jnp.broadcast_to(kn_t[:, b:b + 1], (KV_DIM, WINDOW)),
                              pltpu.roll(kt, WINDOW - 1, axis=1))
        vo_ref[b] = jnp.where(newest, jnp.broadcast_to(vn_t[:, b:b + 1], (KV_DIM, WINDOW)),
                              pltpu.roll(vt, WINDOW - 1, axis=1))


def _attn_step(q, proj, sinks, cache_k, cache_v, layer, prev):
    cache_spec = pl.BlockSpec((None, ATTN_B, KV_DIM, WINDOW), lambda i: (layer, i, 0, 0))
    in_specs = [pl.BlockSpec((N_HEADS, 1), lambda i: (0, 0)),
                pl.BlockSpec((ATTN_B, D_MODEL), lambda i: (i, 0)),
                pl.BlockSpec((ATTN_B, KV_DIM), lambda i: (i, OFF_K // KV_DIM)),
                pl.BlockSpec((ATTN_B, KV_DIM), lambda i: (i, OFF_V // KV_DIM)),
                cache_spec, cache_spec]
    alias_specs, aliases, alias_args = _alias_prev(prev, len(in_specs), 1)
    return pl.pallas_call(
        _attn_step_kernel,
        grid=(DEC_BATCH // ATTN_B,),
        in_specs=in_specs + alias_specs,
        out_specs=[pl.BlockSpec((ATTN_B, D_MODEL), lambda i: (i, 0)), cache_spec, cache_spec],
        out_shape=[jax.ShapeDtypeStruct((DEC_BATCH, D_MODEL), F32),
                   jax.ShapeDtypeStruct(cache_k.shape, F32),
                   jax.ShapeDtypeStruct(cache_v.shape, F32)],
        input_output_aliases=aliases,
        compiler_params=_cparams("parallel"),
        name="attn_step",
    )(sinks, q, proj, proj, cache_k, cache_v, *alias_args)


def _cast_kernel(x_ref, o_ref):
    o_ref[...] = x_ref[...].astype(BF16)


def _cast_bf16(w, tk, name):
    d, k, n = w.shape
    spec = pl.BlockSpec((None, tk, n), lambda l, i: (l, i, 0))
    return pl.pallas_call(
        _cast_kernel,
        grid=(d, k // tk),
        in_specs=[spec],
        out_specs=spec,
        out_shape=jax.ShapeDtypeStruct(w.shape, BF16),
        compiler_params=_cparams("parallel", "parallel"),
        name=name,
    )(w)


SRC_DT = 10240
W_TILE = 512
N_W_TILES = PROJ_W // W_TILE
FIRST_SHIFTED = SRC_DT // W_TILE


def _w_in_layout_kernel(x_ref, o_ref):
    t = pl.program_id(1)
    valid = jnp.where(t == N_W_TILES - 1, N_SSM_HEADS, W_TILE)
    row = lax.broadcasted_iota(jnp.int32, (W_TILE, D_MODEL), 0)
    o_ref[...] = jnp.where(row < valid, x_ref[0], 0.0).T.astype(BF16)


def _w_in_layout(w_t):
    d = w_t.shape[0]

    def src_row(l, t):
        per_tile = W_TILE // N_SSM_HEADS
        unit = jnp.where(t < FIRST_SHIFTED, t * per_tile,
                         jnp.where(t < N_W_TILES - 1, t * per_tile + 1, SRC_DT // N_SSM_HEADS))
        return (l, unit * N_SSM_HEADS, 0)

    return pl.pallas_call(
        _w_in_layout_kernel,
        grid=(d, N_W_TILES),
        in_specs=[pl.BlockSpec((pl.Element(1), pl.Element(W_TILE), pl.Element(D_MODEL)), src_row)],
        out_specs=pl.BlockSpec((None, D_MODEL, W_TILE), lambda l, t: (l, 0, t)),
        out_shape=jax.ShapeDtypeStruct((d, D_MODEL, PROJ_W), BF16),
        compiler_params=_cparams("parallel", "parallel"),
        name="w_in_layout",
    )(w_t)


def _small_params(l, norm_mix, conv_w, conv_b, dt_bias, a_log, d_skip, ssm_norm, attn_sinks, norm_ffn):
    pad = lambda v: jnp.pad(v, (0, LANES - N_SSM_HEADS)).reshape(1, LANES)
    return dict(
        norm_mix=norm_mix[l].reshape(1, D_MODEL),
        conv_w=conv_w[l], conv_b=conv_b[l].reshape(1, CONV_DIM),
        dt_bias=pad(dt_bias[l]), a_log=pad(a_log[l]),
        d_skip=jnp.repeat(d_skip[l], HEADDIM).reshape(1, D_INNER),
        ssm_norm=ssm_norm[l].reshape(1, D_INNER),
        sinks=attn_sinks[l],
        sinks_rm=attn_sinks[l].reshape(N_KV, Q_PER_KV).T.reshape(N_HEADS, 1),
        norm_ffn=norm_ffn[l].reshape(1, D_MODEL))


def _dense_tail(x, y, o, proj, p, w, l, tm):
    merged = _merge(y, o, proj, w["w_ssm"], w["w_attn"], l, tm, 512)
    x1 = _matmul_res(merged, w["w_out"], l, x, tm, 1024, "out_proj")
    hmid = _ffn_up(x1, p["norm_ffn"], w["w_gate"], w["w_up"], l, tm, 512)
    return _matmul_res(hmid, w["w_down"], l, x1, tm, 512, "ffn_down")


PROMPT_TM = 1024
PROJ_TN = 1280


def kernel(x_prompt, x_sample, state_ssm, state_conv, cache_k, cache_v, norm_mix, w_in, conv_w,
           conv_b, dt_bias, a_log, d_skip, ssm_norm, w_ssm_proj, w_attn_proj, attn_sinks, w_out,
           norm_ffn, w_gate, w_up, w_down, norm_final):
    xp = x_prompt.reshape(SEQ, D_MODEL)
    xs = x_sample.reshape(DEC_BATCH, D_MODEL)
    state4 = state_ssm.reshape(DEPTH, DEC_BATCH, D_INNER, D_STATE)
    to_kt = lambda c: c.transpose(0, 1, 3, 4, 2).reshape(DEPTH, DEC_BATCH, KV_DIM, WINDOW)
    ck, cv = to_kt(cache_k), to_kt(cache_v)
    conv_t = state_conv.transpose(0, 2, 1, 3)
    outs = {k: [] for k in ("p_ssm", "p_conv", "p_k", "p_v")}

    w = dict(w_in=_w_in_layout(w_in.transpose(0, 2, 1)),
             w_attn=_cast_bf16(w_attn_proj, 1024, "cast_w_attn"),
             w_ssm=_cast_bf16(w_ssm_proj, 1024, "cast_w_ssm"),
             w_out=_cast_bf16(w_out, 1024, "cast_w_out"),
             w_gate=_cast_bf16(w_gate, 256, "cast_w_gate"),
             w_up=_cast_bf16(w_up, 256, "cast_w_up"),
             w_down=_cast_bf16(w_down, 512, "cast_w_down"))

    h_all = conv_all = kv_all = None
    for l in range(DEPTH):
        p = _small_params(l, norm_mix, conv_w, conv_b, dt_bias, a_log, d_skip, ssm_norm,
                          attn_sinks, norm_ffn)

        proj = _in_proj(xp, p["norm_mix"], w["w_in"], l, PROMPT_TM, PROJ_TN)
        y, h_p, tail = _ssd_prompt(proj, p["conv_w"], p["conv_b"], p["dt_bias"], p["a_log"],
                                   p["d_skip"], p["ssm_norm"])
        o = _attn_prompt(proj, p["sinks"])
        outs["p_ssm"].append(h_p.reshape(1, N_SSM_HEADS, HEADDIM, D_STATE))
        outs["p_conv"].append(tail[SUBLANES - (CONV_W - 1):].reshape(1, CONV_W - 1, CONV_DIM))
        outs["p_k"].append(proj[SEQ - WINDOW:, OFF_K:OFF_K + KV_DIM].reshape(1, WINDOW, N_KV, HEADDIM))
        outs["p_v"].append(proj[SEQ - WINDOW:, OFF_V:OFF_V + KV_DIM].reshape(1, WINDOW, N_KV, HEADDIM))
        xp = _dense_tail(xp, y, o, proj, p, w, l, PROMPT_TM)

        proj_s = _in_proj(xs, p["norm_mix"], w["w_in"], l, DEC_BATCH, PROJ_TN)
        xc, conv_all, dt, da = _conv_step(proj_s, conv_t, l, p["conv_w"], p["conv_b"],
                                          p["dt_bias"], p["a_log"],
                                          None if conv_all is None else (conv_all,))
        y_s, h_all = _ssm_step(xc, proj_s, dt, da, p["d_skip"], p["ssm_norm"], state4, l,
                               None if h_all is None else (h_all,))
        swap_rm = lambda t, a, b: t.reshape(DEC_BATCH, a, b, HEADDIM).transpose(0, 2, 1, 3).reshape(
            DEC_BATCH, D_MODEL)
        q_s = swap_rm(proj_s[:, OFF_Q:OFF_Q + D_MODEL], N_KV, Q_PER_KV)
        o_s, k_all, v_all = _attn_step(q_s, proj_s, p["sinks_rm"], ck, cv, l, kv_all)
        o_s = swap_rm(o_s, Q_PER_KV, N_KV)
        kv_all = (k_all, v_all)
        xs = _dense_tail(xs, y_s, o_s, proj_s, p, w, l, DEC_BATCH)

    y_prompt = _rmsnorm(xp, norm_final.reshape(1, D_MODEL), PROMPT_TM).reshape(1, SEQ, D_MODEL)
    y_sample = _rmsnorm(xs, norm_final.reshape(1, D_MODEL), DEC_BATCH).reshape(DEC_BATCH, 1, D_MODEL)
    st = lambda k: jnp.stack(outs[k])
    from_kt = lambda c: c.reshape(DEPTH, DEC_BATCH, N_KV, HEADDIM, WINDOW).transpose(0, 1, 4, 2, 3)
    return (y_prompt, y_sample, st("p_ssm"), st("p_conv"), st("p_k"), st("p_v"),
            h_all.reshape(DEPTH, DEC_BATCH, N_SSM_HEADS, HEADDIM, D_STATE),
            conv_all.transpose(0, 2, 1, 3), from_kt(kv_all[0]), from_kt(kv_all[1]))
```

```python
import functools

import jax
import jax.numpy as jnp
from jax import lax
from jax.experimental import pallas as pl
from jax.experimental.pallas import tpu as pltpu

F32 = jnp.float32
BF16 = jnp.bfloat16

D_MODEL = 2048
SEQ = 8192
DEPTH = 2
DEC_BATCH = 128
D_INNER = 4096
HEADDIM = 64
N_SSM_HEADS = 64
D_STATE = 128
N_GROUPS = 8
GROUP_W = D_INNER // N_GROUPS
CONV_W = 4
BC_W = N_GROUPS * D_STATE
CONV_DIM = D_INNER + 2 * BC_W
CHUNK = 128
N_HEADS = 32
N_KV = 8
Q_PER_KV = 4
KV_DIM = N_KV * HEADDIM
WINDOW = 128
D_FF = 5632
EPS = 1e-6
IN_DIM = 17472

OFF_Z = 0
OFF_X = 4096
OFF_B = OFF_X + D_INNER
OFF_C = OFF_B + BC_W
OFF_Q = 10240
OFF_K = 12288
OFF_V = 12800
OFF_GS = 13312
OFF_GA = 15360
OFF_DT = 17408
PROJ_W = 17920

LANES = 128
SUBLANES = 8
VMEM_LIMIT = 56 * 1024 * 1024


def _cparams(*sem):
    return pltpu.CompilerParams(dimension_semantics=sem, vmem_limit_bytes=VMEM_LIMIT)


def _dot(a, b):
    return jnp.dot(a, b, preferred_element_type=F32)


def _dot_nt(a, b):
    return lax.dot_general(a, b, (((1,), (1,)), ((), ())), preferred_element_type=F32)


def _div_pow2(v, d):
    return jnp.right_shift(v, d.bit_length() - 1)


def _mod_pow2(v, d):
    return jnp.bitwise_and(v, d - 1)


def _split3(x):
    hi = x.astype(BF16)
    r1 = x - hi.astype(F32)
    mid = r1.astype(BF16)
    lo = (r1 - mid.astype(F32)).astype(BF16)
    return hi, mid, lo


def _dot3_r(x, sel):
    hi, mid, lo = _split3(x)
    return _dot(hi, sel) + _dot(mid, sel) + _dot(lo, sel)


def _dot3_l(sel, x):
    hi, mid, lo = _split3(x)
    return _dot(sel, hi) + _dot(sel, mid) + _dot(sel, lo)


def _silu(x):
    h = 0.5 * x
    return h + h * jnp.tanh(h)


def _softplus(x):
    return jnp.maximum(x, 0.0) + jnp.log1p(jnp.exp(-jnp.abs(x)))


def _rms_scale(x, g):
    r = lax.rsqrt(jnp.mean(x * x, axis=-1, keepdims=True) + EPS)
    return x * r * g


def _norm_matmul_kernel(x_ref, g_ref, w_ref, o_ref, xn_ref):
    @pl.when(pl.program_id(1) == 0)
    def _():
        xn_ref[...] = _rms_scale(x_ref[...], g_ref[...]).astype(BF16)

    o_ref[...] = _dot(xn_ref[...], w_ref[...])


def _in_proj(x, g, w, layer, tm, tn):
    m, k = x.shape
    n = w.shape[2]
    return pl.pallas_call(
        _norm_matmul_kernel,
        grid=(m // tm, n // tn),
        in_specs=[pl.BlockSpec((tm, k), lambda i, j: (i, 0)),
                  pl.BlockSpec((1, k), lambda i, j: (0, 0)),
                  pl.BlockSpec((None, k, tn), lambda i, j: (layer, 0, j))],
        out_specs=pl.BlockSpec((tm, tn), lambda i, j: (i, j)),
        out_shape=jax.ShapeDtypeStruct((m, n), F32),
        scratch_shapes=[pltpu.VMEM((tm, k), BF16)],
        compiler_params=_cparams("parallel", "arbitrary"),
        name="in_proj",
    )(x, g, w)


def _ffn_up_kernel(x_ref, g_ref, wg_ref, wu_ref, o_ref, xn_ref):
    @pl.when(pl.program_id(1) == 0)
    def _():
        xn_ref[...] = _rms_scale(x_ref[...], g_ref[...]).astype(BF16)

    xn = xn_ref[...]
    o_ref[...] = (_silu(_dot(xn, wg_ref[...])) * _dot(xn, wu_ref[...])).astype(BF16)


def _ffn_up(x, g, wg, wu, layer, tm, tn):
    m, k = x.shape
    n = wg.shape[2]
    return pl.pallas_call(
        _ffn_up_kernel,
        grid=(m // tm, n // tn),
        in_specs=[pl.BlockSpec((tm, k), lambda i, j: (i, 0)),
                  pl.BlockSpec((1, k), lambda i, j: (0, 0)),
                  pl.BlockSpec((None, k, tn), lambda i, j: (layer, 0, j)),
                  pl.BlockSpec((None, k, tn), lambda i, j: (layer, 0, j))],
        out_specs=pl.BlockSpec((tm, tn), lambda i, j: (i, j)),
        out_shape=jax.ShapeDtypeStruct((m, n), BF16),
        scratch_shapes=[pltpu.VMEM((tm, k), BF16)],
        compiler_params=_cparams("parallel", "arbitrary"),
        name="ffn_up",
    )(x, g, wg, wu)


def _matmul_res_kernel(a_ref, w_ref, r_ref, o_ref):
    o_ref[...] = r_ref[...] + _dot(a_ref[...], w_ref[...])


def _matmul_res(a, w, layer, res, tm, tn, name):
    m, k = a.shape
    n = w.shape[2]
    return pl.pallas_call(
        _matmul_res_kernel,
        grid=(m // tm, n // tn),
        in_specs=[pl.BlockSpec((tm, k), lambda i, j: (i, 0)),
                  pl.BlockSpec((None, k, tn), lambda i, j: (layer, 0, j)),
                  pl.BlockSpec((tm, tn), lambda i, j: (i, j))],
        out_specs=pl.BlockSpec((tm, tn), lambda i, j: (i, j)),
        out_shape=jax.ShapeDtypeStruct((m, n), F32),
        compiler_params=_cparams("parallel", "parallel"),
        name=name,
    )(a, w, res)


def _merge_kernel(y_ref, o_ref, gs_ref, ga_ref, ws_ref, wa_ref, out_ref):
    ys = _dot(y_ref[...].astype(BF16), ws_ref[...])
    oa = _dot(o_ref[...].astype(BF16), wa_ref[...])
    out_ref[...] = (jax.nn.sigmoid(gs_ref[...]) * ys
                    + jax.nn.sigmoid(ga_ref[...]) * oa).astype(BF16)


def _merge(y, o, proj, ws, wa, layer, tm, tn):
    m = y.shape[0]
    gs0, ga0 = OFF_GS // tn, OFF_GA // tn
    return pl.pallas_call(
        _merge_kernel,
        grid=(m // tm, D_MODEL // tn),
        in_specs=[pl.BlockSpec((tm, D_INNER), lambda i, j: (i, 0)),
                  pl.BlockSpec((tm, D_MODEL), lambda i, j: (i, 0)),
                  pl.BlockSpec((tm, tn), lambda i, j: (i, gs0 + j)),
                  pl.BlockSpec((tm, tn), lambda i, j: (i, ga0 + j)),
                  pl.BlockSpec((None, D_INNER, tn), lambda i, j: (layer, 0, j)),
                  pl.BlockSpec((None, D_MODEL, tn), lambda i, j: (layer, 0, j))],
        out_specs=pl.BlockSpec((tm, tn), lambda i, j: (i, j)),
        out_shape=jax.ShapeDtypeStruct((m, D_MODEL), BF16),
        compiler_params=_cparams("parallel", "parallel"),
        name="merge",
    )(y, o, proj, proj, ws, wa)


def _rmsnorm_kernel(x_ref, g_ref, o_ref):
    o_ref[...] = _rms_scale(x_ref[...], g_ref[...])


def _rmsnorm(x, g, tm):
    m, k = x.shape
    return pl.pallas_call(
        _rmsnorm_kernel,
        grid=(m // tm,),
        in_specs=[pl.BlockSpec((tm, k), lambda i: (i, 0)),
                  pl.BlockSpec((1, k), lambda i: (0, 0))],
        out_specs=pl.BlockSpec((tm, k), lambda i: (i, 0)),
        out_shape=jax.ShapeDtypeStruct((m, k), F32),
        compiler_params=_cparams("parallel"),
        name="final_norm",
    )(x, g)


def _gated_norm(y, z, gain):
    h = y * _silu(z)
    r = lax.rsqrt(jnp.mean(h * h, axis=-1, keepdims=True) + EPS)
    return h * r * gain


def _ssd_kernel(xs_ref, b_ref, c_ref, z_ref, dt_ref, cw_ref, cb_ref, dtb_ref, alog_ref,
                dsk_ref, gn_ref, y_ref, hout_ref, cout_ref, xpad_ref, xc_ref, ht_ref):
    ci = pl.program_id(0)

    @pl.when(ci == 0)
    def _():
        xpad_ref[0:SUBLANES, :] = jnp.zeros((SUBLANES, CONV_DIM), F32)
        ht_ref[...] = jnp.zeros_like(ht_ref)

    xpad_ref[SUBLANES:SUBLANES + CHUNK, 0:D_INNER] = xs_ref[...]
    xpad_ref[SUBLANES:SUBLANES + CHUNK, D_INNER:D_INNER + BC_W] = b_ref[...]
    xpad_ref[SUBLANES:SUBLANES + CHUNK, D_INNER + BC_W:CONV_DIM] = c_ref[...]

    first = SUBLANES - (CONV_W - 1)
    for s in range(CONV_DIM // GROUP_W):
        sl = slice(s * GROUP_W, (s + 1) * GROUP_W)
        acc = cb_ref[:, sl]
        for k in range(CONV_W):
            acc = acc + xpad_ref[first + k:first + k + CHUNK, sl] * cw_ref[k:k + 1, sl]
        xc_ref[:, sl] = _silu(acc)

    tail = xpad_ref[CHUNK:CHUNK + SUBLANES, :]
    xpad_ref[0:SUBLANES, :] = tail
    cout_ref[...] = tail

    dt = _softplus(dt_ref[...] + dtb_ref[...])
    a = dt * (-jnp.exp(alog_ref[...]))
    row = lax.broadcasted_iota(jnp.int32, (CHUNK, CHUNK), 0)
    col = lax.broadcasted_iota(jnp.int32, (CHUNK, CHUNK), 1)
    causal = col <= row
    tri = jnp.where(causal, 1.0, 0.0).astype(BF16)
    a_cs = _dot3_l(tri, a)
    a_cs_t = a_cs.T
    ea = jnp.exp(a_cs)
    to_end = jnp.exp(a_cs[CHUNK - 1:CHUNK, :] - a_cs)
    stack = jnp.concatenate([dt, to_end, ea], axis=0)

    lane = lax.broadcasted_iota(jnp.int32, (CHUNK, LANES), 1)
    low_half = lane < HEADDIM
    hrow = lax.broadcasted_iota(jnp.int32, (LANES, GROUP_W), 0)
    hcol = lax.broadcasted_iota(jnp.int32, (LANES, GROUP_W), 1)

    for g in range(N_GROUPS):
        sl = slice(g * GROUP_W, (g + 1) * GROUP_W)
        sel = jnp.where(hrow == g * (GROUP_W // HEADDIM) + _div_pow2(hcol, HEADDIM), 1.0, 0.0).astype(BF16)
        ex = _dot3_r(stack, sel)
        dt_e, te_e, ea_e = ex[0:CHUNK], ex[CHUNK:2 * CHUNK], ex[2 * CHUNK:3 * CHUNK]

        x_g = xc_ref[:, sl]
        b_g = xc_ref[:, D_INNER + g * D_STATE:D_INNER + (g + 1) * D_STATE]
        c_g = xc_ref[:, D_INNER + BC_W + g * D_STATE:D_INNER + BC_W + (g + 1) * D_STATE]
        xdt = x_g * dt_e
        xdt_b = xdt.astype(BF16)
        cb16, bb16 = c_g.astype(BF16), b_g.astype(BF16)
        cb = _dot_nt(cb16, bb16)

        ht_g = ht_ref[:, sl]
        y_g = _dot(cb16, ht_g.astype(BF16)) * ea_e + dsk_ref[:, sl] * x_g
        states_t = _dot(b_g.T.astype(BF16), (xdt * te_e).astype(BF16))
        ht_ref[:, sl] = ea_e[CHUNK - 1:CHUNK, :] * ht_g + states_t

        y_pairs = []
        for jj in range(GROUP_W // LANES):
            xp = xdt_b[:, jj * LANES:(jj + 1) * LANES]
            halves = []
            for hh in range(2):
                h = g * (GROUP_W // HEADDIM) + 2 * jj + hh
                seg = a_cs[:, h:h + 1] - a_cs_t[h:h + 1, :]
                decay = jnp.exp(jnp.where(causal, seg, -jnp.inf))
                halves.append(_dot((cb * decay).astype(BF16), xp))
            y_pairs.append(jnp.where(low_half, halves[0], halves[1]))
        y_g = y_g + jnp.concatenate(y_pairs, axis=1)

        y_ref[:, sl] = _gated_norm(y_g, z_ref[:, sl], gn_ref[:, sl]).astype(BF16)

    @pl.when(ci == pl.num_programs(0) - 1)
    def _():
        for i in range(D_INNER // LANES):
            hout_ref[i * LANES:(i + 1) * LANES, :] = ht_ref[:, i * LANES:(i + 1) * LANES].T


def _ssd_prompt(proj, cw, cb, dtb, alog, dsk, gn):
    nchunk = SEQ // CHUNK
    full = lambda r, c: pl.BlockSpec((r, c), lambda i: (0, 0))
    return pl.pallas_call(
        _ssd_kernel,
        grid=(nchunk,),
        in_specs=[pl.BlockSpec((CHUNK, D_INNER), lambda i: (i, OFF_X // D_INNER)),
                  pl.BlockSpec((CHUNK, BC_W), lambda i: (i, OFF_B // BC_W)),
                  pl.BlockSpec((CHUNK, BC_W), lambda i: (i, OFF_C // BC_W)),
                  pl.BlockSpec((CHUNK, D_INNER), lambda i: (i, OFF_Z // D_INNER)),
                  pl.BlockSpec((CHUNK, LANES), lambda i: (i, OFF_DT // LANES)),
                  full(CONV_W, CONV_DIM), full(1, CONV_DIM), full(1, LANES), full(1, LANES),
                  full(1, D_INNER), full(1, D_INNER)],
        out_specs=[pl.BlockSpec((CHUNK, D_INNER), lambda i: (i, 0)),
                   pl.BlockSpec((D_INNER, D_STATE), lambda i: (0, 0)),
                   pl.BlockSpec((SUBLANES, CONV_DIM), lambda i: (0, 0))],
        out_shape=[jax.ShapeDtypeStruct((SEQ, D_INNER), BF16),
                   jax.ShapeDtypeStruct((D_INNER, D_STATE), F32),
                   jax.ShapeDtypeStruct((SUBLANES, CONV_DIM), F32)],
        scratch_shapes=[pltpu.VMEM((CHUNK + SUBLANES, CONV_DIM), F32),
                        pltpu.VMEM((CHUNK, CONV_DIM), F32),
                        pltpu.VMEM((D_STATE, D_INNER), F32)],
        compiler_params=_cparams("arbitrary"),
        name="ssd_prompt",
    )(proj, proj, proj, proj, proj, cw, cb, dtb, alog, dsk, gn)


def _attn_kernel(sink_ref, q_ref, kp_ref, kc_ref, vp_ref, vc_ref, o_ref):
    i = pl.program_id(0)
    band = 2 * CHUNK
    l_idx = lax.broadcasted_iota(jnp.int32, (CHUNK, band), 0)
    s_idx = lax.broadcasted_iota(jnp.int32, (CHUNK, band), 1)
    first_key = jnp.where(i > 0, 0, CHUNK)
    mask = (s_idx >= jnp.maximum(l_idx, first_key)) & (s_idx <= l_idx + WINDOW)
    low_q = lax.broadcasted_iota(jnp.int32, (CHUNK, LANES), 1) < HEADDIM
    low_kv = lax.broadcasted_iota(jnp.int32, (band, LANES), 1) < HEADDIM

    for c in range(KV_DIM // LANES):
        csl = slice(c * LANES, (c + 1) * LANES)
        k2 = jnp.concatenate([kp_ref[:, csl], kc_ref[:, csl]], axis=0)
        v2 = jnp.concatenate([vp_ref[:, csl], vc_ref[:, csl]], axis=0)
        k2r = pltpu.roll(k2, HEADDIM, axis=1)
        v2r = pltpu.roll(v2, HEADDIM, axis=1)
        for u in range(2):
            kv_head = 2 * c + u
            kd = (jnp.where(low_kv, k2, k2r) if u == 0 else jnp.where(low_kv, k2r, k2)).astype(BF16)
            vd = (jnp.where(low_kv, v2, v2r) if u == 0 else jnp.where(low_kv, v2r, v2)).astype(BF16)
            for jj in range(Q_PER_KV // 2):
                j = kv_head * (Q_PER_KV // 2) + jj
                q2 = q_ref[:, j * LANES:(j + 1) * LANES] * (HEADDIM ** -0.5)
                halves = []
                for hh in range(2):
                    qm = jnp.where(low_q if hh == 0 else jnp.logical_not(low_q), q2, 0.0)
                    s = jnp.where(mask, _dot_nt(qm.astype(BF16), kd), -jnp.inf)
                    sink = sink_ref[2 * j + hh]
                    m = jnp.maximum(jnp.max(s, axis=-1, keepdims=True), sink)
                    p = jnp.exp(s - m)
                    den = jnp.sum(p, axis=-1, keepdims=True) + jnp.exp(sink - m)
                    halves.append(_dot((p / den).astype(BF16), vd))
                o_ref[:, j * LANES:(j + 1) * LANES] = jnp.where(low_q, halves[0], halves[1]).astype(BF16)


def _attn_prompt(proj, sinks):
    nb = SEQ // CHUNK
    prev = lambda i: jnp.maximum(i - 1, 0)
    return pl.pallas_call(
        _attn_kernel,
        grid=(nb,),
        in_specs=[pl.BlockSpec(memory_space=pltpu.SMEM),
                  pl.BlockSpec((CHUNK, D_MODEL), lambda i: (i, OFF_Q // D_MODEL)),
                  pl.BlockSpec((CHUNK, KV_DIM), lambda i: (prev(i), OFF_K // KV_DIM)),
                  pl.BlockSpec((CHUNK, KV_DIM), lambda i: (i, OFF_K // KV_DIM)),
                  pl.BlockSpec((CHUNK, KV_DIM), lambda i: (prev(i), OFF_V // KV_DIM)),
                  pl.BlockSpec((CHUNK, KV_DIM), lambda i: (i, OFF_V // KV_DIM))],
        out_specs=pl.BlockSpec((CHUNK, D_MODEL), lambda i: (i, 0)),
        out_shape=jax.ShapeDtypeStruct((SEQ, D_MODEL), BF16),
        compiler_params=_cparams("parallel"),
        name="attn_prompt",
    )(sinks, proj, proj, proj, proj, proj)


def _alias_prev(prev, n_in, first_out):
    if prev is None:
        return [], {}, ()
    specs = [pl.BlockSpec(memory_space=pl.ANY)] * len(prev)
    return specs, {n_in + t: first_out + t for t in range(len(prev))}, tuple(prev)


def _conv_step_kernel(x_ref, dt_ref, cs_ref, cw_ref, cb_ref, dtb_ref, alog_ref, *rest):
    xc_ref, csn_ref, dto_ref, dao_ref = rest[-4:]
    taps = [cs_ref[k] for k in range(CONV_W - 1)] + [x_ref[...]]
    acc = cb_ref[...]
    for k in range(CONV_W):
        acc = acc + taps[k] * cw_ref[k:k + 1, :]
    xc_ref[...] = _silu(acc)
    for k in range(CONV_W - 1):
        csn_ref[k] = taps[k + 1]
    dt = _softplus(dt_ref[...] + dtb_ref[...])
    dto_ref[...] = dt
    dao_ref[...] = jnp.exp(dt * (-jnp.exp(alog_ref[...])))


def _conv_step(proj, conv_state, layer, cw, cb, dtb, alog, prev):
    nb = DEC_BATCH
    small = pl.BlockSpec((1, LANES), lambda s: (0, 0))
    head = pl.BlockSpec((nb, LANES), lambda s: (0, 0))
    state = pl.BlockSpec((None, CONV_W - 1, nb, GROUP_W), lambda s: (layer, 0, 0, s))
    in_specs = [pl.BlockSpec((nb, GROUP_W), lambda s: (0, OFF_X // GROUP_W + s)),
                pl.BlockSpec((nb, LANES), lambda s: (0, OFF_DT // LANES)),
                state,
                pl.BlockSpec((CONV_W, GROUP_W), lambda s: (0, s)),
                pl.BlockSpec((1, GROUP_W), lambda s: (0, s)),
                small, small]
    alias_specs, aliases, alias_args = _alias_prev(prev, len(in_specs), 1)
    return pl.pallas_call(
        _conv_step_kernel,
        grid=(CONV_DIM // GROUP_W,),
        in_specs=in_specs + alias_specs,
        out_specs=[pl.BlockSpec((nb, GROUP_W), lambda s: (0, s)), state, head, head],
        out_shape=[jax.ShapeDtypeStruct((nb, CONV_DIM), F32),
                   jax.ShapeDtypeStruct(conv_state.shape, F32),
                   jax.ShapeDtypeStruct((nb, LANES), F32),
                   jax.ShapeDtypeStruct((nb, LANES), F32)],
        input_output_aliases=aliases,
        compiler_params=_cparams("arbitrary"),
        name="conv_step",
    )(proj, proj, conv_state, cw, cb, dtb, alog, *alias_args)


STEP_B = 8
STEP_GROUPS = 4
STEP_W = STEP_GROUPS * GROUP_W


def _ssm_step_kernel(xs_ref, b_ref, c_ref, z_ref, dt_ref, da_ref, dsk_ref, gn_ref, h_ref, *rest):
    y_ref, hn_ref = rest[-2:]
    part = pl.program_id(1)
    xs = xs_ref[...]
    hrow = lax.broadcasted_iota(jnp.int32, (LANES, STEP_W), 0)
    hcol = lax.broadcasted_iota(jnp.int32, (LANES, STEP_W), 1)
    sel = jnp.where(hrow == part * (STEP_W // HEADDIM) + _div_pow2(hcol, HEADDIM), 1.0, 0.0).astype(BF16)
    dt_e = _dot3_r(dt_ref[...], sel)
    da_e = _dot3_r(da_ref[...], sel)
    xdt = xs * dt_e
    lane_group = _div_pow2(lax.broadcasted_iota(jnp.int32, (STEP_B, STEP_W), 1), GROUP_W)

    pieces = [jnp.where(lane_group == g, xdt, 0.0) for g in range(STEP_GROUPS)]
    pieces += [p.astype(F32) for p in _split3(da_e)]
    n_used = len(pieces) * STEP_B
    pieces.append(jnp.zeros((LANES - n_used, STEP_W), F32))
    ut = jnp.concatenate(pieces, axis=0)
    u_big = jnp.concatenate([ut[:, i * LANES:(i + 1) * LANES].T for i in range(STEP_W // LANES)],
                            axis=0).astype(BF16)

    b_rows = jnp.concatenate([b_ref[:, g * D_STATE:(g + 1) * D_STATE] for g in range(STEP_GROUPS)]
                             + [jnp.zeros((LANES - STEP_GROUPS * STEP_B, D_STATE), F32)], axis=0)
    rrow = lax.broadcasted_iota(jnp.int32, (LANES, D_STATE), 0)
    ones_rows = jnp.where((rrow >= STEP_GROUPS * STEP_B) & (rrow < n_used), 1.0, 0.0)
    big_r = jnp.concatenate([b_rows, ones_rows], axis=1)
    r_seq = _mod_pow2(lax.broadcasted_iota(jnp.int32, (LANES, 2 * D_STATE), 0), STEP_B)

    cc = jnp.concatenate([c_ref[:, g * D_STATE:(g + 1) * D_STATE] for g in range(STEP_GROUPS)],
                         axis=0).astype(BF16)
    crow = lax.broadcasted_iota(jnp.int32, (STEP_GROUPS * STEP_B, STEP_W), 0)
    ccol_group = _div_pow2(lax.broadcasted_iota(jnp.int32, (STEP_GROUPS * STEP_B, STEP_W), 1), GROUP_W)
    sub = lax.broadcasted_iota(jnp.int32, (STEP_B, STEP_W), 0)

    y_off = jnp.zeros((STEP_B, STEP_W), F32)
    for b in range(STEP_B):
        h0 = h_ref[b]
        res = _dot(u_big, jnp.where(r_seq == b, big_r, 0.0).astype(BF16))
        hn_ref[b] = res[:, D_STATE:] * h0 + res[:, :D_STATE]
        y_all = _dot_nt(cc, h0.astype(BF16))
        pick = (crow == ccol_group * STEP_B + b)
        y_b = jnp.sum(jnp.where(pick, y_all, 0.0), axis=0, keepdims=True)
        y_off = jnp.where(sub == b, y_b, y_off)

    cb_e = jnp.zeros((STEP_B, STEP_W), F32)
    for g in range(STEP_GROUPS):
        gsl = slice(g * D_STATE, (g + 1) * D_STATE)
        cb_g = jnp.sum(c_ref[:, gsl] * b_ref[:, gsl], axis=-1, keepdims=True)
        cb_e = jnp.where(lane_group == g, cb_g, cb_e)

    y = da_e * y_off + cb_e * xdt + dsk_ref[...] * xs
    for g in range(STEP_GROUPS):
        sl = slice(g * GROUP_W, (g + 1) * GROUP_W)
        y_ref[:, sl] = _gated_norm(y[:, sl], z_ref[:, sl], gn_ref[:, sl])


def _ssm_step(xc, proj, dt, da, dsk, gn, state, layer, prev):
    nparts = D_INNER // STEP_W
    state_spec = pl.BlockSpec((None, STEP_B, STEP_W, D_STATE), lambda i, p: (layer, i, p, 0))
    in_specs = [pl.BlockSpec((STEP_B, STEP_W), lambda i, p: (i, p)),
                pl.BlockSpec((STEP_B, STEP_GROUPS * D_STATE),
                             lambda i, p: (i, D_INNER // (STEP_GROUPS * D_STATE) + p)),
                pl.BlockSpec((STEP_B, STEP_GROUPS * D_STATE),
                             lambda i, p: (i, (D_INNER + BC_W) // (STEP_GROUPS * D_STATE) + p)),
                pl.BlockSpec((STEP_B, STEP_W), lambda i, p: (i, OFF_Z // STEP_W + p)),
                pl.BlockSpec((STEP_B, LANES), lambda i, p: (i, 0)),
                pl.BlockSpec((STEP_B, LANES), lambda i, p: (i, 0)),
                pl.BlockSpec((1, STEP_W), lambda i, p: (0, p)),
                pl.BlockSpec((1, STEP_W), lambda i, p: (0, p)),
                state_spec]
    alias_specs, aliases, alias_args = _alias_prev(prev, len(in_specs), 1)
    return pl.pallas_call(
        _ssm_step_kernel,
        grid=(DEC_BATCH // STEP_B, nparts),
        in_specs=in_specs + alias_specs,
        out_specs=[pl.BlockSpec((STEP_B, STEP_W), lambda i, p: (i, p)), state_spec],
        out_shape=[jax.ShapeDtypeStruct((DEC_BATCH, D_INNER), F32),
                   jax.ShapeDtypeStruct(state.shape, F32)],
        input_output_aliases=aliases,
        compiler_params=_cparams("parallel", "parallel"),
        name="ssm_step",
    )(xc, xc, xc, proj, dt, da, dsk, gn, state, *alias_args)


ATTN_B = 8


def _attn_step_kernel(sink_ref, q_ref, kn_ref, vn_ref, kc_ref, vc_ref, *rest):
    o_ref, ko_ref, vo_ref = rest[-3:]
    lane_kv = _div_pow2(lax.broadcasted_iota(jnp.int32, (N_KV, KV_DIM), 1), HEADDIM)
    own = lane_kv == lax.broadcasted_iota(jnp.int32, (N_KV, KV_DIM), 0)
    newest = lax.broadcasted_iota(jnp.int32, (KV_DIM, WINDOW), 1) == WINDOW - 1
    sink = sink_ref[...]
    pad = jnp.zeros((LANES - ATTN_B, KV_DIM), F32)
    kn_t = jnp.concatenate([kn_ref[...], pad], axis=0).T
    vn_t = jnp.concatenate([vn_ref[...], pad], axis=0).T
    for b in range(ATTN_B):
        qrow = q_ref[b:b + 1, :] * (HEADDIM ** -0.5)
        qm = jnp.concatenate(
            [jnp.where(own, jnp.broadcast_to(qrow[:, r * KV_DIM:(r + 1) * KV_DIM], (N_KV, KV_DIM)), 0.0)
             for r in range(Q_PER_KV)], axis=0)
        k_new = kn_ref[b:b + 1, :]
        v_new = vn_ref[b:b + 1, :]
        kt = kc_ref[b]
        vt = vc_ref[b]
        s = _dot(qm.astype(BF16), kt.astype(BF16))
        s_new = jnp.sum(qm * k_new, axis=-1, keepdims=True)
        m = jnp.maximum(jnp.maximum(jnp.max(s, axis=-1, keepdims=True), s_new), sink)
        p = jnp.exp(s - m)
        p_new = jnp.exp(s_new - m)
        den = jnp.sum(p, axis=-1, keepdims=True) + p_new + jnp.exp(sink - m)
        o_full = _dot_nt((p / den).astype(BF16), vt.astype(BF16)) + (p_new / den) * v_new
        parts = []
        for r in range(Q_PER_KV):
            o_r = jnp.where(own, o_full[r * N_KV:(r + 1) * N_KV, :], 0.0)
            parts.append(jnp.sum(o_r, axis=0, keepdims=True))
        o_ref[b:b + 1, :] = jnp.concatenate(parts, axis=1)
        ko_ref[b] = jnp.where(newest, jnp.broadcast_to(kn_t[:, b:b + 1], (KV_DIM, WINDOW)),
                              pltpu.roll(kt, WINDOW - 1, axis=1))
        vo_ref[b] = jnp.where(newest, jnp.broadcast_to(vn_t[:, b:b + 1], (KV_DIM, WINDOW)),
                              pltpu.roll(vt, WINDOW - 1, axis=1))


def _attn_step(q, proj, sinks, cache_k, cache_v, layer, prev):
    cache_spec = pl.BlockSpec((None, ATTN_B, KV_DIM, WINDOW), lambda i: (layer, i, 0, 0))
    in_specs = [pl.BlockSpec((N_HEADS, 1), lambda i: (0, 0)),
                pl.BlockSpec((ATTN_B, D_MODEL), lambda i: (i, 0)),
                pl.BlockSpec((ATTN_B, KV_DIM), lambda i: (i, OFF_K // KV_DIM)),
                pl.BlockSpec((ATTN_B, KV_DIM), lambda i: (i, OFF_V // KV_DIM)),
                cache_spec, cache_spec]
    alias_specs, aliases, alias_args = _alias_prev(prev, len(in_specs), 1)
    return pl.pallas_call(
        _attn_step_kernel,
        grid=(DEC_BATCH // ATTN_B,),
        in_specs=in_specs + alias_specs,
        out_specs=[pl.BlockSpec((ATTN_B, D_MODEL), lambda i: (i, 0)), cache_spec, cache_spec],
        out_shape=[jax.ShapeDtypeStruct((DEC_BATCH, D_MODEL), F32),
                   jax.ShapeDtypeStruct(cache_k.shape, F32),
                   jax.ShapeDtypeStruct(cache_v.shape, F32)],
        input_output_aliases=aliases,
        compiler_params=_cparams("parallel"),
        name="attn_step",
    )(sinks, q, proj, proj, cache_k, cache_v, *alias_args)


def _cast_kernel(x_ref, o_ref):
    o_ref[...] = x_ref[...].astype(BF16)


def _cast_bf16(w, tk, name):
    d, k, n = w.shape
    spec = pl.BlockSpec((None, tk, n), lambda l, i: (l, i, 0))
    return pl.pallas_call(
        _cast_kernel,
        grid=(d, k // tk),
        in_specs=[spec],
        out_specs=spec,
        out_shape=jax.ShapeDtypeStruct(w.shape, BF16),
        compiler_params=_cparams("parallel", "parallel"),
        name=name,
    )(w)


SRC_DT = 10240
W_TILE = 512
N_W_TILES = PROJ_W // W_TILE
FIRST_SHIFTED = SRC_DT // W_TILE


def _w_in_layout_kernel(x_ref, o_ref):
    t = pl.program_id(1)
    valid = jnp.where(t == N_W_TILES - 1, N_SSM_HEADS, W_TILE)
    row = lax.broadcasted_iota(jnp.int32, (W_TILE, D_MODEL), 0)
    o_ref[...] = jnp.where(row < valid, x_ref[0], 0.0).T.astype(BF16)


def _w_in_layout(w_t):
    d = w_t.shape[0]

    def src_row(l, t):
        per_tile = W_TILE // N_SSM_HEADS
        unit = jnp.where(t < FIRST_SHIFTED, t * per_tile,
                         jnp.where(t < N_W_TILES - 1, t * per_tile + 1, SRC_DT // N_SSM_HEADS))
        return (l, unit * N_SSM_HEADS, 0)

    return pl.pallas_call(
        _w_in_layout_kernel,
        grid=(d, N_W_TILES),
        in_specs=[pl.BlockSpec((pl.Element(1), pl.Element(W_TILE), pl.Element(D_MODEL)), src_row)],
        out_specs=pl.BlockSpec((None, D_MODEL, W_TILE), lambda l, t: (l, 0, t)),
        out_shape=jax.ShapeDtypeStruct((d, D_MODEL, PROJ_W), BF16),
        compiler_params=_cparams("parallel", "parallel"),
        name="w_in_layout",
    )(w_t)


def _small_params(l, norm_mix, conv_w, conv_b, dt_bias, a_log, d_skip, ssm_norm, attn_sinks, norm_ffn):
    pad = lambda v: jnp.pad(v, (0, LANES - N_SSM_HEADS)).reshape(1, LANES)
    return dict(
        norm_mix=norm_mix[l].reshape(1, D_MODEL),
        conv_w=conv_w[l], conv_b=conv_b[l].reshape(1, CONV_DIM),
        dt_bias=pad(dt_bias[l]), a_log=pad(a_log[l]),
        d_skip=jnp.repeat(d_skip[l], HEADDIM).reshape(1, D_INNER),
        ssm_norm=ssm_norm[l].reshape(1, D_INNER),
        sinks=attn_sinks[l],
        sinks_rm=attn_sinks[l].reshape(N_KV, Q_PER_KV).T.reshape(N_HEADS, 1),
        norm_ffn=norm_ffn[l].reshape(1, D_MODEL))


def _dense_tail(x, y, o, proj, p, w, l, tm):
    merged = _merge(y, o, proj, w["w_ssm"], w["w_attn"], l, tm, 512)
    x1 = _matmul_res(merged, w["w_out"], l, x, tm, 1024, "out_proj")
    hmid = _ffn_up(x1, p["norm_ffn"], w["w_gate"], w["w_up"], l, tm, 512)
    return _matmul_res(hmid, w["w_down"], l, x1, tm, 512, "ffn_down")


PROMPT_TM = 1024
PROJ_TN = 1280


def kernel(x_prompt, x_sample, state_ssm, state_conv, cache_k, cache_v, norm_mix, w_in, conv_w,
           conv_b, dt_bias, a_log, d_skip, ssm_norm, w_ssm_proj, w_attn_proj, attn_sinks, w_out,
           norm_ffn, w_gate, w_up, w_down, norm_final):
    xp = x_prompt.reshape(SEQ, D_MODEL)
    xs = x_sample.reshape(DEC_BATCH, D_MODEL)
    state4 = state_ssm.reshape(DEPTH, DEC_BATCH, D_INNER, D_STATE)
    to_kt = lambda c: c.transpose(0, 1, 3, 4, 2).reshape(DEPTH, DEC_BATCH, KV_DIM, WINDOW)
    ck, cv = to_kt(cache_k), to_kt(cache_v)
    conv_t = state_conv.transpose(0, 2, 1, 3)
    outs = {k: [] for k in ("p_ssm", "p_conv", "p_k", "p_v")}

    w = dict(w_in=_w_in_layout(w_in.transpose(0, 2, 1)),
             w_attn=_cast_bf16(w_attn_proj, 1024, "cast_w_attn"),
             w_ssm=_cast_bf16(w_ssm_proj, 1024, "cast_w_ssm"),
             w_out=_cast_bf16(w_out, 1024, "cast_w_out"),
             w_gate=_cast_bf16(w_gate, 256, "cast_w_gate"),
             w_up=_cast_bf16(w_up, 256, "cast_w_up"),
             w_down=_cast_bf16(w_down, 512, "cast_w_down"))

    h_all = conv_all = kv_all = None
    for l in range(DEPTH):
        p = _small_params(l, norm_mix, conv_w, conv_b, dt_bias, a_log, d_skip, ssm_norm,
                          attn_sinks, norm_ffn)

        proj = _in_proj(xp, p["norm_mix"], w["w_in"], l, PROMPT_TM, PROJ_TN)
        y, h_p, tail = _ssd_prompt(proj, p["conv_w"], p["conv_b"], p["dt_bias"], p["a_log"],
                                   p["d_skip"], p["ssm_norm"])
        o = _attn_prompt(proj, p["sinks"])
        outs["p_ssm"].append(h_p.reshape(1, N_SSM_HEADS, HEADDIM, D_STATE))
        outs["p_conv"].append(tail[SUBLANES - (CONV_W - 1):].reshape(1, CONV_W - 1, CONV_DIM))
        outs["p_k"].append(proj[SEQ - WINDOW:, OFF_K:OFF_K + KV_DIM].reshape(1, WINDOW, N_KV, HEADDIM))
        outs["p_v"].append(proj[SEQ - WINDOW:, OFF_V:OFF_V + KV_DIM].reshape(1, WINDOW, N_KV, HEADDIM))
        xp = _dense_tail(xp, y, o, proj, p, w, l, PROMPT_TM)

        proj_s = _in_proj(xs, p["norm_mix"], w["w_in"], l, DEC_BATCH, PROJ_TN)
        xc, conv_all, dt, da = _conv_step(proj_s, conv_t, l, p["conv_w"], p["conv_b"],
                                          p["dt_bias"], p["a_log"],
                                          None if conv_all is None else (conv_all,))
        y_s, h_all = _ssm_step(xc, proj_s, dt, da, p["d_skip"], p["ssm_norm"], state4, l,
                               None if h_all is None else (h_all,))
        swap_rm = lambda t, a, b: t.reshape(DEC_BATCH, a, b, HEADDIM).transpose(0, 2, 1, 3).reshape(
            DEC_BATCH, D_MODEL)
        q_s = swap_rm(proj_s[:, OFF_Q:OFF_Q + D_MODEL], N_KV, Q_PER_KV)
        o_s, k_all, v_all = _attn_step(q_s, proj_s, p["sinks_rm"], ck, cv, l, kv_all)
        o_s = swap_rm(o_s, Q_PER_KV, N_KV)
        kv_all = (k_all, v_all)
        xs = _dense_tail(xs, y_s, o_s, proj_s, p, w, l, DEC_BATCH)

    y_prompt = _rmsnorm(xp, norm_final.reshape(1, D_MODEL), PROMPT_TM).reshape(1, SEQ, D_MODEL)
    y_sample = _rmsnorm(xs, norm_final.reshape(1, D_MODEL), DEC_BATCH).reshape(DEC_BATCH, 1, D_MODEL)
    st = lambda k: jnp.stack(outs[k])
    from_kt = lambda c: c.reshape(DEPTH, DEC_BATCH, N_KV, HEADDIM, WINDOW).transpose(0, 1, 4, 2, 3)
    return (y_prompt, y_sample, st("p_ssm"), st("p_conv"), st("p_k"), st("p_v"),
            h_all.reshape(DEPTH, DEC_BATCH, N_SSM_HEADS, HEADDIM, D_STATE),
            conv_all.transpose(0, 2, 1, 3), from_kt(kv_all[0]), from_kt(kv_all[1]))
```

```python
import functools

import jax
import jax.numpy as jnp
from jax import lax
from jax.experimental import pallas as pl
from jax.experimental.pallas import tpu as pltpu

F32 = jnp.float32
BF16 = jnp.bfloat16

D_MODEL = 2048
SEQ = 8192
DEPTH = 2
DEC_BATCH = 128
D_INNER = 4096
HEADDIM = 64
N_SSM_HEADS = 64
D_STATE = 128
N_GROUPS = 8
GROUP_W = D_INNER // N_GROUPS
CONV_W = 4
BC_W = N_GROUPS * D_STATE
CONV_DIM = D_INNER + 2 * BC_W
CHUNK = 128
N_HEADS = 32
N_KV = 8
Q_PER_KV = 4
KV_DIM = N_KV * HEADDIM
WINDOW = 128
D_FF = 5632
EPS = 1e-6
IN_DIM = 17472

OFF_X = 0
OFF_B = OFF_X + D_INNER
OFF_C = OFF_B + BC_W
OFF_Z = CONV_DIM
OFF_Q = 10240
OFF_K = 12288
OFF_V = 12800
OFF_GS = 13312
OFF_GA = 15360
OFF_DT = 17408
PROJ_W = 17920

LANES = 128
SUBLANES = 8
VMEM_LIMIT = 56 * 1024 * 1024


def _cparams(*sem):
    return pltpu.CompilerParams(dimension_semantics=sem, vmem_limit_bytes=VMEM_LIMIT)


def _dot(a, b):
    return jnp.dot(a, b, preferred_element_type=F32)


def _dot_nt(a, b):
    return lax.dot_general(a, b, (((1,), (1,)), ((), ())), preferred_element_type=F32)


def _div_pow2(v, d):
    return jnp.right_shift(v, d.bit_length() - 1)


def _mod_pow2(v, d):
    return jnp.bitwise_and(v, d - 1)


def _split3(x):
    hi = x.astype(BF16)
    r1 = x - hi.astype(F32)
    mid = r1.astype(BF16)
    lo = (r1 - mid.astype(F32)).astype(BF16)
    return hi, mid, lo


def _dot3_r(x, sel):
    return _dot(jnp.concatenate(_split3(x), axis=1), jnp.concatenate([sel, sel, sel], axis=0))


def _dot3_l(sel, x):
    return _dot(jnp.concatenate([sel, sel, sel], axis=1), jnp.concatenate(_split3(x), axis=0))


def _silu(x):
    h = 0.5 * x
    return h + h * jnp.tanh(h)


def _softplus(x):
    return jnp.maximum(x, 0.0) + jnp.log1p(jnp.exp(-jnp.abs(x)))


def _rms_scale(x, g):
    r = lax.rsqrt(jnp.mean(x * x, axis=-1, keepdims=True) + EPS)
    return x * r * g


def _norm_matmul_kernel(x_ref, g_ref, w_ref, o_ref, xn_ref):
    @pl.when(pl.program_id(1) == 0)
    def _():
        xn_ref[...] = _rms_scale(x_ref[...], g_ref[...]).astype(BF16)

    o_ref[...] = _dot(xn_ref[...], w_ref[...])


def _in_proj(x, g, w, layer, tm, tn):
    m, k = x.shape
    n = w.shape[2]
    return pl.pallas_call(
        _norm_matmul_kernel,
        grid=(m // tm, n // tn),
        in_specs=[pl.BlockSpec((tm, k), lambda i, j: (i, 0)),
                  pl.BlockSpec((1, k), lambda i, j: (0, 0)),
                  pl.BlockSpec((None, k, tn), lambda i, j: (layer, 0, j))],
        out_specs=pl.BlockSpec((tm, tn), lambda i, j: (i, j)),
        out_shape=jax.ShapeDtypeStruct((m, n), F32),
        scratch_shapes=[pltpu.VMEM((tm, k), BF16)],
        compiler_params=_cparams("parallel", "arbitrary"),
        name="in_proj",
    )(x, g, w)


MXU_N = 256
REST_W = OFF_DT - OFF_Z
CONV_ROWS = 256


def _proj_conv_kernel(x_ref, g_ref, w_ref, wdt_ref, cw_ref, cb_ref, xc_ref, dt_ref, tail_ref,
                      xn_ref, xpad_ref, carry_ref):
    i = pl.program_id(0)
    j = pl.program_id(1)
    tm, tn = xc_ref.shape

    @pl.when(j == 0)
    def _():
        xn = _rms_scale(x_ref[...], g_ref[...]).astype(BF16)
        xn_ref[...] = xn
        dt_ref[...] = _dot(xn, wdt_ref[...])

    @pl.when(i == 0)
    def _():
        carry_ref[j] = jnp.zeros((SUBLANES, tn), F32)

    xpad_ref[0:SUBLANES, :] = carry_ref[j]
    for cs in range(tn // MXU_N):
        sl = slice(cs * MXU_N, (cs + 1) * MXU_N)
        xpad_ref[SUBLANES:SUBLANES + tm, sl] = _dot(xn_ref[...], w_ref[:, sl])
        for r0 in range(0, tm, CONV_ROWS):
            xp = xpad_ref[r0:r0 + SUBLANES + CONV_ROWS, sl]
            acc = cb_ref[:, sl] + xp[SUBLANES:] * cw_ref[CONV_W - 1:CONV_W, sl]
            for back in range(1, CONV_W):
                k = CONV_W - 1 - back
                acc = acc + pltpu.roll(xp, back, axis=0)[SUBLANES:] * cw_ref[k:k + 1, sl]
            xc_ref[r0:r0 + CONV_ROWS, sl] = _silu(acc)
    tail = xpad_ref[tm:tm + SUBLANES, :]
    carry_ref[j] = tail
    tail_ref[...] = tail


def _in_proj_conv(x, g, w, layer, cw, cb, tm, tn):
    m, k = x.shape
    ntile = CONV_DIM // tn
    return pl.pallas_call(
        _proj_conv_kernel,
        grid=(m // tm, ntile),
        in_specs=[pl.BlockSpec((tm, k), lambda i, j: (i, 0)),
                  pl.BlockSpec((1, k), lambda i, j: (0, 0)),
                  pl.BlockSpec((None, k, tn), lambda i, j: (layer, 0, OFF_X // tn + j)),
                  pl.BlockSpec((None, k, LANES), lambda i, j: (layer, 0, OFF_DT // LANES)),
                  pl.BlockSpec((CONV_W, tn), lambda i, j: (0, j)),
                  pl.BlockSpec((1, tn), lambda i, j: (0, j))],
        out_specs=[pl.BlockSpec((tm, tn), lambda i, j: (i, j)),
                   pl.BlockSpec((tm, LANES), lambda i, j: (i, 0)),
                   pl.BlockSpec((None, SUBLANES, tn), lambda i, j: (i, 0, j))],
        out_shape=[jax.ShapeDtypeStruct((m, CONV_DIM), F32),
                   jax.ShapeDtypeStruct((m, LANES), F32),
                   jax.ShapeDtypeStruct((m // tm, SUBLANES, CONV_DIM), F32)],
        scratch_shapes=[pltpu.VMEM((tm, k), BF16),
                        pltpu.VMEM((tm + SUBLANES, tn), F32),
                        pltpu.VMEM((ntile, SUBLANES, tn), F32)],
        compiler_params=_cparams("arbitrary", "arbitrary"),
        name="in_proj_conv",
    )(x, g, w, w, cw, cb)


def _proj_rest_kernel(x_ref, g_ref, w_ref, o_ref, xn_ref, *, n_silu_tiles):
    j = pl.program_id(1)
    tn = o_ref.shape[1]

    @pl.when(j == 0)
    def _():
        xn_ref[...] = _rms_scale(x_ref[...], g_ref[...]).astype(BF16)

    @pl.when(j < n_silu_tiles)
    def _():
        for cs in range(tn // MXU_N):
            sl = slice(cs * MXU_N, (cs + 1) * MXU_N)
            o_ref[:, sl] = _silu(_dot(xn_ref[...], w_ref[:, sl]))

    @pl.when(j >= n_silu_tiles)
    def _():
        o_ref[...] = _dot(xn_ref[...], w_ref[...])


def _in_proj_rest(x, g, w, layer, tm, tn):
    m, k = x.shape
    return pl.pallas_call(
        functools.partial(_proj_rest_kernel, n_silu_tiles=D_INNER // tn),
        grid=(m // tm, REST_W // tn),
        in_specs=[pl.BlockSpec((tm, k), lambda i, j: (i, 0)),
                  pl.BlockSpec((1, k), lambda i, j: (0, 0)),
                  pl.BlockSpec((None, k, tn), lambda i, j: (layer, 0, OFF_Z // tn + j))],
        out_specs=pl.BlockSpec((tm, tn), lambda i, j: (i, j)),
        out_shape=jax.ShapeDtypeStruct((m, REST_W), F32),
        scratch_shapes=[pltpu.VMEM((tm, k), BF16)],
        compiler_params=_cparams("parallel", "arbitrary"),
        name="in_proj_rest",
    )(x, g, w)


def _ffn_up_kernel(x_ref, g_ref, wg_ref, wu_ref, o_ref, xn_ref):
    @pl.when(pl.program_id(1) == 0)
    def _():
        xn_ref[...] = _rms_scale(x_ref[...], g_ref[...]).astype(BF16)

    xn = xn_ref[...]
    o_ref[...] = (_silu(_dot(xn, wg_ref[...])) * _dot(xn, wu_ref[...])).astype(BF16)


def _ffn_up(x, g, wg, wu, layer, tm, tn):
    m, k = x.shape
    n = wg.shape[2]
    return pl.pallas_call(
        _ffn_up_kernel,
        grid=(m // tm, n // tn),
        in_specs=[pl.BlockSpec((tm, k), lambda i, j: (i, 0)),
                  pl.BlockSpec((1, k), lambda i, j: (0, 0)),
                  pl.BlockSpec((None, k, tn), lambda i, j: (layer, 0, j)),
                  pl.BlockSpec((None, k, tn), lambda i, j: (layer, 0, j))],
        out_specs=pl.BlockSpec((tm, tn), lambda i, j: (i, j)),
        out_shape=jax.ShapeDtypeStruct((m, n), BF16),
        scratch_shapes=[pltpu.VMEM((tm, k), BF16)],
        compiler_params=_cparams("parallel", "arbitrary"),
        name="ffn_up",
    )(x, g, wg, wu)


def _matmul_res_kernel(a_ref, w_ref, r_ref, o_ref):
    o_ref[...] = r_ref[...] + _dot(a_ref[...], w_ref[...])


def _matmul_res(a, w, layer, res, tm, tn, name):
    m, k = a.shape
    n = w.shape[2]
    return pl.pallas_call(
        _matmul_res_kernel,
        grid=(m // tm, n // tn),
        in_specs=[pl.BlockSpec((tm, k), lambda i, j: (i, 0)),
                  pl.BlockSpec((None, k, tn), lambda i, j: (layer, 0, j)),
                  pl.BlockSpec((tm, tn), lambda i, j: (i, j))],
        out_specs=pl.BlockSpec((tm, tn), lambda i, j: (i, j)),
        out_shape=jax.ShapeDtypeStruct((m, n), F32),
        compiler_params=_cparams("parallel", "parallel"),
        name=name,
    )(a, w, res)


def _merge_kernel(y_ref, o_ref, gs_ref, ga_ref, ws_ref, wa_ref, out_ref):
    ys = _dot(y_ref[...].astype(BF16), ws_ref[...])
    oa = _dot(o_ref[...].astype(BF16), wa_ref[...])
    out_ref[...] = (jax.nn.sigmoid(gs_ref[...]) * ys
                    + jax.nn.sigmoid(ga_ref[...]) * oa).astype(BF16)


def _merge(y, o, proj, col0, ws, wa, layer, tm, tn):
    m = y.shape[0]
    gs0, ga0 = (OFF_GS - col0) // tn, (OFF_GA - col0) // tn
    return pl.pallas_call(
        _merge_kernel,
        grid=(m // tm, D_MODEL // tn),
        in_specs=[pl.BlockSpec((tm, D_INNER), lambda i, j: (i, 0)),
                  pl.BlockSpec((tm, D_MODEL), lambda i, j: (i, 0)),
                  pl.BlockSpec((tm, tn), lambda i, j: (i, gs0 + j)),
                  pl.BlockSpec((tm, tn), lambda i, j: (i, ga0 + j)),
                  pl.BlockSpec((None, D_INNER, tn), lambda i, j: (layer, 0, j)),
                  pl.BlockSpec((None, D_MODEL, tn), lambda i, j: (layer, 0, j))],
        out_specs=pl.BlockSpec((tm, tn), lambda i, j: (i, j)),
        out_shape=jax.ShapeDtypeStruct((m, D_MODEL), BF16),
        compiler_params=_cparams("parallel", "parallel"),
        name="merge",
    )(y, o, proj, proj, ws, wa)


def _rmsnorm_kernel(x_ref, g_ref, o_ref):
    o_ref[...] = _rms_scale(x_ref[...], g_ref[...])


def _rmsnorm(x, g, tm):
    m, k = x.shape
    return pl.pallas_call(
        _rmsnorm_kernel,
        grid=(m // tm,),
        in_specs=[pl.BlockSpec((tm, k), lambda i: (i, 0)),
                  pl.BlockSpec((1, k), lambda i: (0, 0))],
        out_specs=pl.BlockSpec((tm, k), lambda i: (i, 0)),
        out_shape=jax.ShapeDtypeStruct((m, k), F32),
        compiler_params=_cparams("parallel"),
        name="final_norm",
    )(x, g)


def _gated_norm(y, gate, gain):
    h = y * gate
    r = lax.rsqrt(jnp.mean(h * h, axis=-1, keepdims=True) + EPS)
    return h * r * gain


LOG2E = 1.4426950408889634


def _ssd_kernel(xc_ref, zs_ref, dt_ref, dtb_ref, alog_ref, dsk_ref, gn_ref, y_ref, hout_ref,
                ht_ref):
    ci = pl.program_id(0)

    @pl.when(ci == 0)
    def _():
        ht_ref[...] = jnp.zeros_like(ht_ref)

    dt = _softplus(dt_ref[...] + dtb_ref[...])
    a = dt * (-jnp.exp(alog_ref[...])) * LOG2E
    row = lax.broadcasted_iota(jnp.int32, (CHUNK, CHUNK), 0)
    col = lax.broadcasted_iota(jnp.int32, (CHUNK, CHUNK), 1)
    causal = col <= row
    tri = jnp.where(causal, 1.0, 0.0).astype(BF16)
    a_cs = _dot3_l(tri, a)
    a_cs_t = a_cs.T
    ea = jnp.exp2(a_cs)
    to_end = jnp.exp2(a_cs[CHUNK - 1:CHUNK, :] - a_cs)
    stack = jnp.concatenate([dt, to_end, ea], axis=0)

    lane = lax.broadcasted_iota(jnp.int32, (CHUNK, LANES), 1)
    low_half = lane < HEADDIM
    hrow = lax.broadcasted_iota(jnp.int32, (LANES, GROUP_W), 0)
    hcol = lax.broadcasted_iota(jnp.int32, (LANES, GROUP_W), 1)

    for g in range(N_GROUPS):
        sl = slice(g * GROUP_W, (g + 1) * GROUP_W)
        sel = jnp.where(hrow == g * (GROUP_W // HEADDIM) + _div_pow2(hcol, HEADDIM), 1.0, 0.0).astype(BF16)
        ex = _dot3_r(stack, sel)
        dt_e, te_e, ea_e = ex[0:CHUNK], ex[CHUNK:2 * CHUNK], ex[2 * CHUNK:3 * CHUNK]

        x_g = xc_ref[:, sl]
        b_g = xc_ref[:, D_INNER + g * D_STATE:D_INNER + (g + 1) * D_STATE]
        c_g = xc_ref[:, D_INNER + BC_W + g * D_STATE:D_INNER + BC_W + (g + 1) * D_STATE]
        xdt = x_g * dt_e
        xdt_b = xdt.astype(BF16)
        cb16, bb16 = c_g.astype(BF16), b_g.astype(BF16)
        cb = _dot_nt(cb16, bb16)

        ht_g = ht_ref[:, sl]
        y_g = _dot(cb16, ht_g.astype(BF16)) * ea_e + dsk_ref[:, sl] * x_g
        states_t = _dot(b_g.T.astype(BF16), (xdt * te_e).astype(BF16))
        ht_ref[:, sl] = ea_e[CHUNK - 1:CHUNK, :] * ht_g + states_t

        y_pairs = []
        for jj in range(GROUP_W // LANES):
            xp = xdt_b[:, jj * LANES:(jj + 1) * LANES]
            halves = []
            for hh in range(2):
                h = g * (GROUP_W // HEADDIM) + 2 * jj + hh
                seg = a_cs[:, h:h + 1] - a_cs_t[h:h + 1, :]
                decay = jnp.exp2(jnp.where(causal, seg, -jnp.inf))
                halves.append(_dot((cb * decay).astype(BF16), xp))
            y_pairs.append(jnp.where(low_half, halves[0], halves[1]))
        y_g = y_g + jnp.concatenate(y_pairs, axis=1)

        y_ref[:, sl] = _gated_norm(y_g, zs_ref[:, sl], gn_ref[:, sl]).astype(BF16)

    @pl.when(ci == pl.num_programs(0) - 1)
    def _():
        for i in range(D_INNER // LANES):
            hout_ref[i * LANES:(i + 1) * LANES, :] = ht_ref[:, i * LANES:(i + 1) * LANES].T


def _ssd_prompt(xc, rest, dt_raw, dtb, alog, dsk, gn):
    nchunk = SEQ // CHUNK
    full = lambda r, c: pl.BlockSpec((r, c), lambda i: (0, 0))
    return pl.pallas_call(
        _ssd_kernel,
        grid=(nchunk,),
        in_specs=[pl.BlockSpec((CHUNK, CONV_DIM), lambda i: (i, 0)),
                  pl.BlockSpec((CHUNK, D_INNER), lambda i: (i, 0)),
                  pl.BlockSpec((CHUNK, LANES), lambda i: (i, 0)),
                  full(1, LANES), full(1, LANES), full(1, D_INNER), full(1, D_INNER)],
        out_specs=[pl.BlockSpec((CHUNK, D_INNER), lambda i: (i, 0)),
                   pl.BlockSpec((D_INNER, D_STATE), lambda i: (0, 0))],
        out_shape=[jax.ShapeDtypeStruct((SEQ, D_INNER), BF16),
                   jax.ShapeDtypeStruct((D_INNER, D_STATE), F32)],
        scratch_shapes=[pltpu.VMEM((D_STATE, D_INNER), F32)],
        compiler_params=_cparams("arbitrary"),
        name="ssd_prompt",
    )(xc, rest, dt_raw, dtb, alog, dsk, gn)


def _attn_kernel(sink_ref, q_ref, kp_ref, kc_ref, vp_ref, vc_ref, o_ref):
    i = pl.program_id(0)
    band = 2 * CHUNK
    l_idx = lax.broadcasted_iota(jnp.int32, (CHUNK, band), 0)
    s_idx = lax.broadcasted_iota(jnp.int32, (CHUNK, band), 1)
    first_key = jnp.where(i > 0, 0, CHUNK)
    mask = (s_idx >= jnp.maximum(l_idx, first_key)) & (s_idx <= l_idx + WINDOW)
    low_q = lax.broadcasted_iota(jnp.int32, (CHUNK, LANES), 1) < HEADDIM
    low_kv = lax.broadcasted_iota(jnp.int32, (band, LANES), 1) < HEADDIM

    for c in range(KV_DIM // LANES):
        csl = slice(c * LANES, (c + 1) * LANES)
        k2 = jnp.concatenate([kp_ref[:, csl], kc_ref[:, csl]], axis=0)
        v2 = jnp.concatenate([vp_ref[:, csl], vc_ref[:, csl]], axis=0)
        k2r = pltpu.roll(k2, HEADDIM, axis=1)
        v2r = pltpu.roll(v2, HEADDIM, axis=1)
        for u in range(2):
            kv_head = 2 * c + u
            kd = (jnp.where(low_kv, k2, k2r) if u == 0 else jnp.where(low_kv, k2r, k2)).astype(BF16)
            vd = (jnp.where(low_kv, v2, v2r) if u == 0 else jnp.where(low_kv, v2r, v2)).astype(BF16)
            for jj in range(Q_PER_KV // 2):
                j = kv_head * (Q_PER_KV // 2) + jj
                q2 = q_ref[:, j * LANES:(j + 1) * LANES] * (HEADDIM ** -0.5)
                halves = []
                for hh in range(2):
                    qm = jnp.where(low_q if hh == 0 else jnp.logical_not(low_q), q2, 0.0)
                    s = jnp.where(mask, _dot_nt(qm.astype(BF16), kd), -jnp.inf)
                    sink = sink_ref[2 * j + hh]
                    m = jnp.maximum(jnp.max(s, axis=-1, keepdims=True), sink)
                    p = jnp.exp(s - m)
                    den = jnp.sum(p, axis=-1, keepdims=True) + jnp.exp(sink - m)
                    halves.append(_dot((p / den).astype(BF16), vd))
                o_ref[:, j * LANES:(j + 1) * LANES] = jnp.where(low_q, halves[0], halves[1]).astype(BF16)


def _attn_prompt(proj, col0, sinks):
    nb = SEQ // CHUNK
    prev = lambda i: jnp.maximum(i - 1, 0)
    qb, kb, vb = (OFF_Q - col0) // D_MODEL, (OFF_K - col0) // KV_DIM, (OFF_V - col0) // KV_DIM
    return pl.pallas_call(
        _attn_kernel,
        grid=(nb,),
        in_specs=[pl.BlockSpec(memory_space=pltpu.SMEM),
                  pl.BlockSpec((CHUNK, D_MODEL), lambda i: (i, qb)),
                  pl.BlockSpec((CHUNK, KV_DIM), lambda i: (prev(i), kb)),
                  pl.BlockSpec((CHUNK, KV_DIM), lambda i: (i, kb)),
                  pl.BlockSpec((CHUNK, KV_DIM), lambda i: (prev(i), vb)),
                  pl.BlockSpec((CHUNK, KV_DIM), lambda i: (i, vb))],
        out_specs=pl.BlockSpec((CHUNK, D_MODEL), lambda i: (i, 0)),
        out_shape=jax.ShapeDtypeStruct((SEQ, D_MODEL), BF16),
        compiler_params=_cparams("parallel"),
        name="attn_prompt",
    )(sinks, proj, proj, proj, proj, proj)


def _alias_prev(prev, n_in, first_out):
    if prev is None:
        return [], {}, ()
    specs = [pl.BlockSpec(memory_space=pl.ANY)] * len(prev)
    return specs, {n_in + t: first_out + t for t in range(len(prev))}, tuple(prev)


def _conv_step_kernel(x_ref, dt_ref, cs_ref, cw_ref, cb_ref, dtb_ref, alog_ref, *rest):
    xc_ref, csn_ref, dto_ref, dao_ref = rest[-4:]
    taps = [cs_ref[k] for k in range(CONV_W - 1)] + [x_ref[...]]
    acc = cb_ref[...]
    for k in range(CONV_W):
        acc = acc + taps[k] * cw_ref[k:k + 1, :]
    xc_ref[...] = _silu(acc)
    for k in range(CONV_W - 1):
        csn_ref[k] = taps[k + 1]
    dt = _softplus(dt_ref[...] + dtb_ref[...])
    dto_ref[...] = dt
    dao_ref[...] = jnp.exp(dt * (-jnp.exp(alog_ref[...])))


def _conv_step(proj, conv_state, layer, cw, cb, dtb, alog, prev):
    nb = DEC_BATCH
    small = pl.BlockSpec((1, LANES), lambda s: (0, 0))
    head = pl.BlockSpec((nb, LANES), lambda s: (0, 0))
    state = pl.BlockSpec((None, CONV_W - 1, nb, GROUP_W), lambda s: (layer, 0, 0, s))
    in_specs = [pl.BlockSpec((nb, GROUP_W), lambda s: (0, OFF_X // GROUP_W + s)),
                pl.BlockSpec((nb, LANES), lambda s: (0, OFF_DT // LANES)),
                state,
                pl.BlockSpec((CONV_W, GROUP_W), lambda s: (0, s)),
                pl.BlockSpec((1, GROUP_W), lambda s: (0, s)),
                small, small]
    alias_specs, aliases, alias_args = _alias_prev(prev, len(in_specs), 1)
    return pl.pallas_call(
        _conv_step_kernel,
        grid=(CONV_DIM // GROUP_W,),
        in_specs=in_specs + alias_specs,
        out_specs=[pl.BlockSpec((nb, GROUP_W), lambda s: (0, s)), state, head, head],
        out_shape=[jax.ShapeDtypeStruct((nb, CONV_DIM), F32),
                   jax.ShapeDtypeStruct(conv_state.shape, F32),
                   jax.ShapeDtypeStruct((nb, LANES), F32),
                   jax.ShapeDtypeStruct((nb, LANES), F32)],
        input_output_aliases=aliases,
        compiler_params=_cparams("arbitrary"),
        name="conv_step",
    )(proj, proj, conv_state, cw, cb, dtb, alog, *alias_args)


STEP_B = 8
STEP_GROUPS = 4
STEP_W = STEP_GROUPS * GROUP_W


def _ssm_step_kernel(xs_ref, b_ref, c_ref, z_ref, dt_ref, da_ref, dsk_ref, gn_ref, h_ref, *rest):
    y_ref, hn_ref = rest[-2:]
    part = pl.program_id(1)
    xs = xs_ref[...]
    hrow = lax.broadcasted_iota(jnp.int32, (LANES, STEP_W), 0)
    hcol = lax.broadcasted_iota(jnp.int32, (LANES, STEP_W), 1)
    sel = jnp.where(hrow == part * (STEP_W // HEADDIM) + _div_pow2(hcol, HEADDIM), 1.0, 0.0).astype(BF16)
    dt_e = _dot3_r(dt_ref[...], sel)
    da_e = _dot3_r(da_ref[...], sel)
    xdt = xs * dt_e
    lane_group = _div_pow2(lax.broadcasted_iota(jnp.int32, (STEP_B, STEP_W), 1), GROUP_W)

    pieces = [jnp.where(lane_group == g, xdt, 0.0) for g in range(STEP_GROUPS)]
    pieces += [p.astype(F32) for p in _split3(da_e)]
    n_used = len(pieces) * STEP_B
    pieces.append(jnp.zeros((LANES - n_used, STEP_W), F32))
    ut = jnp.concatenate(pieces, axis=0)
    u_big = jnp.concatenate([ut[:, i * LANES:(i + 1) * LANES].T for i in range(STEP_W // LANES)],
                            axis=0).astype(BF16)

    b_rows = jnp.concatenate([b_ref[:, g * D_STATE:(g + 1) * D_STATE] for g in range(STEP_GROUPS)]
                             + [jnp.zeros((LANES - STEP_GROUPS * STEP_B, D_STATE), F32)], axis=0)
    rrow = lax.broadcasted_iota(jnp.int32, (LANES, D_STATE), 0)
    ones_rows = jnp.where((rrow >= STEP_GROUPS * STEP_B) & (rrow < n_used), 1.0, 0.0)
    big_r = jnp.concatenate([b_rows, ones_rows], axis=1)
    r_seq = _mod_pow2(lax.broadcasted_iota(jnp.int32, (LANES, 2 * D_STATE), 0), STEP_B)

    cc = jnp.concatenate([c_ref[:, g * D_STATE:(g + 1) * D_STATE] for g in range(STEP_GROUPS)],
                         axis=0).astype(BF16)
    crow = lax.broadcasted_iota(jnp.int32, (STEP_GROUPS * STEP_B, STEP_W), 0)
    ccol_group = _div_pow2(lax.broadcasted_iota(jnp.int32, (STEP_GROUPS * STEP_B, STEP_W), 1), GROUP_W)
    sub = lax.broadcasted_iota(jnp.int32, (STEP_B, STEP_W), 0)

    y_off = jnp.zeros((STEP_B, STEP_W), F32)
    for b in range(STEP_B):
        h0 = h_ref[b]
        res = _dot(u_big, jnp.where(r_seq == b, big_r, 0.0).astype(BF16))
        hn_ref[b] = res[:, D_STATE:] * h0 + res[:, :D_STATE]
        y_all = _dot_nt(cc, h0.astype(BF16))
        pick = (crow == ccol_group * STEP_B + b)
        y_b = jnp.sum(jnp.where(pick, y_all, 0.0), axis=0, keepdims=True)
        y_off = jnp.where(sub == b, y_b, y_off)

    cb_e = jnp.zeros((STEP_B, STEP_W), F32)
    for g in range(STEP_GROUPS):
        gsl = slice(g * D_STATE, (g + 1) * D_STATE)
        cb_g = jnp.sum(c_ref[:, gsl] * b_ref[:, gsl], axis=-1, keepdims=True)
        cb_e = jnp.where(lane_group == g, cb_g, cb_e)

    y = da_e * y_off + cb_e * xdt + dsk_ref[...] * xs
    for g in range(STEP_GROUPS):
        sl = slice(g * GROUP_W, (g + 1) * GROUP_W)
        y_ref[:, sl] = _gated_norm(y[:, sl], _silu(z_ref[:, sl]), gn_ref[:, sl])


def _ssm_step(xc, proj, dt, da, dsk, gn, state, layer, prev):
    nparts = D_INNER // STEP_W
    state_spec = pl.BlockSpec((None, STEP_B, STEP_W, D_STATE), lambda i, p: (layer, i, p, 0))
    in_specs = [pl.BlockSpec((STEP_B, STEP_W), lambda i, p: (i, p)),
                pl.BlockSpec((STEP_B, STEP_GROUPS * D_STATE),
                             lambda i, p: (i, D_INNER // (STEP_GROUPS * D_STATE) + p)),
                pl.BlockSpec((STEP_B, STEP_GROUPS * D_STATE),
                             lambda i, p: (i, (D_INNER + BC_W) // (STEP_GROUPS * D_STATE) + p)),
                pl.BlockSpec((STEP_B, STEP_W), lambda i, p: (i, OFF_Z // STEP_W + p)),
                pl.BlockSpec((STEP_B, LANES), lambda i, p: (i, 0)),
                pl.BlockSpec((STEP_B, LANES), lambda i, p: (i, 0)),
                pl.BlockSpec((1, STEP_W), lambda i, p: (0, p)),
                pl.BlockSpec((1, STEP_W), lambda i, p: (0, p)),
                state_spec]
    alias_specs, aliases, alias_args = _alias_prev(prev, len(in_specs), 1)
    return pl.pallas_call(
        _ssm_step_kernel,
        grid=(DEC_BATCH // STEP_B, nparts),
        in_specs=in_specs + alias_specs,
        out_specs=[pl.BlockSpec((STEP_B, STEP_W), lambda i, p: (i, p)), state_spec],
        out_shape=[jax.ShapeDtypeStruct((DEC_BATCH, D_INNER), F32),
                   jax.ShapeDtypeStruct(state.shape, F32)],
        input_output_aliases=aliases,
        compiler_params=_cparams("parallel", "parallel"),
        name="ssm_step",
    )(xc, xc, xc, proj, dt, da, dsk, gn, state, *alias_args)


ATTN_B = 8


def _attn_step_kernel(sink_ref, q_ref, kn_ref, vn_ref, kc_ref, vc_ref, *rest):
    o_ref, ko_ref, vo_ref = rest[-3:]
    lane_kv = _div_pow2(lax.broadcasted_iota(jnp.int32, (N_KV, KV_DIM), 1), HEADDIM)
    own = lane_kv == lax.broadcasted_iota(jnp.int32, (N_KV, KV_DIM), 0)
    newest = lax.broadcasted_iota(jnp.int32, (KV_DIM, WINDOW), 1) == WINDOW - 1
    sink = sink_ref[...]
    pad = jnp.zeros((LANES - ATTN_B, KV_DIM), F32)
    kn_t = jnp.concatenate([kn_ref[...], pad], axis=0).T
    vn_t = jnp.concatenate([vn_ref[...], pad], axis=0).T
    for b in range(ATTN_B):
        qrow = q_ref[b:b + 1, :] * (HEADDIM ** -0.5)
        qm = jnp.concatenate(
            [jnp.where(own, jnp.broadcast_to(qrow[:, r * KV_DIM:(r + 1) * KV_DIM], (N_KV, KV_DIM)), 0.0)
             for r in range(Q_PER_KV)], axis=0)
        k_new = kn_ref[b:b + 1, :]
        v_new = vn_ref[b:b + 1, :]
        kt = kc_ref[b]
        vt = vc_ref[b]
        s = _dot(qm.astype(BF16), kt.astype(BF16))
        s_new = jnp.sum(qm * k_new, axis=-1, keepdims=True)
        m = jnp.maximum(jnp.maximum(jnp.max(s, axis=-1, keepdims=True), s_new), sink)
        p = jnp.exp(s - m)
        p_new = jnp.exp(s_new - m)
        den = jnp.sum(p, axis=-1, keepdims=True) + p_new + jnp.exp(sink - m)
        o_full = _dot_nt((p / den).astype(BF16), vt.astype(BF16)) + (p_new / den) * v_new
        parts = []
        for r in range(Q_PER_KV):
            o_r = jnp.where(own, o_full[r * N_KV:(r + 1) * N_KV, :], 0.0)
            parts.append(jnp.sum(o_r, axis=0, keepdims=True))
        o_ref[b:b + 1, :] = jnp.concatenate(parts, axis=1)
        ko_ref[b] = jnp.where(newest, jnp.broadcast_to(kn_t[:, b:b + 1], (KV_DIM, WINDOW)),
                              pltpu.roll(kt, WINDOW - 1, axis=1))
        vo_ref[b] = jnp.where(newest, jnp.broadcast_to(vn_t[:, b:b + 1], (KV_DIM, WINDOW)),
                              pltpu.roll(vt, WINDOW - 1, axis=1))


def _attn_step(q, proj, sinks, cache_k, cache_v, layer, prev):
    cache_spec = pl.BlockSpec((None, ATTN_B, KV_DIM, WINDOW), lambda i: (layer, i, 0, 0))
    in_specs = [pl.BlockSpec((N_HEADS, 1), lambda i: (0, 0)),
                pl.BlockSpec((ATTN_B, D_MODEL), lambda i: (i, 0)),
                pl.BlockSpec((ATTN_B, KV_DIM), lambda i: (i, OFF_K // KV_DIM)),
                pl.BlockSpec((ATTN_B, KV_DIM), lambda i: (i, OFF_V // KV_DIM)),
                cache_spec, cache_spec]
    alias_specs, aliases, alias_args = _alias_prev(prev, len(in_specs), 1)
    return pl.pallas_call(
        _attn_step_kernel,
        grid=(DEC_BATCH // ATTN_B,),
        in_specs=in_specs + alias_specs,
        out_specs=[pl.BlockSpec((ATTN_B, D_MODEL), lambda i: (i, 0)), cache_spec, cache_spec],
        out_shape=[jax.ShapeDtypeStruct((DEC_BATCH, D_MODEL), F32),
                   jax.ShapeDtypeStruct(cache_k.shape, F32),
                   jax.ShapeDtypeStruct(cache_v.shape, F32)],
        input_output_aliases=aliases,
        compiler_params=_cparams("parallel"),
        name="attn_step",
    )(sinks, q, proj, proj, cache_k, cache_v, *alias_args)


def _cast_kernel(x_ref, o_ref):
    o_ref[...] = x_ref[...].astype(BF16)


def _cast_bf16(w, tk, name):
    d, k, n = w.shape
    spec = pl.BlockSpec((None, tk, n), lambda l, i: (l, i, 0))
    return pl.pallas_call(
        _cast_kernel,
        grid=(d, k // tk),
        in_specs=[spec],
        out_specs=spec,
        out_shape=jax.ShapeDtypeStruct(w.shape, BF16),
        compiler_params=_cparams("parallel", "parallel"),
        name=name,
    )(w)


SRC_X = D_INNER
SRC_DT = SRC_X + CONV_DIM
W_TILE = 512
N_W_TILES = PROJ_W // W_TILE


def _w_in_layout_kernel(x_ref, o_ref):
    t = pl.program_id(1)
    valid = jnp.where(t == N_W_TILES - 1, N_SSM_HEADS, W_TILE)
    row = lax.broadcasted_iota(jnp.int32, (W_TILE, D_MODEL), 0)
    o_ref[...] = jnp.where(row < valid, x_ref[0], 0.0).T.astype(BF16)


def _w_in_layout(w_t):
    d = w_t.shape[0]

    def src_row(l, t):
        per_tile = W_TILE // N_SSM_HEADS
        unit = jnp.where(
            t < OFF_Z // W_TILE, SRC_X // N_SSM_HEADS + t * per_tile,
            jnp.where(t < OFF_Q // W_TILE, (t - OFF_Z // W_TILE) * per_tile,
                      jnp.where(t < N_W_TILES - 1, t * per_tile + 1, SRC_DT // N_SSM_HEADS)))
        return (l, unit * N_SSM_HEADS, 0)

    return pl.pallas_call(
        _w_in_layout_kernel,
        grid=(d, N_W_TILES),
        in_specs=[pl.BlockSpec((pl.Element(1), pl.Element(W_TILE), pl.Element(D_MODEL)), src_row)],
        out_specs=pl.BlockSpec((None, D_MODEL, W_TILE), lambda l, t: (l, 0, t)),
        out_shape=jax.ShapeDtypeStruct((d, D_MODEL, PROJ_W), BF16),
        compiler_params=_cparams("parallel", "parallel"),
        name="w_in_layout",
    )(w_t)


def _small_params(l, norm_mix, conv_w, conv_b, dt_bias, a_log, d_skip, ssm_norm, attn_sinks, norm_ffn):
    pad = lambda v: jnp.pad(v, (0, LANES - N_SSM_HEADS)).reshape(1, LANES)
    return dict(
        norm_mix=norm_mix[l].reshape(1, D_MODEL),
        conv_w=conv_w[l], conv_b=conv_b[l].reshape(1, CONV_DIM),
        dt_bias=pad(dt_bias[l]), a_log=pad(a_log[l]),
        d_skip=jnp.repeat(d_skip[l], HEADDIM).reshape(1, D_INNER),
        ssm_norm=ssm_norm[l].reshape(1, D_INNER),
        sinks=attn_sinks[l],
        sinks_rm=attn_sinks[l].reshape(N_KV, Q_PER_KV).T.reshape(N_HEADS, 1),
        norm_ffn=norm_ffn[l].reshape(1, D_MODEL))


def _dense_tail(x, y, o, proj, col0, p, w, l, tm):
    merged = _merge(y, o, proj, col0, w["w_ssm"], w["w_attn"], l, tm, 512)
    x1 = _matmul_res(merged, w["w_out"], l, x, min(tm, 512), D_MODEL, "out_proj")
    hmid = _ffn_up(x1, p["norm_ffn"], w["w_gate"], w["w_up"], l, tm, 512)
    return _matmul_res(hmid, w["w_down"], l, x1, tm, 512, "ffn_down")


PROMPT_TM = 1024
PROJ_TN = 1280


def kernel(x_prompt, x_sample, state_ssm, state_conv, cache_k, cache_v, norm_mix, w_in, conv_w,
           conv_b, dt_bias, a_log, d_skip, ssm_norm, w_ssm_proj, w_attn_proj, attn_sinks, w_out,
           norm_ffn, w_gate, w_up, w_down, norm_final):
    xp = x_prompt.reshape(SEQ, D_MODEL)
    xs = x_sample.reshape(DEC_BATCH, D_MODEL)
    state4 = state_ssm.reshape(DEPTH, DEC_BATCH, D_INNER, D_STATE)
    to_kt = lambda c: c.transpose(0, 1, 3, 4, 2).reshape(DEPTH, DEC_BATCH, KV_DIM, WINDOW)
    ck, cv = to_kt(cache_k), to_kt(cache_v)
    conv_t = state_conv.transpose(0, 2, 1, 3)
    outs = {k: [] for k in ("p_ssm", "p_conv", "p_k", "p_v")}

    w = dict(w_in=_w_in_layout(w_in.transpose(0, 2, 1)),
             w_attn=_cast_bf16(w_attn_proj, 1024, "cast_w_attn"),
             w_ssm=_cast_bf16(w_ssm_proj, 1024, "cast_w_ssm"),
             w_out=_cast_bf16(w_out, 1024, "cast_w_out"),
             w_gate=_cast_bf16(w_gate, 256, "cast_w_gate"),
             w_up=_cast_bf16(w_up, 256, "cast_w_up"),
             w_down=_cast_bf16(w_down, 512, "cast_w_down"))

    h_all = conv_all = kv_all = None
    for l in range(DEPTH):
        p = _small_params(l, norm_mix, conv_w, conv_b, dt_bias, a_log, d_skip, ssm_norm,
                          attn_sinks, norm_ffn)

        xc_p, dt_p, tails = _in_proj_conv(xp, p["norm_mix"], w["w_in"], l, p["conv_w"], p["conv_b"],
                                          PROMPT_TM, 1024)
        rest = _in_proj_rest(xp, p["norm_mix"], w["w_in"], l, PROMPT_TM, 1024)
        y, h_p = _ssd_prompt(xc_p, rest, dt_p, p["dt_bias"], p["a_log"], p["d_skip"], p["ssm_norm"])
        o = _attn_prompt(rest, OFF_Z, p["sinks"])
        last_kv = lambda off: rest[SEQ - WINDOW:, off - OFF_Z:off - OFF_Z + KV_DIM].reshape(
            1, WINDOW, N_KV, HEADDIM)
        outs["p_ssm"].append(h_p.reshape(1, N_SSM_HEADS, HEADDIM, D_STATE))
        outs["p_conv"].append(tails[-1, SUBLANES - (CONV_W - 1):].reshape(1, CONV_W - 1, CONV_DIM))
        outs["p_k"].append(last_kv(OFF_K))
        outs["p_v"].append(last_kv(OFF_V))
        xp = _dense_tail(xp, y, o, rest, OFF_Z, p, w, l, PROMPT_TM)

        proj_s = _in_proj(xs, p["norm_mix"], w["w_in"], l, DEC_BATCH, PROJ_TN)
        xc, conv_all, dt, da = _conv_step(proj_s, conv_t, l, p["conv_w"], p["conv_b"],
                                          p["dt_bias"], p["a_log"],
                                          None if conv_all is None else (conv_all,))
        y_s, h_all = _ssm_step(xc, proj_s, dt, da, p["d_skip"], p["ssm_norm"], state4, l,
                               None if h_all is None else (h_all,))
        swap_rm = lambda t, a, b: t.reshape(DEC_BATCH, a, b, HEADDIM).transpose(0, 2, 1, 3).reshape(
            DEC_BATCH, D_MODEL)
        q_s = swap_rm(proj_s[:, OFF_Q:OFF_Q + D_MODEL], N_KV, Q_PER_KV)
        o_s, k_all, v_all = _attn_step(q_s, proj_s, p["sinks_rm"], ck, cv, l, kv_all)
        o_s = swap_rm(o_s, Q_PER_KV, N_KV)
        kv_all = (k_all, v_all)
        xs = _dense_tail(xs, y_s, o_s, proj_s, 0, p, w, l, DEC_BATCH)

    y_prompt = _rmsnorm(xp, norm_final.reshape(1, D_MODEL), PROMPT_TM).reshape(1, SEQ, D_MODEL)
    y_sample = _rmsnorm(xs, norm_final.reshape(1, D_MODEL), DEC_BATCH).reshape(DEC_BATCH, 1, D_MODEL)
    st = lambda k: jnp.stack(outs[k])
    from_kt = lambda c: c.reshape(DEPTH, DEC_BATCH, N_KV, HEADDIM, WINDOW).transpose(0, 1, 4, 2, 3)
    return (y_prompt, y_sample, st("p_ssm"), st("p_conv"), st("p_k"), st("p_v"),
            h_all.reshape(DEPTH, DEC_BATCH, N_SSM_HEADS, HEADDIM, D_STATE),
            conv_all.transpose(0, 2, 1, 3), from_kt(kv_all[0]), from_kt(kv_all[1]))
```

```python
import functools

import jax
import jax.numpy as jnp
from jax import lax
from jax.experimental import pallas as pl
from jax.experimental.pallas import tpu as pltpu

F32 = jnp.float32
BF16 = jnp.bfloat16

D_MODEL = 2048
SEQ = 8192
DEPTH = 2
DEC_BATCH = 128
D_INNER = 4096
HEADDIM = 64
N_SSM_HEADS = 64
D_STATE = 128
N_GROUPS = 8
GROUP_W = D_INNER // N_GROUPS
CONV_W = 4
BC_W = N_GROUPS * D_STATE
CONV_DIM = D_INNER + 2 * BC_W
CHUNK = 128
N_HEADS = 32
N_KV = 8
Q_PER_KV = 4
KV_DIM = N_KV * HEADDIM
WINDOW = 128
D_FF = 5632
EPS = 1e-6
IN_DIM = 17472

OFF_X = 0
OFF_B = OFF_X + D_INNER
OFF_C = OFF_B + BC_W
OFF_Z = CONV_DIM
OFF_Q = 10240
OFF_K = 12288
OFF_V = 12800
OFF_GS = 13312
OFF_GA = 15360
OFF_DT = 17408
PROJ_W = 17920

LANES = 128
SUBLANES = 8
VMEM_LIMIT = 56 * 1024 * 1024


def _cparams(*sem):
    return pltpu.CompilerParams(dimension_semantics=sem, vmem_limit_bytes=VMEM_LIMIT)


def _dot(a, b):
    return jnp.dot(a, b, preferred_element_type=F32)


def _dot_nt(a, b):
    return lax.dot_general(a, b, (((1,), (1,)), ((), ())), preferred_element_type=F32)


def _div_pow2(v, d):
    return jnp.right_shift(v, d.bit_length() - 1)


def _mod_pow2(v, d):
    return jnp.bitwise_and(v, d - 1)


def _split3(x):
    hi = x.astype(BF16)
    r1 = x - hi.astype(F32)
    mid = r1.astype(BF16)
    lo = (r1 - mid.astype(F32)).astype(BF16)
    return hi, mid, lo


def _dot3_r(x, sel):
    return _dot(jnp.concatenate(_split3(x), axis=1), jnp.concatenate([sel, sel, sel], axis=0))


def _dot3_l(sel, x):
    return _dot(jnp.concatenate([sel, sel, sel], axis=1), jnp.concatenate(_split3(x), axis=0))


def _silu(x):
    h = 0.5 * x
    return h + h * jnp.tanh(h)


def _softplus(x):
    return jnp.maximum(x, 0.0) + jnp.log1p(jnp.exp(-jnp.abs(x)))


def _rms_scale(x, g):
    r = lax.rsqrt(jnp.mean(x * x, axis=-1, keepdims=True) + EPS)
    return x * r * g


def _norm_matmul_kernel(x_ref, g_ref, w_ref, o_ref, xn_ref):
    @pl.when(pl.program_id(1) == 0)
    def _():
        xn_ref[...] = _rms_scale(x_ref[...], g_ref[...]).astype(BF16)

    o_ref[...] = _dot(xn_ref[...], w_ref[...])


def _in_proj(x, g, w, layer, tm, tn):
    m, k = x.shape
    n = w.shape[2]
    return pl.pallas_call(
        _norm_matmul_kernel,
        grid=(m // tm, n // tn),
        in_specs=[pl.BlockSpec((tm, k), lambda i, j: (i, 0)),
                  pl.BlockSpec((1, k), lambda i, j: (0, 0)),
                  pl.BlockSpec((None, k, tn), lambda i, j: (layer, 0, j))],
        out_specs=pl.BlockSpec((tm, tn), lambda i, j: (i, j)),
        out_shape=jax.ShapeDtypeStruct((m, n), F32),
        scratch_shapes=[pltpu.VMEM((tm, k), BF16)],
        compiler_params=_cparams("parallel", "arbitrary"),
        name="in_proj",
    )(x, g, w)


MXU_N = 256
REST_W = OFF_DT - OFF_Z
CONV_ROWS = 64


def _proj_conv_kernel(x_ref, g_ref, w_ref, wdt_ref, cw_ref, cb_ref, xc_ref, dt_ref, tail_ref,
                      xn_ref, xpad_ref, carry_ref):
    i = pl.program_id(0)
    j = pl.program_id(1)
    tm, tn = xc_ref.shape

    @pl.when(j == 0)
    def _():
        xn = _rms_scale(x_ref[...], g_ref[...]).astype(BF16)
        xn_ref[...] = xn
        dt_ref[...] = _dot(xn, wdt_ref[...])

    @pl.when(i == 0)
    def _():
        carry_ref[j] = jnp.zeros((SUBLANES, tn), F32)

    xpad_ref[0:SUBLANES, :] = carry_ref[j]
    for cs in range(tn // MXU_N):
        sl = slice(cs * MXU_N, (cs + 1) * MXU_N)
        xpad_ref[SUBLANES:SUBLANES + tm, sl] = _dot(xn_ref[...], w_ref[:, sl])
        for c0 in range(cs * MXU_N, (cs + 1) * MXU_N, LANES):
            cl = slice(c0, c0 + LANES)
            for r0 in range(0, tm, CONV_ROWS):
                xp = xpad_ref[r0:r0 + SUBLANES + CONV_ROWS, cl]
                acc = cb_ref[:, cl] + xp[SUBLANES:] * cw_ref[CONV_W - 1:CONV_W, cl]
                for back in range(1, CONV_W):
                    k = CONV_W - 1 - back
                    acc = acc + pltpu.roll(xp, back, axis=0)[SUBLANES:] * cw_ref[k:k + 1, cl]
                xc_ref[r0:r0 + CONV_ROWS, cl] = _silu(acc)
    tail = xpad_ref[tm:tm + SUBLANES, :]
    carry_ref[j] = tail
    tail_ref[...] = tail


def _in_proj_conv(x, g, w, layer, cw, cb, tm, tn):
    m, k = x.shape
    ntile = CONV_DIM // tn
    return pl.pallas_call(
        _proj_conv_kernel,
        grid=(m // tm, ntile),
        in_specs=[pl.BlockSpec((tm, k), lambda i, j: (i, 0)),
                  pl.BlockSpec((1, k), lambda i, j: (0, 0)),
                  pl.BlockSpec((None, k, tn), lambda i, j: (layer, 0, OFF_X // tn + j)),
                  pl.BlockSpec((None, k, LANES), lambda i, j: (layer, 0, OFF_DT // LANES)),
                  pl.BlockSpec((CONV_W, tn), lambda i, j: (0, j)),
                  pl.BlockSpec((1, tn), lambda i, j: (0, j))],
        out_specs=[pl.BlockSpec((tm, tn), lambda i, j: (i, j)),
                   pl.BlockSpec((tm, LANES), lambda i, j: (i, 0)),
                   pl.BlockSpec((None, SUBLANES, tn), lambda i, j: (i, 0, j))],
        out_shape=[jax.ShapeDtypeStruct((m, CONV_DIM), F32),
                   jax.ShapeDtypeStruct((m, LANES), F32),
                   jax.ShapeDtypeStruct((m // tm, SUBLANES, CONV_DIM), F32)],
        scratch_shapes=[pltpu.VMEM((tm, k), BF16),
                        pltpu.VMEM((tm + SUBLANES, tn), F32),
                        pltpu.VMEM((ntile, SUBLANES, tn), F32)],
        compiler_params=_cparams("arbitrary", "arbitrary"),
        name="in_proj_conv",
    )(x, g, w, w, cw, cb)


def _proj_rest_kernel(x_ref, g_ref, w_ref, o_ref, xn_ref, *, n_silu_tiles):
    j = pl.program_id(1)
    tn = o_ref.shape[1]

    @pl.when(j == 0)
    def _():
        xn_ref[...] = _rms_scale(x_ref[...], g_ref[...]).astype(BF16)

    @pl.when(j < n_silu_tiles)
    def _():
        for cs in range(tn // MXU_N):
            sl = slice(cs * MXU_N, (cs + 1) * MXU_N)
            o_ref[:, sl] = _silu(_dot(xn_ref[...], w_ref[:, sl]))

    @pl.when(j >= n_silu_tiles)
    def _():
        o_ref[...] = _dot(xn_ref[...], w_ref[...])


def _in_proj_rest(x, g, w, layer, tm, tn):
    m, k = x.shape
    return pl.pallas_call(
        functools.partial(_proj_rest_kernel, n_silu_tiles=D_INNER // tn),
        grid=(m // tm, REST_W // tn),
        in_specs=[pl.BlockSpec((tm, k), lambda i, j: (i, 0)),
                  pl.BlockSpec((1, k), lambda i, j: (0, 0)),
                  pl.BlockSpec((None, k, tn), lambda i, j: (layer, 0, OFF_Z // tn + j))],
        out_specs=pl.BlockSpec((tm, tn), lambda i, j: (i, j)),
        out_shape=jax.ShapeDtypeStruct((m, REST_W), F32),
        scratch_shapes=[pltpu.VMEM((tm, k), BF16)],
        compiler_params=_cparams("parallel", "arbitrary"),
        name="in_proj_rest",
    )(x, g, w)


def _ffn_up_kernel(x_ref, g_ref, wg_ref, wu_ref, o_ref, xn_ref):
    @pl.when(pl.program_id(1) == 0)
    def _():
        xn_ref[...] = _rms_scale(x_ref[...], g_ref[...]).astype(BF16)

    xn = xn_ref[...]
    o_ref[...] = (_silu(_dot(xn, wg_ref[...])) * _dot(xn, wu_ref[...])).astype(BF16)


def _ffn_up(x, g, wg, wu, layer, tm, tn):
    m, k = x.shape
    n = wg.shape[2]
    return pl.pallas_call(
        _ffn_up_kernel,
        grid=(m // tm, n // tn),
        in_specs=[pl.BlockSpec((tm, k), lambda i, j: (i, 0)),
                  pl.BlockSpec((1, k), lambda i, j: (0, 0)),
                  pl.BlockSpec((None, k, tn), lambda i, j: (layer, 0, j)),
                  pl.BlockSpec((None, k, tn), lambda i, j: (layer, 0, j))],
        out_specs=pl.BlockSpec((tm, tn), lambda i, j: (i, j)),
        out_shape=jax.ShapeDtypeStruct((m, n), BF16),
        scratch_shapes=[pltpu.VMEM((tm, k), BF16)],
        compiler_params=_cparams("parallel", "arbitrary"),
        name="ffn_up",
    )(x, g, wg, wu)


def _matmul_res_kernel(a_ref, w_ref, r_ref, o_ref):
    o_ref[...] = r_ref[...] + _dot(a_ref[...], w_ref[...])


def _matmul_res(a, w, layer, res, tm, tn, name):
    m, k = a.shape
    n = w.shape[2]
    return pl.pallas_call(
        _matmul_res_kernel,
        grid=(m // tm, n // tn),
        in_specs=[pl.BlockSpec((tm, k), lambda i, j: (i, 0)),
                  pl.BlockSpec((None, k, tn), lambda i, j: (layer, 0, j)),
                  pl.BlockSpec((tm, tn), lambda i, j: (i, j))],
        out_specs=pl.BlockSpec((tm, tn), lambda i, j: (i, j)),
        out_shape=jax.ShapeDtypeStruct((m, n), F32),
        compiler_params=_cparams("parallel", "parallel"),
        name=name,
    )(a, w, res)


def _ffn_down_kernel(a_ref, w_ref, r_ref, g_ref, o_ref, *, final_norm):
    y = r_ref[...] + _dot(a_ref[...], w_ref[...])
    o_ref[...] = _rms_scale(y, g_ref[...]) if final_norm else y


def _ffn_down(a, w, layer, res, gain, final_norm, tm):
    m, k = a.shape
    n = w.shape[2]
    return pl.pallas_call(
        functools.partial(_ffn_down_kernel, final_norm=final_norm),
        grid=(m // tm,),
        in_specs=[pl.BlockSpec((tm, k), lambda i: (i, 0)),
                  pl.BlockSpec((None, k, n), lambda i: (layer, 0, 0), pipeline_mode=pl.Buffered(1)),
                  pl.BlockSpec((tm, n), lambda i: (i, 0)),
                  pl.BlockSpec((1, n), lambda i: (0, 0))],
        out_specs=pl.BlockSpec((tm, n), lambda i: (i, 0)),
        out_shape=jax.ShapeDtypeStruct((m, n), F32),
        compiler_params=_cparams("parallel"),
        name="ffn_down",
    )(a, w, res, gain)


def _merge_kernel(y_ref, o_ref, gs_ref, ga_ref, ws_ref, wa_ref, out_ref):
    ys = _dot(y_ref[...].astype(BF16), ws_ref[...])
    oa = _dot(o_ref[...].astype(BF16), wa_ref[...])
    out_ref[...] = (jax.nn.sigmoid(gs_ref[...]) * ys
                    + jax.nn.sigmoid(ga_ref[...]) * oa).astype(BF16)


def _merge(y, o, proj, col0, ws, wa, layer, tm, tn):
    m = y.shape[0]
    gs0, ga0 = (OFF_GS - col0) // tn, (OFF_GA - col0) // tn
    return pl.pallas_call(
        _merge_kernel,
        grid=(m // tm, D_MODEL // tn),
        in_specs=[pl.BlockSpec((tm, D_INNER), lambda i, j: (i, 0)),
                  pl.BlockSpec((tm, D_MODEL), lambda i, j: (i, 0)),
                  pl.BlockSpec((tm, tn), lambda i, j: (i, gs0 + j)),
                  pl.BlockSpec((tm, tn), lambda i, j: (i, ga0 + j)),
                  pl.BlockSpec((None, D_INNER, tn), lambda i, j: (layer, 0, j)),
                  pl.BlockSpec((None, D_MODEL, tn), lambda i, j: (layer, 0, j))],
        out_specs=pl.BlockSpec((tm, tn), lambda i, j: (i, j)),
        out_shape=jax.ShapeDtypeStruct((m, D_MODEL), BF16),
        compiler_params=_cparams("parallel", "parallel"),
        name="merge",
    )(y, o, proj, proj, ws, wa)


def _gated_norm(y, gate, gain):
    h = y * gate
    r = lax.rsqrt(jnp.mean(h * h, axis=-1, keepdims=True) + EPS)
    return h * r * gain


LOG2E = 1.4426950408889634


def _ssd_kernel(xc_ref, zs_ref, dt_ref, dtb_ref, alog_ref, dsk_ref, gn_ref, y_ref, hout_ref,
                ht_ref):
    ci = pl.program_id(0)

    @pl.when(ci == 0)
    def _():
        ht_ref[...] = jnp.zeros_like(ht_ref)

    dt = _softplus(dt_ref[...] + dtb_ref[...])
    a = dt * (-jnp.exp(alog_ref[...])) * LOG2E
    row = lax.broadcasted_iota(jnp.int32, (CHUNK, CHUNK), 0)
    col = lax.broadcasted_iota(jnp.int32, (CHUNK, CHUNK), 1)
    causal = col <= row
    tri = jnp.where(causal, 1.0, 0.0).astype(BF16)
    a_cs = _dot3_l(tri, a)
    a_cs_t = a_cs.T
    ea = jnp.exp2(a_cs)
    to_end = jnp.exp2(a_cs[CHUNK - 1:CHUNK, :] - a_cs)
    stack = jnp.concatenate([dt, to_end, ea], axis=0)

    lane = lax.broadcasted_iota(jnp.int32, (CHUNK, LANES), 1)
    low_half = lane < HEADDIM
    hrow = lax.broadcasted_iota(jnp.int32, (LANES, GROUP_W), 0)
    hcol = lax.broadcasted_iota(jnp.int32, (LANES, GROUP_W), 1)

    for g in range(N_GROUPS):
        sl = slice(g * GROUP_W, (g + 1) * GROUP_W)
        sel = jnp.where(hrow == g * (GROUP_W // HEADDIM) + _div_pow2(hcol, HEADDIM), 1.0, 0.0).astype(BF16)
        ex = _dot3_r(stack, sel)
        dt_e, te_e, ea_e = ex[0:CHUNK], ex[CHUNK:2 * CHUNK], ex[2 * CHUNK:3 * CHUNK]

        x_g = xc_ref[:, sl]
        b_g = xc_ref[:, D_INNER + g * D_STATE:D_INNER + (g + 1) * D_STATE]
        c_g = xc_ref[:, D_INNER + BC_W + g * D_STATE:D_INNER + BC_W + (g + 1) * D_STATE]
        xdt = x_g * dt_e
        xdt_b = xdt.astype(BF16)
        cb16, bb16 = c_g.astype(BF16), b_g.astype(BF16)
        cb = _dot_nt(cb16, bb16)

        ht_g = ht_ref[:, sl]
        y_g = _dot(cb16, ht_g.astype(BF16)) * ea_e + dsk_ref[:, sl] * x_g
        states_t = _dot(b_g.T.astype(BF16), (xdt * te_e).astype(BF16))
        ht_ref[:, sl] = ea_e[CHUNK - 1:CHUNK, :] * ht_g + states_t

        y_pairs = []
        for jj in range(GROUP_W // LANES):
            xp = xdt_b[:, jj * LANES:(jj + 1) * LANES]
            halves = []
            for hh in range(2):
                h = g * (GROUP_W // HEADDIM) + 2 * jj + hh
                seg = a_cs[:, h:h + 1] - a_cs_t[h:h + 1, :]
                decay = jnp.exp2(jnp.where(causal, seg, -jnp.inf))
                halves.append(_dot((cb * decay).astype(BF16), xp))
            y_pairs.append(jnp.where(low_half, halves[0], halves[1]))
        y_g = y_g + jnp.concatenate(y_pairs, axis=1)

        y_ref[:, sl] = _gated_norm(y_g, zs_ref[:, sl], gn_ref[:, sl]).astype(BF16)

    @pl.when(ci == pl.num_programs(0) - 1)
    def _():
        for i in range(D_INNER // LANES):
            hout_ref[i * LANES:(i + 1) * LANES, :] = ht_ref[:, i * LANES:(i + 1) * LANES].T


def _ssd_prompt(xc, rest, dt_raw, dtb, alog, dsk, gn):
    nchunk = SEQ // CHUNK
    full = lambda r, c: pl.BlockSpec((r, c), lambda i: (0, 0))
    return pl.pallas_call(
        _ssd_kernel,
        grid=(nchunk,),
        in_specs=[pl.BlockSpec((CHUNK, CONV_DIM), lambda i: (i, 0)),
                  pl.BlockSpec((CHUNK, D_INNER), lambda i: (i, 0)),
                  pl.BlockSpec((CHUNK, LANES), lambda i: (i, 0)),
                  full(1, LANES), full(1, LANES), full(1, D_INNER), full(1, D_INNER)],
        out_specs=[pl.BlockSpec((CHUNK, D_INNER), lambda i: (i, 0)),
                   pl.BlockSpec((D_INNER, D_STATE), lambda i: (0, 0))],
        out_shape=[jax.ShapeDtypeStruct((SEQ, D_INNER), BF16),
                   jax.ShapeDtypeStruct((D_INNER, D_STATE), F32)],
        scratch_shapes=[pltpu.VMEM((D_STATE, D_INNER), F32)],
        compiler_params=_cparams("arbitrary"),
        name="ssd_prompt",
    )(xc, rest, dt_raw, dtb, alog, dsk, gn)


def _attn_kernel(sink_ref, q_ref, kp_ref, kc_ref, vp_ref, vc_ref, o_ref):
    i = pl.program_id(0)
    band = 2 * CHUNK
    l_idx = lax.broadcasted_iota(jnp.int32, (CHUNK, band), 0)
    s_idx = lax.broadcasted_iota(jnp.int32, (CHUNK, band), 1)
    first_key = jnp.where(i > 0, 0, CHUNK)
    mask = (s_idx >= jnp.maximum(l_idx, first_key)) & (s_idx <= l_idx + WINDOW)
    low_q = lax.broadcasted_iota(jnp.int32, (CHUNK, LANES), 1) < HEADDIM
    low_kv = lax.broadcasted_iota(jnp.int32, (band, LANES), 1) < HEADDIM

    for c in range(KV_DIM // LANES):
        csl = slice(c * LANES, (c + 1) * LANES)
        k2 = jnp.concatenate([kp_ref[:, csl], kc_ref[:, csl]], axis=0)
        v2 = jnp.concatenate([vp_ref[:, csl], vc_ref[:, csl]], axis=0)
        k2r = pltpu.roll(k2, HEADDIM, axis=1)
        v2r = pltpu.roll(v2, HEADDIM, axis=1)
        for u in range(2):
            kv_head = 2 * c + u
            kd = (jnp.where(low_kv, k2, k2r) if u == 0 else jnp.where(low_kv, k2r, k2)).astype(BF16)
            vd = (jnp.where(low_kv, v2, v2r) if u == 0 else jnp.where(low_kv, v2r, v2)).astype(BF16)
            for jj in range(Q_PER_KV // 2):
                j = kv_head * (Q_PER_KV // 2) + jj
                q2 = q_ref[:, j * LANES:(j + 1) * LANES] * (HEADDIM ** -0.5)
                halves = []
                for hh in range(2):
                    qm = jnp.where(low_q if hh == 0 else jnp.logical_not(low_q), q2, 0.0)
                    s = jnp.where(mask, _dot_nt(qm.astype(BF16), kd), -jnp.inf)
                    sink = sink_ref[2 * j + hh]
                    m = jnp.maximum(jnp.max(s, axis=-1, keepdims=True), sink)
                    p = jnp.exp(s - m)
                    den = jnp.sum(p, axis=-1, keepdims=True) + jnp.exp(sink - m)
                    halves.append(_dot(p.astype(BF16), vd) / den)
                o_ref[:, j * LANES:(j + 1) * LANES] = jnp.where(low_q, halves[0], halves[1]).astype(BF16)


def _attn_prompt(proj, col0, sinks):
    nb = SEQ // CHUNK
    prev = lambda i: jnp.maximum(i - 1, 0)
    qb, kb, vb = (OFF_Q - col0) // D_MODEL, (OFF_K - col0) // KV_DIM, (OFF_V - col0) // KV_DIM
    return pl.pallas_call(
        _attn_kernel,
        grid=(nb,),
        in_specs=[pl.BlockSpec(memory_space=pltpu.SMEM),
                  pl.BlockSpec((CHUNK, D_MODEL), lambda i: (i, qb)),
                  pl.BlockSpec((CHUNK, KV_DIM), lambda i: (prev(i), kb)),
                  pl.BlockSpec((CHUNK, KV_DIM), lambda i: (i, kb)),
                  pl.BlockSpec((CHUNK, KV_DIM), lambda i: (prev(i), vb)),
                  pl.BlockSpec((CHUNK, KV_DIM), lambda i: (i, vb))],
        out_specs=pl.BlockSpec((CHUNK, D_MODEL), lambda i: (i, 0)),
        out_shape=jax.ShapeDtypeStruct((SEQ, D_MODEL), BF16),
        compiler_params=_cparams("parallel"),
        name="attn_prompt",
    )(sinks, proj, proj, proj, proj, proj)


def _alias_prev(prev, n_in, first_out):
    if prev is None:
        return [], {}, ()
    specs = [pl.BlockSpec(memory_space=pl.ANY)] * len(prev)
    return specs, {n_in + t: first_out + t for t in range(len(prev))}, tuple(prev)


def _conv_step_kernel(x_ref, dt_ref, cs_ref, cw_ref, cb_ref, dtb_ref, alog_ref, *rest):
    xc_ref, csn_ref, dto_ref, dao_ref = rest[-4:]
    taps = [cs_ref[k] for k in range(CONV_W - 1)] + [x_ref[...]]
    acc = cb_ref[...]
    for k in range(CONV_W):
        acc = acc + taps[k] * cw_ref[k:k + 1, :]
    xc_ref[...] = _silu(acc)
    for k in range(CONV_W - 1):
        csn_ref[k] = taps[k + 1]
    dt = _softplus(dt_ref[...] + dtb_ref[...])
    dto_ref[...] = dt
    dao_ref[...] = jnp.exp(dt * (-jnp.exp(alog_ref[...])))


def _conv_step(proj, conv_state, layer, cw, cb, dtb, alog, prev):
    nb = DEC_BATCH
    small = pl.BlockSpec((1, LANES), lambda s: (0, 0))
    head = pl.BlockSpec((nb, LANES), lambda s: (0, 0))
    state = pl.BlockSpec((None, CONV_W - 1, nb, GROUP_W), lambda s: (layer, 0, 0, s))
    in_specs = [pl.BlockSpec((nb, GROUP_W), lambda s: (0, OFF_X // GROUP_W + s)),
                pl.BlockSpec((nb, LANES), lambda s: (0, OFF_DT // LANES)),
                state,
                pl.BlockSpec((CONV_W, GROUP_W), lambda s: (0, s)),
                pl.BlockSpec((1, GROUP_W), lambda s: (0, s)),
                small, small]
    alias_specs, aliases, alias_args = _alias_prev(prev, len(in_specs), 1)
    return pl.pallas_call(
        _conv_step_kernel,
        grid=(CONV_DIM // GROUP_W,),
        in_specs=in_specs + alias_specs,
        out_specs=[pl.BlockSpec((nb, GROUP_W), lambda s: (0, s)), state, head, head],
        out_shape=[jax.ShapeDtypeStruct((nb, CONV_DIM), F32),
                   jax.ShapeDtypeStruct(conv_state.shape, F32),
                   jax.ShapeDtypeStruct((nb, LANES), F32),
                   jax.ShapeDtypeStruct((nb, LANES), F32)],
        input_output_aliases=aliases,
        compiler_params=_cparams("arbitrary"),
        name="conv_step",
    )(proj, proj, conv_state, cw, cb, dtb, alog, *alias_args)


STEP_B = 8
STEP_GROUPS = 4
STEP_W = STEP_GROUPS * GROUP_W


def _ssm_step_kernel(xs_ref, b_ref, c_ref, z_ref, dt_ref, da_ref, dsk_ref, gn_ref, h_ref, *rest):
    y_ref, hn_ref = rest[-2:]
    part = pl.program_id(1)
    xs = xs_ref[...]
    hrow = lax.broadcasted_iota(jnp.int32, (LANES, STEP_W), 0)
    hcol = lax.broadcasted_iota(jnp.int32, (LANES, STEP_W), 1)
    sel = jnp.where(hrow == part * (STEP_W // HEADDIM) + _div_pow2(hcol, HEADDIM), 1.0, 0.0).astype(BF16)
    dt_e = _dot3_r(dt_ref[...], sel)
    da_e = _dot3_r(da_ref[...], sel)
    xdt = xs * dt_e
    lane_group = _div_pow2(lax.broadcasted_iota(jnp.int32, (STEP_B, STEP_W), 1), GROUP_W)

    pieces = [jnp.where(lane_group == g, xdt, 0.0) for g in range(STEP_GROUPS)]
    pieces += [p.astype(F32) for p in _split3(da_e)]
    n_used = len(pieces) * STEP_B
    pieces.append(jnp.zeros((LANES - n_used, STEP_W), F32))
    ut = jnp.concatenate(pieces, axis=0)
    u_big = jnp.concatenate([ut[:, i * LANES:(i + 1) * LANES].T for i in range(STEP_W // LANES)],
                            axis=0).astype(BF16)

    b_rows = jnp.concatenate([b_ref[:, g * D_STATE:(g + 1) * D_STATE] for g in range(STEP_GROUPS)]
                             + [jnp.zeros((LANES - STEP_GROUPS * STEP_B, D_STATE), F32)], axis=0)
    rrow = lax.broadcasted_iota(jnp.int32, (LANES, D_STATE), 0)
    ones_rows = jnp.where((rrow >= STEP_GROUPS * STEP_B) & (rrow < n_used), 1.0, 0.0)
    big_r = jnp.concatenate([b_rows, ones_rows], axis=1)
    r_seq = _mod_pow2(lax.broadcasted_iota(jnp.int32, (LANES, 2 * D_STATE), 0), STEP_B)

    cc = jnp.concatenate([c_ref[:, g * D_STATE:(g + 1) * D_STATE] for g in range(STEP_GROUPS)],
                         axis=0).astype(BF16)
    crow = lax.broadcasted_iota(jnp.int32, (STEP_GROUPS * STEP_B, STEP_W), 0)
    ccol_group = _div_pow2(lax.broadcasted_iota(jnp.int32, (STEP_GROUPS * STEP_B, STEP_W), 1), GROUP_W)
    sub = lax.broadcasted_iota(jnp.int32, (STEP_B, STEP_W), 0)

    y_off = jnp.zeros((STEP_B, STEP_W), F32)
    for b in range(STEP_B):
        h0 = h_ref[b]
        res = _dot(u_big, jnp.where(r_seq == b, big_r, 0.0).astype(BF16))
        hn_ref[b] = res[:, D_STATE:] * h0 + res[:, :D_STATE]
        y_all = _dot_nt(cc, h0.astype(BF16))
        pick = (crow == ccol_group * STEP_B + b)
        y_b = jnp.sum(jnp.where(pick, y_all, 0.0), axis=0, keepdims=True)
        y_off = jnp.where(sub == b, y_b, y_off)

    cb_e = jnp.zeros((STEP_B, STEP_W), F32)
    for g in range(STEP_GROUPS):
        gsl = slice(g * D_STATE, (g + 1) * D_STATE)
        cb_g = jnp.sum(c_ref[:, gsl] * b_ref[:, gsl], axis=-1, keepdims=True)
        cb_e = jnp.where(lane_group == g, cb_g, cb_e)

    y = da_e * y_off + cb_e * xdt + dsk_ref[...] * xs
    for g in range(STEP_GROUPS):
        sl = slice(g * GROUP_W, (g + 1) * GROUP_W)
        y_ref[:, sl] = _gated_norm(y[:, sl], _silu(z_ref[:, sl]), gn_ref[:, sl])


def _ssm_step(xc, proj, dt, da, dsk, gn, state, layer, prev):
    nparts = D_INNER // STEP_W
    state_spec = pl.BlockSpec((None, STEP_B, STEP_W, D_STATE), lambda i, p: (layer, i, p, 0))
    in_specs = [pl.BlockSpec((STEP_B, STEP_W), lambda i, p: (i, p)),
                pl.BlockSpec((STEP_B, STEP_GROUPS * D_STATE),
                             lambda i, p: (i, D_INNER // (STEP_GROUPS * D_STATE) + p)),
                pl.BlockSpec((STEP_B, STEP_GROUPS * D_STATE),
                             lambda i, p: (i, (D_INNER + BC_W) // (STEP_GROUPS * D_STATE) + p)),
                pl.BlockSpec((STEP_B, STEP_W), lambda i, p: (i, OFF_Z // STEP_W + p)),
                pl.BlockSpec((STEP_B, LANES), lambda i, p: (i, 0)),
                pl.BlockSpec((STEP_B, LANES), lambda i, p: (i, 0)),
                pl.BlockSpec((1, STEP_W), lambda i, p: (0, p)),
                pl.BlockSpec((1, STEP_W), lambda i, p: (0, p)),
                state_spec]
    alias_specs, aliases, alias_args = _alias_prev(prev, len(in_specs), 1)
    return pl.pallas_call(
        _ssm_step_kernel,
        grid=(DEC_BATCH // STEP_B, nparts),
        in_specs=in_specs + alias_specs,
        out_specs=[pl.BlockSpec((STEP_B, STEP_W), lambda i, p: (i, p)), state_spec],
        out_shape=[jax.ShapeDtypeStruct((DEC_BATCH, D_INNER), F32),
                   jax.ShapeDtypeStruct(state.shape, F32)],
        input_output_aliases=aliases,
        compiler_params=_cparams("parallel", "parallel"),
        name="ssm_step",
    )(xc, xc, xc, proj, dt, da, dsk, gn, state, *alias_args)


ATTN_B = 8


def _attn_step_kernel(sink_ref, q_ref, kn_ref, vn_ref, kc_ref, vc_ref, *rest):
    o_ref, ko_ref, vo_ref = rest[-3:]
    lane_kv = _div_pow2(lax.broadcasted_iota(jnp.int32, (N_KV, KV_DIM), 1), HEADDIM)
    own = lane_kv == lax.broadcasted_iota(jnp.int32, (N_KV, KV_DIM), 0)
    newest = lax.broadcasted_iota(jnp.int32, (KV_DIM, WINDOW), 1) == WINDOW - 1
    sink = sink_ref[...]
    pad = jnp.zeros((LANES - ATTN_B, KV_DIM), F32)
    kn_t = jnp.concatenate([kn_ref[...], pad], axis=0).T
    vn_t = jnp.concatenate([vn_ref[...], pad], axis=0).T
    for b in range(ATTN_B):
        qrow = q_ref[b:b + 1, :] * (HEADDIM ** -0.5)
        qm = jnp.concatenate(
            [jnp.where(own, jnp.broadcast_to(qrow[:, r * KV_DIM:(r + 1) * KV_DIM], (N_KV, KV_DIM)), 0.0)
             for r in range(Q_PER_KV)], axis=0)
        k_new = kn_ref[b:b + 1, :]
        v_new = vn_ref[b:b + 1, :]
        kt = kc_ref[b]
        vt = vc_ref[b]
        s = _dot(qm.astype(BF16), kt.astype(BF16))
        s_new = jnp.sum(qm * k_new, axis=-1, keepdims=True)
        m = jnp.maximum(jnp.maximum(jnp.max(s, axis=-1, keepdims=True), s_new), sink)
        p = jnp.exp(s - m)
        p_new = jnp.exp(s_new - m)
        den = jnp.sum(p, axis=-1, keepdims=True) + p_new + jnp.exp(sink - m)
        o_full = _dot_nt((p / den).astype(BF16), vt.astype(BF16)) + (p_new / den) * v_new
        parts = []
        for r in range(Q_PER_KV):
            o_r = jnp.where(own, o_full[r * N_KV:(r + 1) * N_KV, :], 0.0)
            parts.append(jnp.sum(o_r, axis=0, keepdims=True))
        o_ref[b:b + 1, :] = jnp.concatenate(parts, axis=1)
        ko_ref[b] = jnp.where(newest, jnp.broadcast_to(kn_t[:, b:b + 1], (KV_DIM, WINDOW)),
                              pltpu.roll(kt, WINDOW - 1, axis=1))
        vo_ref[b] = jnp.where(newest, jnp.broadcast_to(vn_t[:, b:b + 1], (KV_DIM, WINDOW)),
                              pltpu.roll(vt, WINDOW - 1, axis=1))


def _attn_step(q, proj, sinks, cache_k, cache_v, layer, prev):
    cache_spec = pl.BlockSpec((None, ATTN_B, KV_DIM, WINDOW), lambda i: (layer, i, 0, 0))
    in_specs = [pl.BlockSpec((N_HEADS, 1), lambda i: (0, 0)),
                pl.BlockSpec((ATTN_B, D_MODEL), lambda i: (i, 0)),
                pl.BlockSpec((ATTN_B, KV_DIM), lambda i: (i, OFF_K // KV_DIM)),
                pl.BlockSpec((ATTN_B, KV_DIM), lambda i: (i, OFF_V // KV_DIM)),
                cache_spec, cache_spec]
    alias_specs, aliases, alias_args = _alias_prev(prev, len(in_specs), 1)
    return pl.pallas_call(
        _attn_step_kernel,
        grid=(DEC_BATCH // ATTN_B,),
        in_specs=in_specs + alias_specs,
        out_specs=[pl.BlockSpec((ATTN_B, D_MODEL), lambda i: (i, 0)), cache_spec, cache_spec],
        out_shape=[jax.ShapeDtypeStruct((DEC_BATCH, D_MODEL), F32),
                   jax.ShapeDtypeStruct(cache_k.shape, F32),
                   jax.ShapeDtypeStruct(cache_v.shape, F32)],
        input_output_aliases=aliases,
        compiler_params=_cparams("parallel"),
        name="attn_step",
    )(sinks, q, proj, proj, cache_k, cache_v, *alias_args)


def _cast_kernel(x_ref, o_ref):
    o_ref[...] = x_ref[...].astype(BF16)


def _cast_bf16(w, tk, name):
    d, k, n = w.shape
    spec = pl.BlockSpec((None, tk, n), lambda l, i: (l, i, 0))
    return pl.pallas_call(
        _cast_kernel,
        grid=(d, k // tk),
        in_specs=[spec],
        out_specs=spec,
        out_shape=jax.ShapeDtypeStruct(w.shape, BF16),
        compiler_params=_cparams("parallel", "parallel"),
        name=name,
    )(w)


SRC_X = D_INNER
SRC_DT = SRC_X + CONV_DIM
W_TILE = 512
N_W_TILES = PROJ_W // W_TILE


def _w_in_layout_kernel(x_ref, o_ref):
    t = pl.program_id(1)
    valid = jnp.where(t == N_W_TILES - 1, N_SSM_HEADS, W_TILE)
    row = lax.broadcasted_iota(jnp.int32, (W_TILE, D_MODEL), 0)
    o_ref[...] = jnp.where(row < valid, x_ref[0], 0.0).T.astype(BF16)


def _w_in_layout(w_t):
    d = w_t.shape[0]

    def src_row(l, t):
        per_tile = W_TILE // N_SSM_HEADS
        unit = jnp.where(
            t < OFF_Z // W_TILE, SRC_X // N_SSM_HEADS + t * per_tile,
            jnp.where(t < OFF_Q // W_TILE, (t - OFF_Z // W_TILE) * per_tile,
                      jnp.where(t < N_W_TILES - 1, t * per_tile + 1, SRC_DT // N_SSM_HEADS)))
        return (l, unit * N_SSM_HEADS, 0)

    return pl.pallas_call(
        _w_in_layout_kernel,
        grid=(d, N_W_TILES),
        in_specs=[pl.BlockSpec((pl.Element(1), pl.Element(W_TILE), pl.Element(D_MODEL)), src_row)],
        out_specs=pl.BlockSpec((None, D_MODEL, W_TILE), lambda l, t: (l, 0, t)),
        out_shape=jax.ShapeDtypeStruct((d, D_MODEL, PROJ_W), BF16),
        compiler_params=_cparams("parallel", "parallel"),
        name="w_in_layout",
    )(w_t)


def _small_params(l, norm_mix, conv_w, conv_b, dt_bias, a_log, d_skip, ssm_norm, attn_sinks, norm_ffn):
    pad = lambda v: jnp.pad(v, (0, LANES - N_SSM_HEADS)).reshape(1, LANES)
    return dict(
        norm_mix=norm_mix[l].reshape(1, D_MODEL),
        conv_w=conv_w[l], conv_b=conv_b[l].reshape(1, CONV_DIM),
        dt_bias=pad(dt_bias[l]), a_log=pad(a_log[l]),
        d_skip=jnp.repeat(d_skip[l], HEADDIM).reshape(1, D_INNER),
        ssm_norm=ssm_norm[l].reshape(1, D_INNER),
        sinks=attn_sinks[l],
        sinks_rm=attn_sinks[l].reshape(N_KV, Q_PER_KV).T.reshape(N_HEADS, 1),
        norm_ffn=norm_ffn[l].reshape(1, D_MODEL))


def _dense_tail(x, y, o, proj, col0, p, w, l, tm, norm_final):
    merged = _merge(y, o, proj, col0, w["w_ssm"], w["w_attn"], l, tm, 512)
    x1 = _matmul_res(merged, w["w_out"], l, x, min(tm, 512), D_MODEL, "out_proj")
    hmid = _ffn_up(x1, p["norm_ffn"], w["w_gate"], w["w_up"], l, tm, 512)
    return _ffn_down(hmid, w["w_down"], l, x1, norm_final, l == DEPTH - 1, min(tm, 512))


PROMPT_TM = 1024
PROJ_TN = 1280


def kernel(x_prompt, x_sample, state_ssm, state_conv, cache_k, cache_v, norm_mix, w_in, conv_w,
           conv_b, dt_bias, a_log, d_skip, ssm_norm, w_ssm_proj, w_attn_proj, attn_sinks, w_out,
           norm_ffn, w_gate, w_up, w_down, norm_final):
    xp = x_prompt.reshape(SEQ, D_MODEL)
    xs = x_sample.reshape(DEC_BATCH, D_MODEL)
    state4 = state_ssm.reshape(DEPTH, DEC_BATCH, D_INNER, D_STATE)
    to_kt = lambda c: c.transpose(0, 1, 3, 4, 2).reshape(DEPTH, DEC_BATCH, KV_DIM, WINDOW)
    ck, cv = to_kt(cache_k), to_kt(cache_v)
    conv_t = state_conv.transpose(0, 2, 1, 3)
    outs = {k: [] for k in ("p_ssm", "p_conv", "p_k", "p_v")}

    w = dict(w_in=_w_in_layout(w_in.transpose(0, 2, 1)),
             w_attn=_cast_bf16(w_attn_proj, 1024, "cast_w_attn"),
             w_ssm=_cast_bf16(w_ssm_proj, 1024, "cast_w_ssm"),
             w_out=_cast_bf16(w_out, 1024, "cast_w_out"),
             w_gate=_cast_bf16(w_gate, 256, "cast_w_gate"),
             w_up=_cast_bf16(w_up, 256, "cast_w_up"),
             w_down=_cast_bf16(w_down, 512, "cast_w_down"))

    g_final = norm_final.reshape(1, D_MODEL)
    h_all = conv_all = kv_all = None
    for l in range(DEPTH):
        p = _small_params(l, norm_mix, conv_w, conv_b, dt_bias, a_log, d_skip, ssm_norm,
                          attn_sinks, norm_ffn)

        xc_p, dt_p, tails = _in_proj_conv(xp, p["norm_mix"], w["w_in"], l, p["conv_w"], p["conv_b"],
                                          PROMPT_TM, 1024)
        rest = _in_proj_rest(xp, p["norm_mix"], w["w_in"], l, PROMPT_TM, 1024)
        y, h_p = _ssd_prompt(xc_p, rest, dt_p, p["dt_bias"], p["a_log"], p["d_skip"], p["ssm_norm"])
        o = _attn_prompt(rest, OFF_Z, p["sinks"])
        last_kv = lambda off: rest[SEQ - WINDOW:, off - OFF_Z:off - OFF_Z + KV_DIM].reshape(
            1, WINDOW, N_KV, HEADDIM)
        outs["p_ssm"].append(h_p.reshape(1, N_SSM_HEADS, HEADDIM, D_STATE))
        outs["p_conv"].append(tails[-1, SUBLANES - (CONV_W - 1):].reshape(1, CONV_W - 1, CONV_DIM))
        outs["p_k"].append(last_kv(OFF_K))
        outs["p_v"].append(last_kv(OFF_V))
        xp = _dense_tail(xp, y, o, rest, OFF_Z, p, w, l, PROMPT_TM, g_final)

        proj_s = _in_proj(xs, p["norm_mix"], w["w_in"], l, DEC_BATCH, PROJ_TN)
        xc, conv_all, dt, da = _conv_step(proj_s, conv_t, l, p["conv_w"], p["conv_b"],
                                          p["dt_bias"], p["a_log"],
                                          None if conv_all is None else (conv_all,))
        y_s, h_all = _ssm_step(xc, proj_s, dt, da, p["d_skip"], p["ssm_norm"], state4, l,
                               None if h_all is None else (h_all,))
        swap_rm = lambda t, a, b: t.reshape(DEC_BATCH, a, b, HEADDIM).transpose(0, 2, 1, 3).reshape(
            DEC_BATCH, D_MODEL)
        q_s = swap_rm(proj_s[:, OFF_Q:OFF_Q + D_MODEL], N_KV, Q_PER_KV)
        o_s, k_all, v_all = _attn_step(q_s, proj_s, p["sinks_rm"], ck, cv, l, kv_all)
        o_s = swap_rm(o_s, Q_PER_KV, N_KV)
        kv_all = (k_all, v_all)
        xs = _dense_tail(xs, y_s, o_s, proj_s, 0, p, w, l, DEC_BATCH, g_final)

    y_prompt = xp.reshape(1, SEQ, D_MODEL)
    y_sample = xs.reshape(DEC_BATCH, 1, D_MODEL)
    st = lambda k: jnp.stack(outs[k])
    from_kt = lambda c: c.reshape(DEPTH, DEC_BATCH, N_KV, HEADDIM, WINDOW).transpose(0, 1, 4, 2, 3)
    return (y_prompt, y_sample, st("p_ssm"), st("p_conv"), st("p_k"), st("p_v"),
            h_all.reshape(DEPTH, DEC_BATCH, N_SSM_HEADS, HEADDIM, D_STATE),
            conv_all.transpose(0, 2, 1, 3), from_kt(kv_all[0]), from_kt(kv_all[1]))
```

```python
import functools

import jax
import jax.numpy as jnp
from jax import lax
from jax.experimental import pallas as pl
from jax.experimental.pallas import tpu as pltpu

F32 = jnp.float32
BF16 = jnp.bfloat16

D_MODEL = 2048
SEQ = 8192
DEPTH = 2
DEC_BATCH = 128
D_INNER = 4096
HEADDIM = 64
N_SSM_HEADS = 64
D_STATE = 128
N_GROUPS = 8
GROUP_W = D_INNER // N_GROUPS
CONV_W = 4
BC_W = N_GROUPS * D_STATE
CONV_DIM = D_INNER + 2 * BC_W
CHUNK = 128
N_HEADS = 32
N_KV = 8
Q_PER_KV = 4
KV_DIM = N_KV * HEADDIM
WINDOW = 128
D_FF = 5632
EPS = 1e-6
IN_DIM = 17472

OFF_X = 0
OFF_B = OFF_X + D_INNER
OFF_C = OFF_B + BC_W
OFF_Z = CONV_DIM
OFF_Q = 10240
OFF_K = 12288
OFF_V = 12800
OFF_GS = 13312
OFF_GA = 15360
OFF_DT = 17408
PROJ_W = 17920

LANES = 128
SUBLANES = 8
VMEM_LIMIT = 56 * 1024 * 1024


def _cparams(*sem):
    return pltpu.CompilerParams(dimension_semantics=sem, vmem_limit_bytes=VMEM_LIMIT)


def _dot(a, b):
    return jnp.dot(a, b, preferred_element_type=F32)


def _dot_nt(a, b):
    return lax.dot_general(a, b, (((1,), (1,)), ((), ())), preferred_element_type=F32)


def _div_pow2(v, d):
    return jnp.right_shift(v, d.bit_length() - 1)


def _mod_pow2(v, d):
    return jnp.bitwise_and(v, d - 1)


def _split3(x):
    hi = x.astype(BF16)
    r1 = x - hi.astype(F32)
    mid = r1.astype(BF16)
    lo = (r1 - mid.astype(F32)).astype(BF16)
    return hi, mid, lo


def _dot3_r(x, sel):
    return _dot(jnp.concatenate(_split3(x), axis=1), jnp.concatenate([sel, sel, sel], axis=0))


def _dot3_l(sel, x):
    return _dot(jnp.concatenate([sel, sel, sel], axis=1), jnp.concatenate(_split3(x), axis=0))


def _silu(x):
    h = 0.5 * x
    return h + h * jnp.tanh(h)


def _softplus(x):
    return jnp.maximum(x, 0.0) + jnp.log1p(jnp.exp(-jnp.abs(x)))


def _rms_scale(x, g):
    r = lax.rsqrt(jnp.mean(x * x, axis=-1, keepdims=True) + EPS)
    return x * r * g


def _norm_matmul_kernel(x_ref, g_ref, w_ref, o_ref, xn_ref):
    @pl.when(pl.program_id(1) == 0)
    def _():
        xn_ref[...] = _rms_scale(x_ref[...], g_ref[...]).astype(BF16)

    o_ref[...] = _dot(xn_ref[...], w_ref[...])


def _in_proj(x, g, w, layer, tm, tn):
    m, k = x.shape
    n = w.shape[2]
    return pl.pallas_call(
        _norm_matmul_kernel,
        grid=(m // tm, n // tn),
        in_specs=[pl.BlockSpec((tm, k), lambda i, j: (i, 0)),
                  pl.BlockSpec((1, k), lambda i, j: (0, 0)),
                  pl.BlockSpec((None, k, tn), lambda i, j: (layer, 0, j))],
        out_specs=pl.BlockSpec((tm, tn), lambda i, j: (i, j)),
        out_shape=jax.ShapeDtypeStruct((m, n), F32),
        scratch_shapes=[pltpu.VMEM((tm, k), BF16)],
        compiler_params=_cparams("parallel", "arbitrary"),
        name="in_proj",
    )(x, g, w)


MXU_N = 256
REST_W = OFF_DT - OFF_Z
CONV_ROWS = 64


def _proj_conv_kernel(x_ref, g_ref, w_ref, wdt_ref, cw_ref, cb_ref, xc_ref, dt_ref, tail_ref,
                      xn_ref, xpad_ref, carry_ref):
    i = pl.program_id(0)
    j = pl.program_id(1)
    tm, tn = xc_ref.shape

    @pl.when(j == 0)
    def _():
        xn = _rms_scale(x_ref[...], g_ref[...]).astype(BF16)
        xn_ref[...] = xn
        dt_ref[...] = _dot(xn, wdt_ref[...])

    @pl.when(i == 0)
    def _():
        carry_ref[j] = jnp.zeros((SUBLANES, tn), F32)

    xpad_ref[0:SUBLANES, :] = carry_ref[j]
    for cs in range(tn // MXU_N):
        sl = slice(cs * MXU_N, (cs + 1) * MXU_N)
        xpad_ref[SUBLANES:SUBLANES + tm, sl] = _dot(xn_ref[...], w_ref[:, sl])
        for c0 in range(cs * MXU_N, (cs + 1) * MXU_N, LANES):
            cl = slice(c0, c0 + LANES)
            for r0 in range(0, tm, CONV_ROWS):
                xp = xpad_ref[r0:r0 + SUBLANES + CONV_ROWS, cl]
                acc = cb_ref[:, cl] + xp[SUBLANES:] * cw_ref[CONV_W - 1:CONV_W, cl]
                for back in range(1, CONV_W):
                    k = CONV_W - 1 - back
                    acc = acc + pltpu.roll(xp, back, axis=0)[SUBLANES:] * cw_ref[k:k + 1, cl]
                xc_ref[r0:r0 + CONV_ROWS, cl] = _silu(acc)
    tail = xpad_ref[tm:tm + SUBLANES, :]
    carry_ref[j] = tail
    tail_ref[...] = tail


def _in_proj_conv(x, g, w, layer, cw, cb, tm, tn):
    m, k = x.shape
    ntile = CONV_DIM // tn
    return pl.pallas_call(
        _proj_conv_kernel,
        grid=(m // tm, ntile),
        in_specs=[pl.BlockSpec((tm, k), lambda i, j: (i, 0)),
                  pl.BlockSpec((1, k), lambda i, j: (0, 0)),
                  pl.BlockSpec((None, k, tn), lambda i, j: (layer, 0, OFF_X // tn + j)),
                  pl.BlockSpec((None, k, LANES), lambda i, j: (layer, 0, OFF_DT // LANES)),
                  pl.BlockSpec((CONV_W, tn), lambda i, j: (0, j)),
                  pl.BlockSpec((1, tn), lambda i, j: (0, j))],
        out_specs=[pl.BlockSpec((tm, tn), lambda i, j: (i, j)),
                   pl.BlockSpec((tm, LANES), lambda i, j: (i, 0)),
                   pl.BlockSpec((None, SUBLANES, tn), lambda i, j: (i, 0, j))],
        out_shape=[jax.ShapeDtypeStruct((m, CONV_DIM), F32),
                   jax.ShapeDtypeStruct((m, LANES), F32),
                   jax.ShapeDtypeStruct((m // tm, SUBLANES, CONV_DIM), F32)],
        scratch_shapes=[pltpu.VMEM((tm, k), BF16),
                        pltpu.VMEM((tm + SUBLANES, tn), F32),
                        pltpu.VMEM((ntile, SUBLANES, tn), F32)],
        compiler_params=_cparams("arbitrary", "arbitrary"),
        name="in_proj_conv",
    )(x, g, w, w, cw, cb)


def _proj_rest_kernel(x_ref, g_ref, w_ref, o_ref, xn_ref, *, n_silu_tiles):
    j = pl.program_id(1)
    tn = o_ref.shape[1]

    @pl.when(j == 0)
    def _():
        xn_ref[...] = _rms_scale(x_ref[...], g_ref[...]).astype(BF16)

    @pl.when(j < n_silu_tiles)
    def _():
        for cs in range(tn // MXU_N):
            sl = slice(cs * MXU_N, (cs + 1) * MXU_N)
            o_ref[:, sl] = _silu(_dot(xn_ref[...], w_ref[:, sl]))

    @pl.when(j >= n_silu_tiles)
    def _():
        o_ref[...] = _dot(xn_ref[...], w_ref[...])


def _in_proj_rest(x, g, w, layer, tm, tn):
    m, k = x.shape
    return pl.pallas_call(
        functools.partial(_proj_rest_kernel, n_silu_tiles=D_INNER // tn),
        grid=(m // tm, REST_W // tn),
        in_specs=[pl.BlockSpec((tm, k), lambda i, j: (i, 0)),
                  pl.BlockSpec((1, k), lambda i, j: (0, 0)),
                  pl.BlockSpec((None, k, tn), lambda i, j: (layer, 0, OFF_Z // tn + j))],
        out_specs=pl.BlockSpec((tm, tn), lambda i, j: (i, j)),
        out_shape=jax.ShapeDtypeStruct((m, REST_W), F32),
        scratch_shapes=[pltpu.VMEM((tm, k), BF16)],
        compiler_params=_cparams("parallel", "arbitrary"),
        name="in_proj_rest",
    )(x, g, w)


def _ffn_up_kernel(x_ref, g_ref, wg_ref, wu_ref, o_ref, xn_ref):
    @pl.when(pl.program_id(1) == 0)
    def _():
        xn_ref[...] = _rms_scale(x_ref[...], g_ref[...]).astype(BF16)

    xn = xn_ref[...]
    o_ref[...] = (_silu(_dot(xn, wg_ref[...])) * _dot(xn, wu_ref[...])).astype(BF16)


def _ffn_up(x, g, wg, wu, layer, tm, tn):
    m, k = x.shape
    n = wg.shape[2]
    return pl.pallas_call(
        _ffn_up_kernel,
        grid=(m // tm, n // tn),
        in_specs=[pl.BlockSpec((tm, k), lambda i, j: (i, 0)),
                  pl.BlockSpec((1, k), lambda i, j: (0, 0)),
                  pl.BlockSpec((None, k, tn), lambda i, j: (layer, 0, j)),
                  pl.BlockSpec((None, k, tn), lambda i, j: (layer, 0, j))],
        out_specs=pl.BlockSpec((tm, tn), lambda i, j: (i, j)),
        out_shape=jax.ShapeDtypeStruct((m, n), BF16),
        scratch_shapes=[pltpu.VMEM((tm, k), BF16)],
        compiler_params=_cparams("parallel", "arbitrary"),
        name="ffn_up",
    )(x, g, wg, wu)


def _matmul_res_kernel(a_ref, w_ref, r_ref, o_ref):
    o_ref[...] = r_ref[...] + _dot(a_ref[...], w_ref[...])


def _matmul_res(a, w, layer, res, tm, tn, name):
    m, k = a.shape
    n = w.shape[2]
    return pl.pallas_call(
        _matmul_res_kernel,
        grid=(m // tm, n // tn),
        in_specs=[pl.BlockSpec((tm, k), lambda i, j: (i, 0)),
                  pl.BlockSpec((None, k, tn), lambda i, j: (layer, 0, j)),
                  pl.BlockSpec((tm, tn), lambda i, j: (i, j))],
        out_specs=pl.BlockSpec((tm, tn), lambda i, j: (i, j)),
        out_shape=jax.ShapeDtypeStruct((m, n), F32),
        compiler_params=_cparams("parallel", "parallel"),
        name=name,
    )(a, w, res)


def _ffn_down_kernel(a_ref, w_ref, r_ref, g_ref, o_ref, *, final_norm):
    y = r_ref[...] + _dot(a_ref[...], w_ref[...])
    o_ref[...] = _rms_scale(y, g_ref[...]) if final_norm else y


def _ffn_down(a, w, layer, res, gain, final_norm, tm):
    m, k = a.shape
    n = w.shape[2]
    return pl.pallas_call(
        functools.partial(_ffn_down_kernel, final_norm=final_norm),
        grid=(m // tm,),
        in_specs=[pl.BlockSpec((tm, k), lambda i: (i, 0)),
                  pl.BlockSpec((None, k, n), lambda i: (layer, 0, 0), pipeline_mode=pl.Buffered(1)),
                  pl.BlockSpec((tm, n), lambda i: (i, 0)),
                  pl.BlockSpec((1, n), lambda i: (0, 0))],
        out_specs=pl.BlockSpec((tm, n), lambda i: (i, 0)),
        out_shape=jax.ShapeDtypeStruct((m, n), F32),
        compiler_params=_cparams("parallel"),
        name="ffn_down",
    )(a, w, res, gain)


def _merge_kernel(y_ref, o_ref, gs_ref, ga_ref, ws_ref, wa_ref, out_ref):
    ys = _dot(y_ref[...].astype(BF16), ws_ref[...])
    oa = _dot(o_ref[...].astype(BF16), wa_ref[...])
    out_ref[...] = (jax.nn.sigmoid(gs_ref[...]) * ys
                    + jax.nn.sigmoid(ga_ref[...]) * oa).astype(BF16)


def _merge(y, o, proj, col0, ws, wa, layer, tm, tn):
    m = y.shape[0]
    gs0, ga0 = (OFF_GS - col0) // tn, (OFF_GA - col0) // tn
    return pl.pallas_call(
        _merge_kernel,
        grid=(m // tm, D_MODEL // tn),
        in_specs=[pl.BlockSpec((tm, D_INNER), lambda i, j: (i, 0)),
                  pl.BlockSpec((tm, D_MODEL), lambda i, j: (i, 0)),
                  pl.BlockSpec((tm, tn), lambda i, j: (i, gs0 + j)),
                  pl.BlockSpec((tm, tn), lambda i, j: (i, ga0 + j)),
                  pl.BlockSpec((None, D_INNER, tn), lambda i, j: (layer, 0, j)),
                  pl.BlockSpec((None, D_MODEL, tn), lambda i, j: (layer, 0, j))],
        out_specs=pl.BlockSpec((tm, tn), lambda i, j: (i, j)),
        out_shape=jax.ShapeDtypeStruct((m, D_MODEL), BF16),
        compiler_params=_cparams("parallel", "parallel"),
        name="merge",
    )(y, o, proj, proj, ws, wa)


def _gated_norm(y, gate, gain):
    h = y * gate
    r = lax.rsqrt(jnp.mean(h * h, axis=-1, keepdims=True) + EPS)
    return h * r * gain


LOG2E = 1.4426950408889634


def _hosted_casts(arrays, steps):
    specs = [pl.BlockSpec((a.shape[0] // steps, a.shape[1]), lambda i: (i, 0)) for a in arrays]
    shapes = [jax.ShapeDtypeStruct(a.shape, BF16) for a in arrays]
    return specs, shapes


def _run_hosted_casts(src_refs, dst_refs):
    for src, dst in zip(src_refs, dst_refs):
        dst[...] = src[...].astype(BF16)


def _ssd_kernel(*refs, n_cast):
    xc_ref, zs_ref, dt_ref, dtb_ref, alog_ref, dsk_ref, gn_ref = refs[:7]
    y_ref, hout_ref = refs[7 + n_cast:9 + n_cast]
    ht_ref = refs[-1]
    _run_hosted_casts(refs[7:7 + n_cast], refs[9 + n_cast:9 + 2 * n_cast])
    ci = pl.program_id(0)

    @pl.when(ci == 0)
    def _():
        ht_ref[...] = jnp.zeros_like(ht_ref)

    dt = _softplus(dt_ref[...] + dtb_ref[...])
    a = dt * (-jnp.exp(alog_ref[...])) * LOG2E
    row = lax.broadcasted_iota(jnp.int32, (CHUNK, CHUNK), 0)
    col = lax.broadcasted_iota(jnp.int32, (CHUNK, CHUNK), 1)
    causal = col <= row
    tri = jnp.where(causal, 1.0, 0.0).astype(BF16)
    a_cs = _dot3_l(tri, a)
    a_cs_t = a_cs.T
    ea = jnp.exp2(a_cs)
    to_end = jnp.exp2(a_cs[CHUNK - 1:CHUNK, :] - a_cs)
    stack = jnp.concatenate([dt, to_end, ea], axis=0)

    lane = lax.broadcasted_iota(jnp.int32, (CHUNK, LANES), 1)
    low_half = lane < HEADDIM
    hrow = lax.broadcasted_iota(jnp.int32, (LANES, GROUP_W), 0)
    hcol = lax.broadcasted_iota(jnp.int32, (LANES, GROUP_W), 1)

    for g in range(N_GROUPS):
        sl = slice(g * GROUP_W, (g + 1) * GROUP_W)
        sel = jnp.where(hrow == g * (GROUP_W // HEADDIM) + _div_pow2(hcol, HEADDIM), 1.0, 0.0).astype(BF16)
        ex = _dot3_r(stack, sel)
        dt_e, te_e, ea_e = ex[0:CHUNK], ex[CHUNK:2 * CHUNK], ex[2 * CHUNK:3 * CHUNK]

        x_g = xc_ref[:, sl]
        b_g = xc_ref[:, D_INNER + g * D_STATE:D_INNER + (g + 1) * D_STATE]
        c_g = xc_ref[:, D_INNER + BC_W + g * D_STATE:D_INNER + BC_W + (g + 1) * D_STATE]
        xdt = x_g * dt_e
        xdt_b = xdt.astype(BF16)
        cb16, bb16 = c_g.astype(BF16), b_g.astype(BF16)
        cb = _dot_nt(cb16, bb16)

        ht_g = ht_ref[:, sl]
        y_g = _dot(cb16, ht_g.astype(BF16)) * ea_e + dsk_ref[:, sl] * x_g
        states_t = _dot(b_g.T.astype(BF16), (xdt * te_e).astype(BF16))
        ht_ref[:, sl] = ea_e[CHUNK - 1:CHUNK, :] * ht_g + states_t

        y_pairs = []
        for jj in range(GROUP_W // LANES):
            xp = xdt_b[:, jj * LANES:(jj + 1) * LANES]
            halves = []
            for hh in range(2):
                h = g * (GROUP_W // HEADDIM) + 2 * jj + hh
                seg = a_cs[:, h:h + 1] - a_cs_t[h:h + 1, :]
                decay = jnp.exp2(jnp.where(causal, seg, -jnp.inf))
                halves.append(_dot((cb * decay).astype(BF16), xp))
            y_pairs.append(jnp.where(low_half, halves[0], halves[1]))
        y_g = y_g + jnp.concatenate(y_pairs, axis=1)

        y_ref[:, sl] = _gated_norm(y_g, zs_ref[:, sl], gn_ref[:, sl]).astype(BF16)

    @pl.when(ci == pl.num_programs(0) - 1)
    def _():
        for i in range(D_INNER // LANES):
            hout_ref[i * LANES:(i + 1) * LANES, :] = ht_ref[:, i * LANES:(i + 1) * LANES].T


def _ssd_prompt(xc, rest, dt_raw, dtb, alog, dsk, gn, casts=()):
    nchunk = SEQ // CHUNK
    full = lambda r, c: pl.BlockSpec((r, c), lambda i: (0, 0))
    cast_specs, cast_shapes = _hosted_casts(casts, nchunk)
    return pl.pallas_call(
        functools.partial(_ssd_kernel, n_cast=len(casts)),
        grid=(nchunk,),
        in_specs=[pl.BlockSpec((CHUNK, CONV_DIM), lambda i: (i, 0)),
                  pl.BlockSpec((CHUNK, D_INNER), lambda i: (i, 0)),
                  pl.BlockSpec((CHUNK, LANES), lambda i: (i, 0)),
                  full(1, LANES), full(1, LANES), full(1, D_INNER), full(1, D_INNER)] + cast_specs,
        out_specs=[pl.BlockSpec((CHUNK, D_INNER), lambda i: (i, 0)),
                   pl.BlockSpec((D_INNER, D_STATE), lambda i: (0, 0))] + cast_specs,
        out_shape=[jax.ShapeDtypeStruct((SEQ, D_INNER), BF16),
                   jax.ShapeDtypeStruct((D_INNER, D_STATE), F32)] + cast_shapes,
        scratch_shapes=[pltpu.VMEM((D_STATE, D_INNER), F32)],
        compiler_params=_cparams("arbitrary"),
        name="ssd_prompt",
    )(xc, rest, dt_raw, dtb, alog, dsk, gn, *casts)


def _attn_kernel(*refs, n_cast):
    sink_ref, q_ref, kp_ref, kc_ref, vp_ref, vc_ref = refs[:6]
    o_ref = refs[6 + n_cast]
    _run_hosted_casts(refs[6:6 + n_cast], refs[7 + n_cast:7 + 2 * n_cast])
    i = pl.program_id(0)
    band = 2 * CHUNK
    l_idx = lax.broadcasted_iota(jnp.int32, (CHUNK, band), 0)
    s_idx = lax.broadcasted_iota(jnp.int32, (CHUNK, band), 1)
    first_key = jnp.where(i > 0, 0, CHUNK)
    mask = (s_idx >= jnp.maximum(l_idx, first_key)) & (s_idx <= l_idx + WINDOW)
    low_q = lax.broadcasted_iota(jnp.int32, (CHUNK, LANES), 1) < HEADDIM
    low_kv = lax.broadcasted_iota(jnp.int32, (band, LANES), 1) < HEADDIM

    for c in range(KV_DIM // LANES):
        csl = slice(c * LANES, (c + 1) * LANES)
        k2 = jnp.concatenate([kp_ref[:, csl], kc_ref[:, csl]], axis=0)
        v2 = jnp.concatenate([vp_ref[:, csl], vc_ref[:, csl]], axis=0)
        k2r = pltpu.roll(k2, HEADDIM, axis=1)
        v2r = pltpu.roll(v2, HEADDIM, axis=1)
        for u in range(2):
            kv_head = 2 * c + u
            kd = (jnp.where(low_kv, k2, k2r) if u == 0 else jnp.where(low_kv, k2r, k2)).astype(BF16)
            vd = (jnp.where(low_kv, v2, v2r) if u == 0 else jnp.where(low_kv, v2r, v2)).astype(BF16)
            for jj in range(Q_PER_KV // 2):
                j = kv_head * (Q_PER_KV // 2) + jj
                q2 = q_ref[:, j * LANES:(j + 1) * LANES] * (HEADDIM ** -0.5)
                halves = []
                for hh in range(2):
                    qm = jnp.where(low_q if hh == 0 else jnp.logical_not(low_q), q2, 0.0)
                    s = jnp.where(mask, _dot_nt(qm.astype(BF16), kd), -jnp.inf)
                    sink = sink_ref[2 * j + hh]
                    m = jnp.maximum(jnp.max(s, axis=-1, keepdims=True), sink)
                    p = jnp.exp(s - m)
                    den = jnp.sum(p, axis=-1, keepdims=True) + jnp.exp(sink - m)
                    halves.append(_dot(p.astype(BF16), vd) / den)
                o_ref[:, j * LANES:(j + 1) * LANES] = jnp.where(low_q, halves[0], halves[1]).astype(BF16)


def _attn_prompt(proj, col0, sinks, casts=()):
    nb = SEQ // CHUNK
    prev = lambda i: jnp.maximum(i - 1, 0)
    qb, kb, vb = (OFF_Q - col0) // D_MODEL, (OFF_K - col0) // KV_DIM, (OFF_V - col0) // KV_DIM
    cast_specs, cast_shapes = _hosted_casts(casts, nb)
    return pl.pallas_call(
        functools.partial(_attn_kernel, n_cast=len(casts)),
        grid=(nb,),
        in_specs=[pl.BlockSpec(memory_space=pltpu.SMEM),
                  pl.BlockSpec((CHUNK, D_MODEL), lambda i: (i, qb)),
                  pl.BlockSpec((CHUNK, KV_DIM), lambda i: (prev(i), kb)),
                  pl.BlockSpec((CHUNK, KV_DIM), lambda i: (i, kb)),
                  pl.BlockSpec((CHUNK, KV_DIM), lambda i: (prev(i), vb)),
                  pl.BlockSpec((CHUNK, KV_DIM), lambda i: (i, vb))] + cast_specs,
        out_specs=[pl.BlockSpec((CHUNK, D_MODEL), lambda i: (i, 0))] + cast_specs,
        out_shape=[jax.ShapeDtypeStruct((SEQ, D_MODEL), BF16)] + cast_shapes,
        compiler_params=_cparams("parallel"),
        name="attn_prompt",
    )(sinks, proj, proj, proj, proj, proj, *casts)


def _alias_prev(prev, n_in, first_out):
    if prev is None:
        return [], {}, ()
    specs = [pl.BlockSpec(memory_space=pl.ANY)] * len(prev)
    return specs, {n_in + t: first_out + t for t in range(len(prev))}, tuple(prev)


def _conv_step_kernel(x_ref, dt_ref, cs_ref, cw_ref, cb_ref, dtb_ref, alog_ref, *rest):
    xc_ref, csn_ref, dto_ref, dao_ref = rest[-4:]
    taps = [cs_ref[k] for k in range(CONV_W - 1)] + [x_ref[...]]
    acc = cb_ref[...]
    for k in range(CONV_W):
        acc = acc + taps[k] * cw_ref[k:k + 1, :]
    xc_ref[...] = _silu(acc)
    for k in range(CONV_W - 1):
        csn_ref[k] = taps[k + 1]
    dt = _softplus(dt_ref[...] + dtb_ref[...])
    dto_ref[...] = dt
    dao_ref[...] = jnp.exp(dt * (-jnp.exp(alog_ref[...])))


def _conv_step(proj, conv_state, layer, cw, cb, dtb, alog, prev):
    nb = DEC_BATCH
    small = pl.BlockSpec((1, LANES), lambda s: (0, 0))
    head = pl.BlockSpec((nb, LANES), lambda s: (0, 0))
    state = pl.BlockSpec((None, CONV_W - 1, nb, GROUP_W), lambda s: (layer, 0, 0, s))
    in_specs = [pl.BlockSpec((nb, GROUP_W), lambda s: (0, OFF_X // GROUP_W + s)),
                pl.BlockSpec((nb, LANES), lambda s: (0, OFF_DT // LANES)),
                state,
                pl.BlockSpec((CONV_W, GROUP_W), lambda s: (0, s)),
                pl.BlockSpec((1, GROUP_W), lambda s: (0, s)),
                small, small]
    alias_specs, aliases, alias_args = _alias_prev(prev, len(in_specs), 1)
    return pl.pallas_call(
        _conv_step_kernel,
        grid=(CONV_DIM // GROUP_W,),
        in_specs=in_specs + alias_specs,
        out_specs=[pl.BlockSpec((nb, GROUP_W), lambda s: (0, s)), state, head, head],
        out_shape=[jax.ShapeDtypeStruct((nb, CONV_DIM), F32),
                   jax.ShapeDtypeStruct(conv_state.shape, F32),
                   jax.ShapeDtypeStruct((nb, LANES), F32),
                   jax.ShapeDtypeStruct((nb, LANES), F32)],
        input_output_aliases=aliases,
        compiler_params=_cparams("arbitrary"),
        name="conv_step",
    )(proj, proj, conv_state, cw, cb, dtb, alog, *alias_args)


STEP_B = 8
STEP_GROUPS = 4
STEP_W = STEP_GROUPS * GROUP_W


def _ssm_step_kernel(xs_ref, b_ref, c_ref, z_ref, dt_ref, da_ref, dsk_ref, gn_ref, h_ref, *rest):
    y_ref, hn_ref = rest[-2:]
    part = pl.program_id(1)
    xs = xs_ref[...]
    hrow = lax.broadcasted_iota(jnp.int32, (LANES, STEP_W), 0)
    hcol = lax.broadcasted_iota(jnp.int32, (LANES, STEP_W), 1)
    sel = jnp.where(hrow == part * (STEP_W // HEADDIM) + _div_pow2(hcol, HEADDIM), 1.0, 0.0).astype(BF16)
    dt_e = _dot3_r(dt_ref[...], sel)
    da_e = _dot3_r(da_ref[...], sel)
    xdt = xs * dt_e
    lane_group = _div_pow2(lax.broadcasted_iota(jnp.int32, (STEP_B, STEP_W), 1), GROUP_W)

    pieces = [jnp.where(lane_group == g, xdt, 0.0) for g in range(STEP_GROUPS)]
    pieces += [p.astype(F32) for p in _split3(da_e)]
    n_used = len(pieces) * STEP_B
    pieces.append(jnp.zeros((LANES - n_used, STEP_W), F32))
    ut = jnp.concatenate(pieces, axis=0)
    u_big = jnp.concatenate([ut[:, i * LANES:(i + 1) * LANES].T for i in range(STEP_W // LANES)],
                            axis=0).astype(BF16)

    b_rows = jnp.concatenate([b_ref[:, g * D_STATE:(g + 1) * D_STATE] for g in range(STEP_GROUPS)]
                             + [jnp.zeros((LANES - STEP_GROUPS * STEP_B, D_STATE), F32)], axis=0)
    rrow = lax.broadcasted_iota(jnp.int32, (LANES, D_STATE), 0)
    ones_rows = jnp.where((rrow >= STEP_GROUPS * STEP_B) & (rrow < n_used), 1.0, 0.0)
    big_r = jnp.concatenate([b_rows, ones_rows], axis=1)
    r_seq = _mod_pow2(lax.broadcasted_iota(jnp.int32, (LANES, 2 * D_STATE), 0), STEP_B)

    cc = jnp.concatenate([c_ref[:, g * D_STATE:(g + 1) * D_STATE] for g in range(STEP_GROUPS)],
                         axis=0).astype(BF16)
    crow = lax.broadcasted_iota(jnp.int32, (STEP_GROUPS * STEP_B, STEP_W), 0)
    ccol_group = _div_pow2(lax.broadcasted_iota(jnp.int32, (STEP_GROUPS * STEP_B, STEP_W), 1), GROUP_W)
    sub = lax.broadcasted_iota(jnp.int32, (STEP_B, STEP_W), 0)

    y_off = jnp.zeros((STEP_B, STEP_W), F32)
    for b in range(STEP_B):
        h0 = h_ref[b]
        res = _dot(u_big, jnp.where(r_seq == b, big_r, 0.0).astype(BF16))
        hn_ref[b] = res[:, D_STATE:] * h0 + res[:, :D_STATE]
        y_all = _dot_nt(cc, h0.astype(BF16))
        pick = (crow == ccol_group * STEP_B + b)
        y_b = jnp.sum(jnp.where(pick, y_all, 0.0), axis=0, keepdims=True)
        y_off = jnp.where(sub == b, y_b, y_off)

    cb_e = jnp.zeros((STEP_B, STEP_W), F32)
    for g in range(STEP_GROUPS):
        gsl = slice(g * D_STATE, (g + 1) * D_STATE)
        cb_g = jnp.sum(c_ref[:, gsl] * b_ref[:, gsl], axis=-1, keepdims=True)
        cb_e = jnp.where(lane_group == g, cb_g, cb_e)

    y = da_e * y_off + cb_e * xdt + dsk_ref[...] * xs
    for g in range(STEP_GROUPS):
        sl = slice(g * GROUP_W, (g + 1) * GROUP_W)
        y_ref[:, sl] = _gated_norm(y[:, sl], _silu(z_ref[:, sl]), gn_ref[:, sl])


def _ssm_step(xc, proj, dt, da, dsk, gn, state, layer, prev):
    nparts = D_INNER // STEP_W
    state_spec = pl.BlockSpec((None, STEP_B, STEP_W, D_STATE), lambda i, p: (layer, i, p, 0))
    in_specs = [pl.BlockSpec((STEP_B, STEP_W), lambda i, p: (i, p)),
                pl.BlockSpec((STEP_B, STEP_GROUPS * D_STATE),
                             lambda i, p: (i, D_INNER // (STEP_GROUPS * D_STATE) + p)),
                pl.BlockSpec((STEP_B, STEP_GROUPS * D_STATE),
                             lambda i, p: (i, (D_INNER + BC_W) // (STEP_GROUPS * D_STATE) + p)),
                pl.BlockSpec((STEP_B, STEP_W), lambda i, p: (i, OFF_Z // STEP_W + p)),
                pl.BlockSpec((STEP_B, LANES), lambda i, p: (i, 0)),
                pl.BlockSpec((STEP_B, LANES), lambda i, p: (i, 0)),
                pl.BlockSpec((1, STEP_W), lambda i, p: (0, p)),
                pl.BlockSpec((1, STEP_W), lambda i, p: (0, p)),
                state_spec]
    alias_specs, aliases, alias_args = _alias_prev(prev, len(in_specs), 1)
    return pl.pallas_call(
        _ssm_step_kernel,
        grid=(DEC_BATCH // STEP_B, nparts),
        in_specs=in_specs + alias_specs,
        out_specs=[pl.BlockSpec((STEP_B, STEP_W), lambda i, p: (i, p)), state_spec],
        out_shape=[jax.ShapeDtypeStruct((DEC_BATCH, D_INNER), F32),
                   jax.ShapeDtypeStruct(state.shape, F32)],
        input_output_aliases=aliases,
        compiler_params=_cparams("parallel", "parallel"),
        name="ssm_step",
    )(xc, xc, xc, proj, dt, da, dsk, gn, state, *alias_args)


ATTN_B = 8


def _attn_step_kernel(sink_ref, q_ref, kn_ref, vn_ref, kc_ref, vc_ref, *rest):
    o_ref, ko_ref, vo_ref = rest[-3:]
    lane_kv = _div_pow2(lax.broadcasted_iota(jnp.int32, (N_KV, KV_DIM), 1), HEADDIM)
    own = lane_kv == lax.broadcasted_iota(jnp.int32, (N_KV, KV_DIM), 0)
    newest = lax.broadcasted_iota(jnp.int32, (KV_DIM, WINDOW), 1) == WINDOW - 1
    sink = sink_ref[...]
    pad = jnp.zeros((LANES - ATTN_B, KV_DIM), F32)
    kn_t = jnp.concatenate([kn_ref[...], pad], axis=0).T
    vn_t = jnp.concatenate([vn_ref[...], pad], axis=0).T
    for b in range(ATTN_B):
        qrow = q_ref[b:b + 1, :] * (HEADDIM ** -0.5)
        qm = jnp.concatenate(
            [jnp.where(own, jnp.broadcast_to(qrow[:, r * KV_DIM:(r + 1) * KV_DIM], (N_KV, KV_DIM)), 0.0)
             for r in range(Q_PER_KV)], axis=0)
        k_new = kn_ref[b:b + 1, :]
        v_new = vn_ref[b:b + 1, :]
        kt = kc_ref[b]
        vt = vc_ref[b]
        s = _dot(qm.astype(BF16), kt.astype(BF16))
        s_new = jnp.sum(qm * k_new, axis=-1, keepdims=True)
        m = jnp.maximum(jnp.maximum(jnp.max(s, axis=-1, keepdims=True), s_new), sink)
        p = jnp.exp(s - m)
        p_new = jnp.exp(s_new - m)
        den = jnp.sum(p, axis=-1, keepdims=True) + p_new + jnp.exp(sink - m)
        o_full = _dot_nt((p / den).astype(BF16), vt.astype(BF16)) + (p_new / den) * v_new
        parts = []
        for r in range(Q_PER_KV):
            o_r = jnp.where(own, o_full[r * N_KV:(r + 1) * N_KV, :], 0.0)
            parts.append(jnp.sum(o_r, axis=0, keepdims=True))
        o_ref[b:b + 1, :] = jnp.concatenate(parts, axis=1)
        ko_ref[b] = jnp.where(newest, jnp.broadcast_to(kn_t[:, b:b + 1], (KV_DIM, WINDOW)),
                              pltpu.roll(kt, WINDOW - 1, axis=1))
        vo_ref[b] = jnp.where(newest, jnp.broadcast_to(vn_t[:, b:b + 1], (KV_DIM, WINDOW)),
                              pltpu.roll(vt, WINDOW - 1, axis=1))


def _attn_step(q, proj, sinks, cache_k, cache_v, layer, prev):
    cache_spec = pl.BlockSpec((None, ATTN_B, KV_DIM, WINDOW), lambda i: (layer, i, 0, 0))
    in_specs = [pl.BlockSpec((N_HEADS, 1), lambda i: (0, 0)),
                pl.BlockSpec((ATTN_B, D_MODEL), lambda i: (i, 0)),
                pl.BlockSpec((ATTN_B, KV_DIM), lambda i: (i, OFF_K // KV_DIM)),
                pl.BlockSpec((ATTN_B, KV_DIM), lambda i: (i, OFF_V // KV_DIM)),
                cache_spec, cache_spec]
    alias_specs, aliases, alias_args = _alias_prev(prev, len(in_specs), 1)
    return pl.pallas_call(
        _attn_step_kernel,
        grid=(DEC_BATCH // ATTN_B,),
        in_specs=in_specs + alias_specs,
        out_specs=[pl.BlockSpec((ATTN_B, D_MODEL), lambda i: (i, 0)), cache_spec, cache_spec],
        out_shape=[jax.ShapeDtypeStruct((DEC_BATCH, D_MODEL), F32),
                   jax.ShapeDtypeStruct(cache_k.shape, F32),
                   jax.ShapeDtypeStruct(cache_v.shape, F32)],
        input_output_aliases=aliases,
        compiler_params=_cparams("parallel"),
        name="attn_step",
    )(sinks, q, proj, proj, cache_k, cache_v, *alias_args)


SRC_X = D_INNER
SRC_DT = SRC_X + CONV_DIM
W_TILE = 512
N_W_TILES = PROJ_W // W_TILE


def _w_in_layout_kernel(x_ref, o_ref):
    t = pl.program_id(1)
    valid = jnp.where(t == N_W_TILES - 1, N_SSM_HEADS, W_TILE)
    row = lax.broadcasted_iota(jnp.int32, (W_TILE, D_MODEL), 0)
    o_ref[...] = jnp.where(row < valid, x_ref[0], 0.0).T.astype(BF16)


def _w_in_layout(w_t):
    d = w_t.shape[0]

    def src_row(l, t):
        per_tile = W_TILE // N_SSM_HEADS
        unit = jnp.where(
            t < OFF_Z // W_TILE, SRC_X // N_SSM_HEADS + t * per_tile,
            jnp.where(t < OFF_Q // W_TILE, (t - OFF_Z // W_TILE) * per_tile,
                      jnp.where(t < N_W_TILES - 1, t * per_tile + 1, SRC_DT // N_SSM_HEADS)))
        return (l, unit * N_SSM_HEADS, 0)

    return pl.pallas_call(
        _w_in_layout_kernel,
        grid=(d, N_W_TILES),
        in_specs=[pl.BlockSpec((pl.Element(1), pl.Element(W_TILE), pl.Element(D_MODEL)), src_row)],
        out_specs=pl.BlockSpec((None, D_MODEL, W_TILE), lambda l, t: (l, 0, t)),
        out_shape=jax.ShapeDtypeStruct((d, D_MODEL, PROJ_W), BF16),
        compiler_params=_cparams("parallel", "parallel"),
        name="w_in_layout",
    )(w_t)


def _small_params(l, norm_mix, conv_w, conv_b, dt_bias, a_log, d_skip, ssm_norm, attn_sinks, norm_ffn):
    pad = lambda v: jnp.pad(v, (0, LANES - N_SSM_HEADS)).reshape(1, LANES)
    return dict(
        norm_mix=norm_mix[l].reshape(1, D_MODEL),
        conv_w=conv_w[l], conv_b=conv_b[l].reshape(1, CONV_DIM),
        dt_bias=pad(dt_bias[l]), a_log=pad(a_log[l]),
        d_skip=jnp.repeat(d_skip[l], HEADDIM).reshape(1, D_INNER),
        ssm_norm=ssm_norm[l].reshape(1, D_INNER),
        sinks=attn_sinks[l],
        sinks_rm=attn_sinks[l].reshape(N_KV, Q_PER_KV).T.reshape(N_HEADS, 1),
        norm_ffn=norm_ffn[l].reshape(1, D_MODEL))


def _dense_tail(x, y, o, proj, col0, p, w, l, tm, norm_final):
    merged = _merge(y, o, proj, col0, w["w_ssm"], w["w_attn"], l, tm, 512)
    x1 = _matmul_res(merged, w["w_out"], l, x, min(tm, 512), D_MODEL, "out_proj")
    hmid = _ffn_up(x1, p["norm_ffn"], w["w_gate"], w["w_up"], l, tm, 512)
    return _ffn_down(hmid, w["w_down"], l, x1, norm_final, l == DEPTH - 1, min(tm, 512))


PROMPT_TM = 1024
PROJ_TN = 1280


def kernel(x_prompt, x_sample, state_ssm, state_conv, cache_k, cache_v, norm_mix, w_in, conv_w,
           conv_b, dt_bias, a_log, d_skip, ssm_norm, w_ssm_proj, w_attn_proj, attn_sinks, w_out,
           norm_ffn, w_gate, w_up, w_down, norm_final):
    xp = x_prompt.reshape(SEQ, D_MODEL)
    xs = x_sample.reshape(DEC_BATCH, D_MODEL)
    state4 = state_ssm.reshape(DEPTH, DEC_BATCH, D_INNER, D_STATE)
    to_kt = lambda c: c.transpose(0, 1, 3, 4, 2).reshape(DEPTH, DEC_BATCH, KV_DIM, WINDOW)
    ck, cv = to_kt(cache_k), to_kt(cache_v)
    conv_t = state_conv.transpose(0, 2, 1, 3)
    outs = {k: [] for k in ("p_ssm", "p_conv", "p_k", "p_v")}

    w = dict(w_in=_w_in_layout(w_in.transpose(0, 2, 1)))
    rows = lambda t: t.reshape(-1, t.shape[-1])
    ssd_casts = dict(w_ssm=w_ssm_proj, w_attn=w_attn_proj, w_out=w_out, w_down=w_down)
    attn_casts = dict(w_gate=w_gate, w_up=w_up)

    g_final = norm_final.reshape(1, D_MODEL)
    h_all = conv_all = kv_all = None
    for l in range(DEPTH):
        p = _small_params(l, norm_mix, conv_w, conv_b, dt_bias, a_log, d_skip, ssm_norm,
                          attn_sinks, norm_ffn)

        xc_p, dt_p, tails = _in_proj_conv(xp, p["norm_mix"], w["w_in"], l, p["conv_w"], p["conv_b"],
                                          PROMPT_TM, 1024)
        rest = _in_proj_rest(xp, p["norm_mix"], w["w_in"], l, PROMPT_TM, 1024)
        hosted = [ssd_casts, attn_casts] if l == 0 else [{}, {}]
        y, h_p, *cast_a = _ssd_prompt(xc_p, rest, dt_p, p["dt_bias"], p["a_log"], p["d_skip"],
                                      p["ssm_norm"], [rows(t) for t in hosted[0].values()])
        o, *cast_b = _attn_prompt(rest, OFF_Z, p["sinks"], [rows(t) for t in hosted[1].values()])
        for group, done in zip(hosted, (cast_a, cast_b)):
            w.update({k: c.reshape(t.shape) for (k, t), c in zip(group.items(), done)})
        last_kv = lambda off: rest[SEQ - WINDOW:, off - OFF_Z:off - OFF_Z + KV_DIM].reshape(
            1, WINDOW, N_KV, HEADDIM)
        outs["p_ssm"].append(h_p.reshape(1, N_SSM_HEADS, HEADDIM, D_STATE))
        outs["p_conv"].append(tails[-1, SUBLANES - (CONV_W - 1):].reshape(1, CONV_W - 1, CONV_DIM))
        outs["p_k"].append(last_kv(OFF_K))
        outs["p_v"].append(last_kv(OFF_V))
        xp = _dense_tail(xp, y, o, rest, OFF_Z, p, w, l, PROMPT_TM, g_final)

        proj_s = _in_proj(xs, p["norm_mix"], w["w_in"], l, DEC_BATCH, PROJ_TN)
        xc, conv_all, dt, da = _conv_step(proj_s, conv_t, l, p["conv_w"], p["conv_b"],
                                          p["dt_bias"], p["a_log"],
                                          None if conv_all is None else (conv_all,))
        y_s, h_all = _ssm_step(xc, proj_s, dt, da, p["d_skip"], p["ssm_norm"], state4, l,
                               None if h_all is None else (h_all,))
        swap_rm = lambda t, a, b: t.reshape(DEC_BATCH, a, b, HEADDIM).transpose(0, 2, 1, 3).reshape(
            DEC_BATCH, D_MODEL)
        q_s = swap_rm(proj_s[:, OFF_Q:OFF_Q + D_MODEL], N_KV, Q_PER_KV)
        o_s, k_all, v_all = _attn_step(q_s, proj_s, p["sinks_rm"], ck, cv, l, kv_all)
        o_s = swap_rm(o_s, Q_PER_KV, N_KV)
        kv_all = (k_all, v_all)
        xs = _dense_tail(xs, y_s, o_s, proj_s, 0, p, w, l, DEC_BATCH, g_final)

    y_prompt = xp.reshape(1, SEQ, D_MODEL)
    y_sample = xs.reshape(DEC_BATCH, 1, D_MODEL)
    st = lambda k: jnp.stack(outs[k])
    from_kt = lambda c: c.reshape(DEPTH, DEC_BATCH, N_KV, HEADDIM, WINDOW).transpose(0, 1, 4, 2, 3)
    return (y_prompt, y_sample, st("p_ssm"), st("p_conv"), st("p_k"), st("p_v"),
            h_all.reshape(DEPTH, DEC_BATCH, N_SSM_HEADS, HEADDIM, D_STATE),
            conv_all.transpose(0, 2, 1, 3), from_kt(kv_all[0]), from_kt(kv_all[1]))
```

```python
import functools

import jax
import jax.numpy as jnp
from jax import lax
from jax.experimental import pallas as pl
from jax.experimental.pallas import tpu as pltpu

F32 = jnp.float32
BF16 = jnp.bfloat16

D_MODEL = 2048
SEQ = 8192
DEPTH = 2
DEC_BATCH = 128
D_INNER = 4096
HEADDIM = 64
N_SSM_HEADS = 64
D_STATE = 128
N_GROUPS = 8
GROUP_W = D_INNER // N_GROUPS
CONV_W = 4
BC_W = N_GROUPS * D_STATE
CONV_DIM = D_INNER + 2 * BC_W
CHUNK = 128
N_HEADS = 32
N_KV = 8
Q_PER_KV = 4
KV_DIM = N_KV * HEADDIM
WINDOW = 128
D_FF = 5632
EPS = 1e-6
IN_DIM = 17472

OFF_X = 0
OFF_B = OFF_X + D_INNER
OFF_C = OFF_B + BC_W
OFF_Z = CONV_DIM
OFF_Q = 10240
OFF_K = 12288
OFF_V = 12800
OFF_GS = 13312
OFF_GA = 15360
OFF_DT = 17408
PROJ_W = 17920

LANES = 128
SUBLANES = 8
VMEM_LIMIT = 56 * 1024 * 1024


def _cparams(*sem):
    return pltpu.CompilerParams(dimension_semantics=sem, vmem_limit_bytes=VMEM_LIMIT)


def _dot(a, b):
    return jnp.dot(a, b, preferred_element_type=F32)


def _dot_nt(a, b):
    return lax.dot_general(a, b, (((1,), (1,)), ((), ())), preferred_element_type=F32)


def _div_pow2(v, d):
    return jnp.right_shift(v, d.bit_length() - 1)


def _mod_pow2(v, d):
    return jnp.bitwise_and(v, d - 1)


def _split3(x):
    hi = x.astype(BF16)
    r1 = x - hi.astype(F32)
    mid = r1.astype(BF16)
    lo = (r1 - mid.astype(F32)).astype(BF16)
    return hi, mid, lo


def _dot3_r(x, sel):
    return _dot(jnp.concatenate(_split3(x), axis=1), jnp.concatenate([sel, sel, sel], axis=0))


def _dot3_l(sel, x):
    return _dot(jnp.concatenate([sel, sel, sel], axis=1), jnp.concatenate(_split3(x), axis=0))


def _silu(x):
    h = 0.5 * x
    return h + h * jnp.tanh(h)


def _softplus(x):
    return jnp.maximum(x, 0.0) + jnp.log1p(jnp.exp(-jnp.abs(x)))


def _rms_scale(x, g):
    r = lax.rsqrt(jnp.mean(x * x, axis=-1, keepdims=True) + EPS)
    return x * r * g


def _norm_matmul_kernel(x_ref, g_ref, w_ref, o_ref, xn_ref):
    @pl.when(pl.program_id(1) == 0)
    def _():
        xn_ref[...] = _rms_scale(x_ref[...], g_ref[...]).astype(BF16)

    o_ref[...] = _dot(xn_ref[...], w_ref[...])


def _in_proj(x, g, w, layer, tm, tn):
    m, k = x.shape
    n = w.shape[2]
    return pl.pallas_call(
        _norm_matmul_kernel,
        grid=(m // tm, n // tn),
        in_specs=[pl.BlockSpec((tm, k), lambda i, j: (i, 0)),
                  pl.BlockSpec((1, k), lambda i, j: (0, 0)),
                  pl.BlockSpec((None, k, tn), lambda i, j: (layer, 0, j))],
        out_specs=pl.BlockSpec((tm, tn), lambda i, j: (i, j)),
        out_shape=jax.ShapeDtypeStruct((m, n), F32),
        scratch_shapes=[pltpu.VMEM((tm, k), BF16)],
        compiler_params=_cparams("parallel", "arbitrary"),
        name="in_proj",
    )(x, g, w)


MXU_N = 256
REST_W = OFF_DT - OFF_Z
CONV_ROWS = 64


def _proj_conv_kernel(x_ref, g_ref, w_ref, wdt_ref, cw_ref, cb_ref, xc_ref, dt_ref, tail_ref,
                      xn_ref, xpad_ref, carry_ref):
    i = pl.program_id(0)
    j = pl.program_id(1)
    tm, tn = xc_ref.shape

    @pl.when(j == 0)
    def _():
        xn = _rms_scale(x_ref[...], g_ref[...]).astype(BF16)
        xn_ref[...] = xn
        dt_ref[...] = _dot(xn, wdt_ref[...])

    @pl.when(i == 0)
    def _():
        carry_ref[j] = jnp.zeros((SUBLANES, tn), F32)

    xpad_ref[0:SUBLANES, :] = carry_ref[j]
    for cs in range(tn // MXU_N):
        sl = slice(cs * MXU_N, (cs + 1) * MXU_N)
        xpad_ref[SUBLANES:SUBLANES + tm, sl] = _dot(xn_ref[...], w_ref[:, sl])
        for c0 in range(cs * MXU_N, (cs + 1) * MXU_N, LANES):
            cl = slice(c0, c0 + LANES)
            for r0 in range(0, tm, CONV_ROWS):
                xp = xpad_ref[r0:r0 + SUBLANES + CONV_ROWS, cl]
                acc = cb_ref[:, cl] + xp[SUBLANES:] * cw_ref[CONV_W - 1:CONV_W, cl]
                for back in range(1, CONV_W):
                    k = CONV_W - 1 - back
                    acc = acc + pltpu.roll(xp, back, axis=0)[SUBLANES:] * cw_ref[k:k + 1, cl]
                xc_ref[r0:r0 + CONV_ROWS, cl] = _silu(acc)
    tail = xpad_ref[tm:tm + SUBLANES, :]
    carry_ref[j] = tail
    tail_ref[...] = tail


def _in_proj_conv(x, g, w, layer, cw, cb, tm, tn):
    m, k = x.shape
    ntile = CONV_DIM // tn
    return pl.pallas_call(
        _proj_conv_kernel,
        grid=(m // tm, ntile),
        in_specs=[pl.BlockSpec((tm, k), lambda i, j: (i, 0)),
                  pl.BlockSpec((1, k), lambda i, j: (0, 0)),
                  pl.BlockSpec((None, k, tn), lambda i, j: (layer, 0, OFF_X // tn + j)),
                  pl.BlockSpec((None, k, LANES), lambda i, j: (layer, 0, OFF_DT // LANES)),
                  pl.BlockSpec((CONV_W, tn), lambda i, j: (0, j)),
                  pl.BlockSpec((1, tn), lambda i, j: (0, j))],
        out_specs=[pl.BlockSpec((tm, tn), lambda i, j: (i, j)),
                   pl.BlockSpec((tm, LANES), lambda i, j: (i, 0)),
                   pl.BlockSpec((None, SUBLANES, tn), lambda i, j: (i, 0, j))],
        out_shape=[jax.ShapeDtypeStruct((m, CONV_DIM), F32),
                   jax.ShapeDtypeStruct((m, LANES), F32),
                   jax.ShapeDtypeStruct((m // tm, SUBLANES, CONV_DIM), F32)],
        scratch_shapes=[pltpu.VMEM((tm, k), BF16),
                        pltpu.VMEM((tm + SUBLANES, tn), F32),
                        pltpu.VMEM((ntile, SUBLANES, tn), F32)],
        compiler_params=_cparams("arbitrary", "arbitrary"),
        name="in_proj_conv",
    )(x, g, w, w, cw, cb)


def _proj_rest_kernel(*refs, n_silu_tiles, hosted_tile):
    if hosted_tile is None:
        x_ref, g_ref, w_ref, o_ref, xn_ref = refs
    else:
        x_ref, g_ref, w_ref, wsrc_ref, o_ref, wdst_ref, xn_ref = refs
    j = pl.program_id(1)
    tn = o_ref.shape[1]

    def relayout():
        if hosted_tile is not None:
            t = hosted_tile(pl.program_id(0), j)
            wdst_ref[...] = _w_tile_relayout(wsrc_ref[0], t, W_TILE_HOSTED)

    @pl.when(j == 0)
    def _():
        xn_ref[...] = _rms_scale(x_ref[...], g_ref[...]).astype(BF16)

    @pl.when(j < n_silu_tiles)
    def _():
        relayout()
        for cs in range(tn // MXU_N):
            sl = slice(cs * MXU_N, (cs + 1) * MXU_N)
            o_ref[:, sl] = _silu(_dot(xn_ref[...], w_ref[:, sl]))

    @pl.when(j >= n_silu_tiles)
    def _():
        relayout()
        o_ref[...] = _dot(xn_ref[...], w_ref[...])


def _in_proj_rest(x, g, w, layer, tm, tn, next_w_t=None, next_layer=None):
    m, k = x.shape
    ncol = REST_W // tn
    in_specs = [pl.BlockSpec((tm, k), lambda i, j: (i, 0)),
                pl.BlockSpec((1, k), lambda i, j: (0, 0)),
                pl.BlockSpec((None, k, tn), lambda i, j: (layer, 0, OFF_Z // tn + j))]
    out_specs = [pl.BlockSpec((tm, tn), lambda i, j: (i, j))]
    out_shape = [jax.ShapeDtypeStruct((m, REST_W), F32)]
    args = [x, g, w]
    hosted_tile = None
    if next_w_t is not None:
        n_tiles = PROJ_W // W_TILE_HOSTED
        assert (m // tm) * ncol >= n_tiles
        hosted_tile = lambda i, j: jnp.minimum(i * ncol + j, n_tiles - 1)
        src, dst = _w_tile_specs(next_layer, W_TILE_HOSTED, hosted_tile)
        in_specs.append(src)
        out_specs.append(dst)
        out_shape.append(jax.ShapeDtypeStruct((1, D_MODEL, PROJ_W), BF16))
        args.append(next_w_t)
    return pl.pallas_call(
        functools.partial(_proj_rest_kernel, n_silu_tiles=D_INNER // tn, hosted_tile=hosted_tile),
        grid=(m // tm, ncol),
        in_specs=in_specs,
        out_specs=out_specs,
        out_shape=out_shape,
        scratch_shapes=[pltpu.VMEM((tm, k), BF16)],
        compiler_params=_cparams("arbitrary", "arbitrary"),
        name="in_proj_rest",
    )(*args)


def _ffn_up_kernel(x_ref, g_ref, wg_ref, wu_ref, o_ref, xn_ref):
    @pl.when(pl.program_id(1) == 0)
    def _():
        xn_ref[...] = _rms_scale(x_ref[...], g_ref[...]).astype(BF16)

    xn = xn_ref[...]
    o_ref[...] = (_silu(_dot(xn, wg_ref[...])) * _dot(xn, wu_ref[...])).astype(BF16)


def _ffn_up(x, g, wg, wu, layer, tm, tn):
    m, k = x.shape
    n = wg.shape[2]
    return pl.pallas_call(
        _ffn_up_kernel,
        grid=(m // tm, n // tn),
        in_specs=[pl.BlockSpec((tm, k), lambda i, j: (i, 0)),
                  pl.BlockSpec((1, k), lambda i, j: (0, 0)),
                  pl.BlockSpec((None, k, tn), lambda i, j: (layer, 0, j)),
                  pl.BlockSpec((None, k, tn), lambda i, j: (layer, 0, j))],
        out_specs=pl.BlockSpec((tm, tn), lambda i, j: (i, j)),
        out_shape=jax.ShapeDtypeStruct((m, n), BF16),
        scratch_shapes=[pltpu.VMEM((tm, k), BF16)],
        compiler_params=_cparams("parallel", "arbitrary"),
        name="ffn_up",
    )(x, g, wg, wu)


def _matmul_res_kernel(a_ref, w_ref, r_ref, o_ref):
    o_ref[...] = r_ref[...] + _dot(a_ref[...], w_ref[...])


def _matmul_res(a, w, layer, res, tm, tn, name):
    m, k = a.shape
    n = w.shape[2]
    return pl.pallas_call(
        _matmul_res_kernel,
        grid=(m // tm, n // tn),
        in_specs=[pl.BlockSpec((tm, k), lambda i, j: (i, 0)),
                  pl.BlockSpec((None, k, tn), lambda i, j: (layer, 0, j)),
                  pl.BlockSpec((tm, tn), lambda i, j: (i, j))],
        out_specs=pl.BlockSpec((tm, tn), lambda i, j: (i, j)),
        out_shape=jax.ShapeDtypeStruct((m, n), F32),
        compiler_params=_cparams("parallel", "parallel"),
        name=name,
    )(a, w, res)


def _ffn_down_kernel(a_ref, w_ref, r_ref, g_ref, o_ref, *, final_norm):
    y = r_ref[...] + _dot(a_ref[...], w_ref[...])
    o_ref[...] = _rms_scale(y, g_ref[...]) if final_norm else y


def _ffn_down(a, w, layer, res, gain, final_norm, tm):
    m, k = a.shape
    n = w.shape[2]
    return pl.pallas_call(
        functools.partial(_ffn_down_kernel, final_norm=final_norm),
        grid=(m // tm,),
        in_specs=[pl.BlockSpec((tm, k), lambda i: (i, 0)),
                  pl.BlockSpec((None, k, n), lambda i: (layer, 0, 0), pipeline_mode=pl.Buffered(1)),
                  pl.BlockSpec((tm, n), lambda i: (i, 0)),
                  pl.BlockSpec((1, n), lambda i: (0, 0))],
        out_specs=pl.BlockSpec((tm, n), lambda i: (i, 0)),
        out_shape=jax.ShapeDtypeStruct((m, n), F32),
        compiler_params=_cparams("parallel"),
        name="ffn_down",
    )(a, w, res, gain)


def _merge_kernel(y_ref, o_ref, gs_ref, ga_ref, ws_ref, wa_ref, out_ref):
    ys = _dot(y_ref[...].astype(BF16), ws_ref[...])
    oa = _dot(o_ref[...].astype(BF16), wa_ref[...])
    out_ref[...] = (jax.nn.sigmoid(gs_ref[...]) * ys
                    + jax.nn.sigmoid(ga_ref[...]) * oa).astype(BF16)


def _merge(y, o, proj, col0, ws, wa, layer, tm, tn):
    m = y.shape[0]
    gs0, ga0 = (OFF_GS - col0) // tn, (OFF_GA - col0) // tn
    return pl.pallas_call(
        _merge_kernel,
        grid=(m // tm, D_MODEL // tn),
        in_specs=[pl.BlockSpec((tm, D_INNER), lambda i, j: (i, 0)),
                  pl.BlockSpec((tm, D_MODEL), lambda i, j: (i, 0)),
                  pl.BlockSpec((tm, tn), lambda i, j: (i, gs0 + j)),
                  pl.BlockSpec((tm, tn), lambda i, j: (i, ga0 + j)),
                  pl.BlockSpec((None, D_INNER, tn), lambda i, j: (layer, 0, j)),
                  pl.BlockSpec((None, D_MODEL, tn), lambda i, j: (layer, 0, j))],
        out_specs=pl.BlockSpec((tm, tn), lambda i, j: (i, j)),
        out_shape=jax.ShapeDtypeStruct((m, D_MODEL), BF16),
        compiler_params=_cparams("parallel", "parallel"),
        name="merge",
    )(y, o, proj, proj, ws, wa)


def _gated_norm(y, gate, gain):
    h = y * gate
    r = lax.rsqrt(jnp.mean(h * h, axis=-1, keepdims=True) + EPS)
    return h * r * gain


LOG2E = 1.4426950408889634


def _hosted_casts(arrays, steps):
    specs = [pl.BlockSpec((a.shape[0] // steps, a.shape[1]), lambda i: (i, 0)) for a in arrays]
    shapes = [jax.ShapeDtypeStruct(a.shape, BF16) for a in arrays]
    return specs, shapes


def _run_hosted_casts(src_refs, dst_refs):
    for src, dst in zip(src_refs, dst_refs):
        dst[...] = src[...].astype(BF16)


def _ssd_kernel(*refs, n_cast):
    xc_ref, zs_ref, dt_ref, dtb_ref, alog_ref, dsk_ref, gn_ref = refs[:7]
    y_ref, hout_ref = refs[7 + n_cast:9 + n_cast]
    ht_ref = refs[-1]
    _run_hosted_casts(refs[7:7 + n_cast], refs[9 + n_cast:9 + 2 * n_cast])
    ci = pl.program_id(0)

    @pl.when(ci == 0)
    def _():
        ht_ref[...] = jnp.zeros_like(ht_ref)

    dt = _softplus(dt_ref[...] + dtb_ref[...])
    a = dt * (-jnp.exp(alog_ref[...])) * LOG2E
    row = lax.broadcasted_iota(jnp.int32, (CHUNK, CHUNK), 0)
    col = lax.broadcasted_iota(jnp.int32, (CHUNK, CHUNK), 1)
    causal = col <= row
    tri = jnp.where(causal, 1.0, 0.0).astype(BF16)
    a_cs = _dot3_l(tri, a)
    a_cs_t = a_cs.T
    ea = jnp.exp2(a_cs)
    to_end = jnp.exp2(a_cs[CHUNK - 1:CHUNK, :] - a_cs)
    stack = jnp.concatenate([dt, to_end, ea], axis=0)

    lane = lax.broadcasted_iota(jnp.int32, (CHUNK, LANES), 1)
    low_half = lane < HEADDIM
    hrow = lax.broadcasted_iota(jnp.int32, (LANES, GROUP_W), 0)
    hcol = lax.broadcasted_iota(jnp.int32, (LANES, GROUP_W), 1)

    for g in range(N_GROUPS):
        sl = slice(g * GROUP_W, (g + 1) * GROUP_W)
        sel = jnp.where(hrow == g * (GROUP_W // HEADDIM) + _div_pow2(hcol, HEADDIM), 1.0, 0.0).astype(BF16)
        ex = _dot3_r(stack, sel)
        dt_e, te_e, ea_e = ex[0:CHUNK], ex[CHUNK:2 * CHUNK], ex[2 * CHUNK:3 * CHUNK]

        x_g = xc_ref[:, sl]
        b_g = xc_ref[:, D_INNER + g * D_STATE:D_INNER + (g + 1) * D_STATE]
        c_g = xc_ref[:, D_INNER + BC_W + g * D_STATE:D_INNER + BC_W + (g + 1) * D_STATE]
        xdt = x_g * dt_e
        xdt_b = xdt.astype(BF16)
        cb16, bb16 = c_g.astype(BF16), b_g.astype(BF16)
        cb = _dot_nt(cb16, bb16)

        ht_g = ht_ref[:, sl]
        y_g = _dot(cb16, ht_g.astype(BF16)) * ea_e + dsk_ref[:, sl] * x_g
        states_t = _dot(b_g.T.astype(BF16), (xdt * te_e).astype(BF16))
        ht_ref[:, sl] = ea_e[CHUNK - 1:CHUNK, :] * ht_g + states_t

        y_pairs = []
        for jj in range(GROUP_W // LANES):
            xp = xdt_b[:, jj * LANES:(jj + 1) * LANES]
            halves = []
            for hh in range(2):
                h = g * (GROUP_W // HEADDIM) + 2 * jj + hh
                seg = a_cs[:, h:h + 1] - a_cs_t[h:h + 1, :]
                decay = jnp.exp2(jnp.where(causal, seg, -jnp.inf))
                halves.append(_dot((cb * decay).astype(BF16), xp))
            y_pairs.append(jnp.where(low_half, halves[0], halves[1]))
        y_g = y_g + jnp.concatenate(y_pairs, axis=1)

        y_ref[:, sl] = _gated_norm(y_g, zs_ref[:, sl], gn_ref[:, sl]).astype(BF16)

    @pl.when(ci == pl.num_programs(0) - 1)
    def _():
        for i in range(D_INNER // LANES):
            hout_ref[i * LANES:(i + 1) * LANES, :] = ht_ref[:, i * LANES:(i + 1) * LANES].T


def _ssd_prompt(xc, rest, dt_raw, dtb, alog, dsk, gn, casts=()):
    nchunk = SEQ // CHUNK
    full = lambda r, c: pl.BlockSpec((r, c), lambda i: (0, 0))
    cast_specs, cast_shapes = _hosted_casts(casts, nchunk)
    return pl.pallas_call(
        functools.partial(_ssd_kernel, n_cast=len(casts)),
        grid=(nchunk,),
        in_specs=[pl.BlockSpec((CHUNK, CONV_DIM), lambda i: (i, 0)),
                  pl.BlockSpec((CHUNK, D_INNER), lambda i: (i, 0)),
                  pl.BlockSpec((CHUNK, LANES), lambda i: (i, 0)),
                  full(1, LANES), full(1, LANES), full(1, D_INNER), full(1, D_INNER)] + cast_specs,
        out_specs=[pl.BlockSpec((CHUNK, D_INNER), lambda i: (i, 0)),
                   pl.BlockSpec((D_INNER, D_STATE), lambda i: (0, 0))] + cast_specs,
        out_shape=[jax.ShapeDtypeStruct((SEQ, D_INNER), BF16),
                   jax.ShapeDtypeStruct((D_INNER, D_STATE), F32)] + cast_shapes,
        scratch_shapes=[pltpu.VMEM((D_STATE, D_INNER), F32)],
        compiler_params=_cparams("arbitrary"),
        name="ssd_prompt",
    )(xc, rest, dt_raw, dtb, alog, dsk, gn, *casts)


def _attn_kernel(*refs, n_cast):
    sink_ref, q_ref, kp_ref, kc_ref, vp_ref, vc_ref = refs[:6]
    o_ref = refs[6 + n_cast]
    _run_hosted_casts(refs[6:6 + n_cast], refs[7 + n_cast:7 + 2 * n_cast])
    i = pl.program_id(0)
    band = 2 * CHUNK
    l_idx = lax.broadcasted_iota(jnp.int32, (CHUNK, band), 0)
    s_idx = lax.broadcasted_iota(jnp.int32, (CHUNK, band), 1)
    first_key = jnp.where(i > 0, 0, CHUNK)
    mask = (s_idx >= jnp.maximum(l_idx, first_key)) & (s_idx <= l_idx + WINDOW)
    low_q = lax.broadcasted_iota(jnp.int32, (CHUNK, LANES), 1) < HEADDIM
    low_kv = lax.broadcasted_iota(jnp.int32, (band, LANES), 1) < HEADDIM

    for c in range(KV_DIM // LANES):
        csl = slice(c * LANES, (c + 1) * LANES)
        k2 = jnp.concatenate([kp_ref[:, csl], kc_ref[:, csl]], axis=0)
        v2 = jnp.concatenate([vp_ref[:, csl], vc_ref[:, csl]], axis=0)
        k2r = pltpu.roll(k2, HEADDIM, axis=1)
        v2r = pltpu.roll(v2, HEADDIM, axis=1)
        for u in range(2):
            kv_head = 2 * c + u
            kd = (jnp.where(low_kv, k2, k2r) if u == 0 else jnp.where(low_kv, k2r, k2)).astype(BF16)
            vd = (jnp.where(low_kv, v2, v2r) if u == 0 else jnp.where(low_kv, v2r, v2)).astype(BF16)
            for jj in range(Q_PER_KV // 2):
                j = kv_head * (Q_PER_KV // 2) + jj
                q2 = q_ref[:, j * LANES:(j + 1) * LANES] * (HEADDIM ** -0.5)
                halves = []
                for hh in range(2):
                    qm = jnp.where(low_q if hh == 0 else jnp.logical_not(low_q), q2, 0.0)
                    s = jnp.where(mask, _dot_nt(qm.astype(BF16), kd), -jnp.inf)
                    sink = sink_ref[2 * j + hh]
                    m = jnp.maximum(jnp.max(s, axis=-1, keepdims=True), sink)
                    p = jnp.exp(s - m)
                    den = jnp.sum(p, axis=-1, keepdims=True) + jnp.exp(sink - m)
                    halves.append(_dot(p.astype(BF16), vd) / den)
                o_ref[:, j * LANES:(j + 1) * LANES] = jnp.where(low_q, halves[0], halves[1]).astype(BF16)


def _attn_prompt(proj, col0, sinks, casts=()):
    nb = SEQ // CHUNK
    prev = lambda i: jnp.maximum(i - 1, 0)
    qb, kb, vb = (OFF_Q - col0) // D_MODEL, (OFF_K - col0) // KV_DIM, (OFF_V - col0) // KV_DIM
    cast_specs, cast_shapes = _hosted_casts(casts, nb)
    return pl.pallas_call(
        functools.partial(_attn_kernel, n_cast=len(casts)),
        grid=(nb,),
        in_specs=[pl.BlockSpec(memory_space=pltpu.SMEM),
                  pl.BlockSpec((CHUNK, D_MODEL), lambda i: (i, qb)),
                  pl.BlockSpec((CHUNK, KV_DIM), lambda i: (prev(i), kb)),
                  pl.BlockSpec((CHUNK, KV_DIM), lambda i: (i, kb)),
                  pl.BlockSpec((CHUNK, KV_DIM), lambda i: (prev(i), vb)),
                  pl.BlockSpec((CHUNK, KV_DIM), lambda i: (i, vb))] + cast_specs,
        out_specs=[pl.BlockSpec((CHUNK, D_MODEL), lambda i: (i, 0))] + cast_specs,
        out_shape=[jax.ShapeDtypeStruct((SEQ, D_MODEL), BF16)] + cast_shapes,
        compiler_params=_cparams("parallel"),
        name="attn_prompt",
    )(sinks, proj, proj, proj, proj, proj, *casts)


def _alias_prev(prev, n_in, first_out):
    if prev is None:
        return [], {}, ()
    specs = [pl.BlockSpec(memory_space=pl.ANY)] * len(prev)
    return specs, {n_in + t: first_out + t for t in range(len(prev))}, tuple(prev)


def _conv_step_kernel(x_ref, dt_ref, cs_ref, cw_ref, cb_ref, dtb_ref, alog_ref, *rest):
    xc_ref, csn_ref, dto_ref, dao_ref = rest[-4:]
    taps = [cs_ref[k] for k in range(CONV_W - 1)] + [x_ref[...]]
    acc = cb_ref[...]
    for k in range(CONV_W):
        acc = acc + taps[k] * cw_ref[k:k + 1, :]
    xc_ref[...] = _silu(acc)
    for k in range(CONV_W - 1):
        csn_ref[k] = taps[k + 1]
    dt = _softplus(dt_ref[...] + dtb_ref[...])
    dto_ref[...] = dt
    dao_ref[...] = jnp.exp(dt * (-jnp.exp(alog_ref[...])))


def _conv_step(proj, conv_state, layer, cw, cb, dtb, alog, prev):
    nb = DEC_BATCH
    small = pl.BlockSpec((1, LANES), lambda s: (0, 0))
    head = pl.BlockSpec((nb, LANES), lambda s: (0, 0))
    state = pl.BlockSpec((None, CONV_W - 1, nb, GROUP_W), lambda s: (layer, 0, 0, s))
    in_specs = [pl.BlockSpec((nb, GROUP_W), lambda s: (0, OFF_X // GROUP_W + s)),
                pl.BlockSpec((nb, LANES), lambda s: (0, OFF_DT // LANES)),
                state,
                pl.BlockSpec((CONV_W, GROUP_W), lambda s: (0, s)),
                pl.BlockSpec((1, GROUP_W), lambda s: (0, s)),
                small, small]
    alias_specs, aliases, alias_args = _alias_prev(prev, len(in_specs), 1)
    return pl.pallas_call(
        _conv_step_kernel,
        grid=(CONV_DIM // GROUP_W,),
        in_specs=in_specs + alias_specs,
        out_specs=[pl.BlockSpec((nb, GROUP_W), lambda s: (0, s)), state, head, head],
        out_shape=[jax.ShapeDtypeStruct((nb, CONV_DIM), F32),
                   jax.ShapeDtypeStruct(conv_state.shape, F32),
                   jax.ShapeDtypeStruct((nb, LANES), F32),
                   jax.ShapeDtypeStruct((nb, LANES), F32)],
        input_output_aliases=aliases,
        compiler_params=_cparams("arbitrary"),
        name="conv_step",
    )(proj, proj, conv_state, cw, cb, dtb, alog, *alias_args)


STEP_B = 8
STEP_GROUPS = 4
STEP_W = STEP_GROUPS * GROUP_W


def _ssm_step_kernel(xs_ref, b_ref, c_ref, z_ref, dt_ref, da_ref, dsk_ref, gn_ref, h_ref, *rest):
    y_ref, hn_ref = rest[-2:]
    part = pl.program_id(1)
    xs = xs_ref[...]
    hrow = lax.broadcasted_iota(jnp.int32, (LANES, STEP_W), 0)
    hcol = lax.broadcasted_iota(jnp.int32, (LANES, STEP_W), 1)
    sel = jnp.where(hrow == part * (STEP_W // HEADDIM) + _div_pow2(hcol, HEADDIM), 1.0, 0.0).astype(BF16)
    dt_e = _dot3_r(dt_ref[...], sel)
    da_e = _dot3_r(da_ref[...], sel)
    xdt = xs * dt_e
    lane_group = _div_pow2(lax.broadcasted_iota(jnp.int32, (STEP_B, STEP_W), 1), GROUP_W)

    pieces = [jnp.where(lane_group == g, xdt, 0.0) for g in range(STEP_GROUPS)]
    pieces += [p.astype(F32) for p in _split3(da_e)]
    n_used = len(pieces) * STEP_B
    pieces.append(jnp.zeros((LANES - n_used, STEP_W), F32))
    ut = jnp.concatenate(pieces, axis=0)
    u_big = jnp.concatenate([ut[:, i * LANES:(i + 1) * LANES].T for i in range(STEP_W // LANES)],
                            axis=0).astype(BF16)

    b_rows = jnp.concatenate([b_ref[:, g * D_STATE:(g + 1) * D_STATE] for g in range(STEP_GROUPS)]
                             + [jnp.zeros((LANES - STEP_GROUPS * STEP_B, D_STATE), F32)], axis=0)
    rrow = lax.broadcasted_iota(jnp.int32, (LANES, D_STATE), 0)
    ones_rows = jnp.where((rrow >= STEP_GROUPS * STEP_B) & (rrow < n_used), 1.0, 0.0)
    big_r = jnp.concatenate([b_rows, ones_rows], axis=1)
    r_seq = _mod_pow2(lax.broadcasted_iota(jnp.int32, (LANES, 2 * D_STATE), 0), STEP_B)

    cc = jnp.concatenate([c_ref[:, g * D_STATE:(g + 1) * D_STATE] for g in range(STEP_GROUPS)],
                         axis=0).astype(BF16)
    crow = lax.broadcasted_iota(jnp.int32, (STEP_GROUPS * STEP_B, STEP_W), 0)
    ccol_group = _div_pow2(lax.broadcasted_iota(jnp.int32, (STEP_GROUPS * STEP_B, STEP_W), 1), GROUP_W)
    sub = lax.broadcasted_iota(jnp.int32, (STEP_B, STEP_W), 0)

    y_off = jnp.zeros((STEP_B, STEP_W), F32)
    for b in range(STEP_B):
        h0 = h_ref[b]
        res = _dot(u_big, jnp.where(r_seq == b, big_r, 0.0).astype(BF16))
        hn_ref[b] = res[:, D_STATE:] * h0 + res[:, :D_STATE]
        y_all = _dot_nt(cc, h0.astype(BF16))
        pick = (crow == ccol_group * STEP_B + b)
        y_b = jnp.sum(jnp.where(pick, y_all, 0.0), axis=0, keepdims=True)
        y_off = jnp.where(sub == b, y_b, y_off)

    cb_e = jnp.zeros((STEP_B, STEP_W), F32)
    for g in range(STEP_GROUPS):
        gsl = slice(g * D_STATE, (g + 1) * D_STATE)
        cb_g = jnp.sum(c_ref[:, gsl] * b_ref[:, gsl], axis=-1, keepdims=True)
        cb_e = jnp.where(lane_group == g, cb_g, cb_e)

    y = da_e * y_off + cb_e * xdt + dsk_ref[...] * xs
    for g in range(STEP_GROUPS):
        sl = slice(g * GROUP_W, (g + 1) * GROUP_W)
        y_ref[:, sl] = _gated_norm(y[:, sl], _silu(z_ref[:, sl]), gn_ref[:, sl])


def _ssm_step(xc, proj, dt, da, dsk, gn, state, layer, prev):
    nparts = D_INNER // STEP_W
    state_spec = pl.BlockSpec((None, STEP_B, STEP_W, D_STATE), lambda i, p: (layer, i, p, 0))
    in_specs = [pl.BlockSpec((STEP_B, STEP_W), lambda i, p: (i, p)),
                pl.BlockSpec((STEP_B, STEP_GROUPS * D_STATE),
                             lambda i, p: (i, D_INNER // (STEP_GROUPS * D_STATE) + p)),
                pl.BlockSpec((STEP_B, STEP_GROUPS * D_STATE),
                             lambda i, p: (i, (D_INNER + BC_W) // (STEP_GROUPS * D_STATE) + p)),
                pl.BlockSpec((STEP_B, STEP_W), lambda i, p: (i, OFF_Z // STEP_W + p)),
                pl.BlockSpec((STEP_B, LANES), lambda i, p: (i, 0)),
                pl.BlockSpec((STEP_B, LANES), lambda i, p: (i, 0)),
                pl.BlockSpec((1, STEP_W), lambda i, p: (0, p)),
                pl.BlockSpec((1, STEP_W), lambda i, p: (0, p)),
                state_spec]
    alias_specs, aliases, alias_args = _alias_prev(prev, len(in_specs), 1)
    return pl.pallas_call(
        _ssm_step_kernel,
        grid=(DEC_BATCH // STEP_B, nparts),
        in_specs=in_specs + alias_specs,
        out_specs=[pl.BlockSpec((STEP_B, STEP_W), lambda i, p: (i, p)), state_spec],
        out_shape=[jax.ShapeDtypeStruct((DEC_BATCH, D_INNER), F32),
                   jax.ShapeDtypeStruct(state.shape, F32)],
        input_output_aliases=aliases,
        compiler_params=_cparams("parallel", "parallel"),
        name="ssm_step",
    )(xc, xc, xc, proj, dt, da, dsk, gn, state, *alias_args)


ATTN_B = 8


def _attn_step_kernel(sink_ref, q_ref, kn_ref, vn_ref, kc_ref, vc_ref, *rest):
    o_ref, ko_ref, vo_ref = rest[-3:]
    lane_kv = _div_pow2(lax.broadcasted_iota(jnp.int32, (N_KV, KV_DIM), 1), HEADDIM)
    own = lane_kv == lax.broadcasted_iota(jnp.int32, (N_KV, KV_DIM), 0)
    newest = lax.broadcasted_iota(jnp.int32, (KV_DIM, WINDOW), 1) == WINDOW - 1
    sink = sink_ref[...]
    pad = jnp.zeros((LANES - ATTN_B, KV_DIM), F32)
    kn_t = jnp.concatenate([kn_ref[...], pad], axis=0).T
    vn_t = jnp.concatenate([vn_ref[...], pad], axis=0).T
    for b in range(ATTN_B):
        qrow = q_ref[b:b + 1, :] * (HEADDIM ** -0.5)
        qm = jnp.concatenate(
            [jnp.where(own, jnp.broadcast_to(qrow[:, r * KV_DIM:(r + 1) * KV_DIM], (N_KV, KV_DIM)), 0.0)
             for r in range(Q_PER_KV)], axis=0)
        k_new = kn_ref[b:b + 1, :]
        v_new = vn_ref[b:b + 1, :]
        kt = kc_ref[b]
        vt = vc_ref[b]
        s = _dot(qm.astype(BF16), kt.astype(BF16))
        s_new = jnp.sum(qm * k_new, axis=-1, keepdims=True)
        m = jnp.maximum(jnp.maximum(jnp.max(s, axis=-1, keepdims=True), s_new), sink)
        p = jnp.exp(s - m)
        p_new = jnp.exp(s_new - m)
        den = jnp.sum(p, axis=-1, keepdims=True) + p_new + jnp.exp(sink - m)
        o_full = _dot_nt((p / den).astype(BF16), vt.astype(BF16)) + (p_new / den) * v_new
        parts = []
        for r in range(Q_PER_KV):
            o_r = jnp.where(own, o_full[r * N_KV:(r + 1) * N_KV, :], 0.0)
            parts.append(jnp.sum(o_r, axis=0, keepdims=True))
        o_ref[b:b + 1, :] = jnp.concatenate(parts, axis=1)
        ko_ref[b] = jnp.where(newest, jnp.broadcast_to(kn_t[:, b:b + 1], (KV_DIM, WINDOW)),
                              pltpu.roll(kt, WINDOW - 1, axis=1))
        vo_ref[b] = jnp.where(newest, jnp.broadcast_to(vn_t[:, b:b + 1], (KV_DIM, WINDOW)),
                              pltpu.roll(vt, WINDOW - 1, axis=1))


def _attn_step(q, proj, sinks, cache_k, cache_v, layer, prev):
    cache_spec = pl.BlockSpec((None, ATTN_B, KV_DIM, WINDOW), lambda i: (layer, i, 0, 0))
    in_specs = [pl.BlockSpec((N_HEADS, 1), lambda i: (0, 0)),
                pl.BlockSpec((ATTN_B, D_MODEL), lambda i: (i, 0)),
                pl.BlockSpec((ATTN_B, KV_DIM), lambda i: (i, OFF_K // KV_DIM)),
                pl.BlockSpec((ATTN_B, KV_DIM), lambda i: (i, OFF_V // KV_DIM)),
                cache_spec, cache_spec]
    alias_specs, aliases, alias_args = _alias_prev(prev, len(in_specs), 1)
    return pl.pallas_call(
        _attn_step_kernel,
        grid=(DEC_BATCH // ATTN_B,),
        in_specs=in_specs + alias_specs,
        out_specs=[pl.BlockSpec((ATTN_B, D_MODEL), lambda i: (i, 0)), cache_spec, cache_spec],
        out_shape=[jax.ShapeDtypeStruct((DEC_BATCH, D_MODEL), F32),
                   jax.ShapeDtypeStruct(cache_k.shape, F32),
                   jax.ShapeDtypeStruct(cache_v.shape, F32)],
        input_output_aliases=aliases,
        compiler_params=_cparams("parallel"),
        name="attn_step",
    )(sinks, q, proj, proj, cache_k, cache_v, *alias_args)


SRC_X = D_INNER
SRC_DT = SRC_X + CONV_DIM
W_TILE = 512
W_TILE_HOSTED = 256


def _w_tile_src_row(t, tile_w):
    per_tile = tile_w // N_SSM_HEADS
    unit = jnp.where(
        t < OFF_Z // tile_w, SRC_X // N_SSM_HEADS + t * per_tile,
        jnp.where(t < OFF_Q // tile_w, (t - OFF_Z // tile_w) * per_tile,
                  jnp.where(t < OFF_DT // tile_w, t * per_tile + 1, SRC_DT // N_SSM_HEADS)))
    return unit * N_SSM_HEADS


def _w_tile_relayout(x, t, tile_w):
    first_pad = OFF_DT // tile_w
    valid = jnp.where(t < first_pad, tile_w, jnp.where(t == first_pad, N_SSM_HEADS, 0))
    row = lax.broadcasted_iota(jnp.int32, x.shape, 0)
    return jnp.where(row < valid, x, 0.0).T.astype(BF16)


def _w_tile_specs(layer, tile_w, tile_of):
    src = pl.BlockSpec((pl.Element(1), pl.Element(tile_w), pl.Element(D_MODEL)),
                       lambda *ids: (layer, _w_tile_src_row(tile_of(*ids), tile_w), 0))
    dst = pl.BlockSpec((None, D_MODEL, tile_w), lambda *ids: (0, 0, tile_of(*ids)))
    return src, dst


def _w_in_layout_kernel(x_ref, o_ref):
    o_ref[...] = _w_tile_relayout(x_ref[0], pl.program_id(0), W_TILE)


def _w_in_layout(w_t, layer):
    src, dst = _w_tile_specs(layer, W_TILE, lambda t: t)
    return pl.pallas_call(
        _w_in_layout_kernel,
        grid=(PROJ_W // W_TILE,),
        in_specs=[src],
        out_specs=dst,
        out_shape=jax.ShapeDtypeStruct((1, D_MODEL, PROJ_W), BF16),
        compiler_params=_cparams("parallel"),
        name="w_in_layout",
    )(w_t)


def _small_params(l, norm_mix, conv_w, conv_b, dt_bias, a_log, d_skip, ssm_norm, attn_sinks, norm_ffn):
    pad = lambda v: jnp.pad(v, (0, LANES - N_SSM_HEADS)).reshape(1, LANES)
    return dict(
        norm_mix=norm_mix[l].reshape(1, D_MODEL),
        conv_w=conv_w[l], conv_b=conv_b[l].reshape(1, CONV_DIM),
        dt_bias=pad(dt_bias[l]), a_log=pad(a_log[l]),
        d_skip=jnp.repeat(d_skip[l], HEADDIM).reshape(1, D_INNER),
        ssm_norm=ssm_norm[l].reshape(1, D_INNER),
        sinks=attn_sinks[l],
        sinks_rm=attn_sinks[l].reshape(N_KV, Q_PER_KV).T.reshape(N_HEADS, 1),
        norm_ffn=norm_ffn[l].reshape(1, D_MODEL))


def _dense_tail(x, y, o, proj, col0, p, w, l, tm, norm_final):
    merged = _merge(y, o, proj, col0, w["w_ssm"], w["w_attn"], l, tm, 512)
    x1 = _matmul_res(merged, w["w_out"], l, x, min(tm, 512), D_MODEL, "out_proj")
    hmid = _ffn_up(x1, p["norm_ffn"], w["w_gate"], w["w_up"], l, tm, 512)
    return _ffn_down(hmid, w["w_down"], l, x1, norm_final, l == DEPTH - 1, min(tm, 512))


PROMPT_TM = 1024
PROJ_TN = 1280


def kernel(x_prompt, x_sample, state_ssm, state_conv, cache_k, cache_v, norm_mix, w_in, conv_w,
           conv_b, dt_bias, a_log, d_skip, ssm_norm, w_ssm_proj, w_attn_proj, attn_sinks, w_out,
           norm_ffn, w_gate, w_up, w_down, norm_final):
    xp = x_prompt.reshape(SEQ, D_MODEL)
    xs = x_sample.reshape(DEC_BATCH, D_MODEL)
    state4 = state_ssm.reshape(DEPTH, DEC_BATCH, D_INNER, D_STATE)
    to_kt = lambda c: c.transpose(0, 1, 3, 4, 2).reshape(DEPTH, DEC_BATCH, KV_DIM, WINDOW)
    ck, cv = to_kt(cache_k), to_kt(cache_v)
    conv_t = state_conv.transpose(0, 2, 1, 3)
    outs = {k: [] for k in ("p_ssm", "p_conv", "p_k", "p_v")}

    w_in_t = w_in.transpose(0, 2, 1)
    w = {}
    w_in_l = _w_in_layout(w_in_t, 0)
    rows =lambda t: t.reshape(-1, t.shape[-1])
    ssd_casts = dict(w_ssm=w_ssm_proj, w_attn=w_attn_proj, w_out=w_out, w_down=w_down)
    attn_casts = dict(w_gate=w_gate, w_up=w_up)

    g_final = norm_final.reshape(1, D_MODEL)
    h_all = conv_all = kv_all = None
    for l in range(DEPTH):
        p = _small_params(l, norm_mix, conv_w, conv_b, dt_bias, a_log, d_skip, ssm_norm,
                          attn_sinks, norm_ffn)

        xc_p, dt_p, tails = _in_proj_conv(xp, p["norm_mix"], w_in_l, 0, p["conv_w"], p["conv_b"],
                                          PROMPT_TM, 1024)
        if l + 1 < DEPTH:
            rest, w_in_next = _in_proj_rest(xp, p["norm_mix"], w_in_l, 0, PROMPT_TM, 1024, w_in_t, l + 1)
        else:
            (rest,), w_in_next = _in_proj_rest(xp, p["norm_mix"], w_in_l, 0, PROMPT_TM, 1024), None
        hosted = [ssd_casts, attn_casts] if l == 0 else [{}, {}]
        y, h_p, *cast_a = _ssd_prompt(xc_p, rest, dt_p, p["dt_bias"], p["a_log"], p["d_skip"],
                                      p["ssm_norm"], [rows(t) for t in hosted[0].values()])
        o, *cast_b = _attn_prompt(rest, OFF_Z, p["sinks"], [rows(t) for t in hosted[1].values()])
        for group, done in zip(hosted, (cast_a, cast_b)):
            w.update({k: c.reshape(t.shape) for (k, t), c in zip(group.items(), done)})
        last_kv = lambda off: rest[SEQ - WINDOW:, off - OFF_Z:off - OFF_Z + KV_DIM].reshape(
            1, WINDOW, N_KV, HEADDIM)
        outs["p_ssm"].append(h_p.reshape(1, N_SSM_HEADS, HEADDIM, D_STATE))
        outs["p_conv"].append(tails[-1, SUBLANES - (CONV_W - 1):].reshape(1, CONV_W - 1, CONV_DIM))
        outs["p_k"].append(last_kv(OFF_K))
        outs["p_v"].append(last_kv(OFF_V))
        xp = _dense_tail(xp, y, o, rest, OFF_Z, p, w, l, PROMPT_TM, g_final)

        proj_s = _in_proj(xs, p["norm_mix"], w_in_l, 0, DEC_BATCH, PROJ_TN)
        xc, conv_all, dt, da = _conv_step(proj_s, conv_t, l, p["conv_w"], p["conv_b"],
                                          p["dt_bias"], p["a_log"],
                                          None if conv_all is None else (conv_all,))
        y_s, h_all = _ssm_step(xc, proj_s, dt, da, p["d_skip"], p["ssm_norm"], state4, l,
                               None if h_all is None else (h_all,))
        swap_rm = lambda t, a, b: t.reshape(DEC_BATCH, a, b, HEADDIM).transpose(0, 2, 1, 3).reshape(
            DEC_BATCH, D_MODEL)
        q_s = swap_rm(proj_s[:, OFF_Q:OFF_Q + D_MODEL], N_KV, Q_PER_KV)
        o_s, k_all, v_all = _attn_step(q_s, proj_s, p["sinks_rm"], ck, cv, l, kv_all)
        o_s = swap_rm(o_s, Q_PER_KV, N_KV)
        kv_all = (k_all, v_all)
        xs = _dense_tail(xs, y_s, o_s, proj_s, 0, p, w, l, DEC_BATCH, g_final)
        w_in_l = w_in_next

    y_prompt = xp.reshape(1, SEQ, D_MODEL)
    y_sample = xs.reshape(DEC_BATCH, 1, D_MODEL)
    st = lambda k: jnp.stack(outs[k])
    from_kt = lambda c: c.reshape(DEPTH, DEC_BATCH, N_KV, HEADDIM, WINDOW).transpose(0, 1, 4, 2, 3)
    return (y_prompt, y_sample, st("p_ssm"), st("p_conv"), st("p_k"), st("p_v"),
            h_all.reshape(DEPTH, DEC_BATCH, N_SSM_HEADS, HEADDIM, D_STATE),
            conv_all.transpose(0, 2, 1, 3), from_kt(kv_all[0]), from_kt(kv_all[1]))
```

```python
import functools

import jax
import jax.numpy as jnp
from jax import lax
from jax.experimental import pallas as pl
from jax.experimental.pallas import tpu as pltpu

F32 = jnp.float32
BF16 = jnp.bfloat16

D_MODEL = 2048
SEQ = 8192
DEPTH = 2
DEC_BATCH = 128
D_INNER = 4096
HEADDIM = 64
N_SSM_HEADS = 64
D_STATE = 128
N_GROUPS = 8
GROUP_W = D_INNER // N_GROUPS
CONV_W = 4
BC_W = N_GROUPS * D_STATE
CONV_DIM = D_INNER + 2 * BC_W
CHUNK = 128
N_HEADS = 32
N_KV = 8
Q_PER_KV = 4
KV_DIM = N_KV * HEADDIM
WINDOW = 128
D_FF = 5632
EPS = 1e-6
IN_DIM = 17472

OFF_X = 0
OFF_B = OFF_X + D_INNER
OFF_C = OFF_B + BC_W
OFF_Z = CONV_DIM
OFF_Q = 10240
OFF_K = 12288
OFF_V = 12800
OFF_GS = 13312
OFF_GA = 15360
OFF_DT = 17408
PROJ_W = 17920

LANES = 128
SUBLANES = 8
VMEM_LIMIT = 56 * 1024 * 1024


def _cparams(*sem):
    return pltpu.CompilerParams(dimension_semantics=sem, vmem_limit_bytes=VMEM_LIMIT)


def _dot(a, b):
    return jnp.dot(a, b, preferred_element_type=F32)


def _dot_nt(a, b):
    return lax.dot_general(a, b, (((1,), (1,)), ((), ())), preferred_element_type=F32)


def _div_pow2(v, d):
    return jnp.right_shift(v, d.bit_length() - 1)


def _mod_pow2(v, d):
    return jnp.bitwise_and(v, d - 1)


def _split3(x):
    hi = x.astype(BF16)
    r1 = x - hi.astype(F32)
    mid = r1.astype(BF16)
    lo = (r1 - mid.astype(F32)).astype(BF16)
    return hi, mid, lo


def _dot3_r(x, sel):
    return _dot(jnp.concatenate(_split3(x), axis=1), jnp.concatenate([sel, sel, sel], axis=0))


def _dot3_l(sel, x):
    return _dot(jnp.concatenate([sel, sel, sel], axis=1), jnp.concatenate(_split3(x), axis=0))


def _silu(x):
    h = 0.5 * x
    return h + h * jnp.tanh(h)


def _softplus(x):
    return jnp.maximum(x, 0.0) + jnp.log1p(jnp.exp(-jnp.abs(x)))


def _rms_scale(x, g):
    r = lax.rsqrt(jnp.mean(x * x, axis=-1, keepdims=True) + EPS)
    return x * r * g


def _norm_matmul_kernel(x_ref, g_ref, w_ref, o_ref, xn_ref):
    @pl.when(pl.program_id(1) == 0)
    def _():
        xn_ref[...] = _rms_scale(x_ref[...], g_ref[...]).astype(BF16)

    o_ref[...] = _dot(xn_ref[...], w_ref[...])


def _in_proj(x, g, w, layer, tm, tn):
    m, k = x.shape
    n = w.shape[2]
    return pl.pallas_call(
        _norm_matmul_kernel,
        grid=(m // tm, n // tn),
        in_specs=[pl.BlockSpec((tm, k), lambda i, j: (i, 0)),
                  pl.BlockSpec((1, k), lambda i, j: (0, 0)),
                  pl.BlockSpec((None, k, tn), lambda i, j: (layer, 0, j))],
        out_specs=pl.BlockSpec((tm, tn), lambda i, j: (i, j)),
        out_shape=jax.ShapeDtypeStruct((m, n), F32),
        scratch_shapes=[pltpu.VMEM((tm, k), BF16)],
        compiler_params=_cparams("parallel", "arbitrary"),
        name="in_proj",
    )(x, g, w)


MXU_N = 256
REST_W = OFF_DT - OFF_Z
CONV_ROWS = 64


def _proj_conv_kernel(x_ref, g_ref, w_ref, wdt_ref, cw_ref, cb_ref, xc_ref, dt_ref, tail_ref,
                      xn_ref, xpad_ref, carry_ref):
    i = pl.program_id(0)
    j = pl.program_id(1)
    tm, tn = xc_ref.shape

    @pl.when(j == 0)
    def _():
        xn = _rms_scale(x_ref[...], g_ref[...]).astype(BF16)
        xn_ref[...] = xn
        dt_ref[...] = _dot(xn, wdt_ref[...])

    @pl.when(i == 0)
    def _():
        carry_ref[j] = jnp.zeros((SUBLANES, tn), F32)

    xpad_ref[0:SUBLANES, :] = carry_ref[j]
    for cs in range(tn // MXU_N):
        sl = slice(cs * MXU_N, (cs + 1) * MXU_N)
        xpad_ref[SUBLANES:SUBLANES + tm, sl] = _dot(xn_ref[...], w_ref[:, sl])
        for c0 in range(cs * MXU_N, (cs + 1) * MXU_N, LANES):
            cl = slice(c0, c0 + LANES)
            for r0 in range(0, tm, CONV_ROWS):
                xp = xpad_ref[r0:r0 + SUBLANES + CONV_ROWS, cl]
                acc = cb_ref[:, cl] + xp[SUBLANES:] * cw_ref[CONV_W - 1:CONV_W, cl]
                for back in range(1, CONV_W):
                    k = CONV_W - 1 - back
                    acc = acc + pltpu.roll(xp, back, axis=0)[SUBLANES:] * cw_ref[k:k + 1, cl]
                xc_ref[r0:r0 + CONV_ROWS, cl] = _silu(acc)
    tail = xpad_ref[tm:tm + SUBLANES, :]
    carry_ref[j] = tail
    tail_ref[...] = tail


def _in_proj_conv(x, g, w, layer, cw, cb, tm, tn):
    m, k = x.shape
    ntile = CONV_DIM // tn
    return pl.pallas_call(
        _proj_conv_kernel,
        grid=(m // tm, ntile),
        in_specs=[pl.BlockSpec((tm, k), lambda i, j: (i, 0)),
                  pl.BlockSpec((1, k), lambda i, j: (0, 0)),
                  pl.BlockSpec((None, k, tn), lambda i, j: (layer, 0, OFF_X // tn + j)),
                  pl.BlockSpec((None, k, LANES), lambda i, j: (layer, 0, OFF_DT // LANES)),
                  pl.BlockSpec((CONV_W, tn), lambda i, j: (0, j)),
                  pl.BlockSpec((1, tn), lambda i, j: (0, j))],
        out_specs=[pl.BlockSpec((tm, tn), lambda i, j: (i, j)),
                   pl.BlockSpec((tm, LANES), lambda i, j: (i, 0)),
                   pl.BlockSpec((None, SUBLANES, tn), lambda i, j: (i, 0, j))],
        out_shape=[jax.ShapeDtypeStruct((m, CONV_DIM), F32),
                   jax.ShapeDtypeStruct((m, LANES), F32),
                   jax.ShapeDtypeStruct((m // tm, SUBLANES, CONV_DIM), F32)],
        scratch_shapes=[pltpu.VMEM((tm, k), BF16),
                        pltpu.VMEM((tm + SUBLANES, tn), F32),
                        pltpu.VMEM((ntile, SUBLANES, tn), F32)],
        compiler_params=_cparams("arbitrary", "arbitrary"),
        name="in_proj_conv",
    )(x, g, w, w, cw, cb)


def _proj_rest_kernel(*refs, n_silu_tiles, hosted_tile):
    if hosted_tile is None:
        x_ref, g_ref, w_ref, o_ref, xn_ref = refs
    else:
        x_ref, g_ref, w_ref, wsrc_ref, o_ref, wdst_ref, xn_ref = refs
    j = pl.program_id(1)
    tn = o_ref.shape[1]

    def relayout():
        if hosted_tile is not None:
            t = hosted_tile(pl.program_id(0), j)
            wdst_ref[...] = _w_tile_relayout(wsrc_ref[0], t, W_TILE_HOSTED)

    @pl.when(j == 0)
    def _():
        xn_ref[...] = _rms_scale(x_ref[...], g_ref[...]).astype(BF16)

    @pl.when(j < n_silu_tiles)
    def _():
        relayout()
        for cs in range(tn // MXU_N):
            sl = slice(cs * MXU_N, (cs + 1) * MXU_N)
            o_ref[:, sl] = _silu(_dot(xn_ref[...], w_ref[:, sl]))

    @pl.when(j >= n_silu_tiles)
    def _():
        relayout()
        o_ref[...] = _dot(xn_ref[...], w_ref[...])


def _in_proj_rest(x, g, w, layer, tm, tn, next_w_t=None, next_layer=None):
    m, k = x.shape
    ncol = REST_W // tn
    in_specs = [pl.BlockSpec((tm, k), lambda i, j: (i, 0)),
                pl.BlockSpec((1, k), lambda i, j: (0, 0)),
                pl.BlockSpec((None, k, tn), lambda i, j: (layer, 0, OFF_Z // tn + j))]
    out_specs = [pl.BlockSpec((tm, tn), lambda i, j: (i, j))]
    out_shape = [jax.ShapeDtypeStruct((m, REST_W), F32)]
    args = [x, g, w]
    hosted_tile = None
    if next_w_t is not None:
        n_tiles = PROJ_W // W_TILE_HOSTED
        assert (m // tm) * ncol >= n_tiles
        hosted_tile = lambda i, j: jnp.minimum(i * ncol + j, n_tiles - 1)
        src, dst = _w_tile_specs(next_layer, W_TILE_HOSTED, hosted_tile)
        in_specs.append(src)
        out_specs.append(dst)
        out_shape.append(jax.ShapeDtypeStruct((1, D_MODEL, PROJ_W), BF16))
        args.append(next_w_t)
    return pl.pallas_call(
        functools.partial(_proj_rest_kernel, n_silu_tiles=D_INNER // tn, hosted_tile=hosted_tile),
        grid=(m // tm, ncol),
        in_specs=in_specs,
        out_specs=out_specs,
        out_shape=out_shape,
        scratch_shapes=[pltpu.VMEM((tm, k), BF16)],
        compiler_params=_cparams("arbitrary", "arbitrary"),
        name="in_proj_rest",
    )(*args)


def _ffn_up_kernel(xn_ref, wg_ref, wu_ref, o_ref):
    xn = xn_ref[...]
    o_ref[...] = (_silu(_dot(xn, wg_ref[...])) * _dot(xn, wu_ref[...])).astype(BF16)


def _ffn_up(xn, wg, wu, layer, tm, tn):
    m, k = xn.shape
    n = wg.shape[2]
    return pl.pallas_call(
        _ffn_up_kernel,
        grid=(m // tm, n // tn),
        in_specs=[pl.BlockSpec((tm, k), lambda i, j: (i, 0)),
                  pl.BlockSpec((None, k, tn), lambda i, j: (layer, 0, j)),
                  pl.BlockSpec((None, k, tn), lambda i, j: (layer, 0, j))],
        out_specs=pl.BlockSpec((tm, tn), lambda i, j: (i, j)),
        out_shape=jax.ShapeDtypeStruct((m, n), BF16),
        compiler_params=_cparams("parallel", "parallel"),
        name="ffn_up",
    )(xn, wg, wu)


def _out_proj_kernel(a_ref, w_ref, r_ref, g_ref, o_ref, xn_ref):
    y = r_ref[...] + _dot(a_ref[...], w_ref[...])
    o_ref[...] = y
    xn_ref[...] = _rms_scale(y, g_ref[...]).astype(BF16)


def _out_proj(a, w, layer, res, gain, tm):
    m, k = a.shape
    n = w.shape[2]
    return pl.pallas_call(
        _out_proj_kernel,
        grid=(m // tm,),
        in_specs=[pl.BlockSpec((tm, k), lambda i: (i, 0)),
                  pl.BlockSpec((None, k, n), lambda i: (layer, 0, 0)),
                  pl.BlockSpec((tm, n), lambda i: (i, 0)),
                  pl.BlockSpec((1, n), lambda i: (0, 0))],
        out_specs=[pl.BlockSpec((tm, n), lambda i: (i, 0)),
                   pl.BlockSpec((tm, n), lambda i: (i, 0))],
        out_shape=[jax.ShapeDtypeStruct((m, n), F32),
                   jax.ShapeDtypeStruct((m, n), BF16)],
        compiler_params=_cparams("parallel"),
        name="out_proj",
    )(a, w, res, gain)


def _ffn_down_kernel(a_ref, w_ref, r_ref, g_ref, o_ref, *, final_norm):
    y = r_ref[...] + _dot(a_ref[...], w_ref[...])
    o_ref[...] = _rms_scale(y, g_ref[...]) if final_norm else y


def _ffn_down(a, w, layer, res, gain, final_norm, tm):
    m, k = a.shape
    n = w.shape[2]
    return pl.pallas_call(
        functools.partial(_ffn_down_kernel, final_norm=final_norm),
        grid=(m // tm,),
        in_specs=[pl.BlockSpec((tm, k), lambda i: (i, 0)),
                  pl.BlockSpec((None, k, n), lambda i: (layer, 0, 0), pipeline_mode=pl.Buffered(1)),
                  pl.BlockSpec((tm, n), lambda i: (i, 0)),
                  pl.BlockSpec((1, n), lambda i: (0, 0))],
        out_specs=pl.BlockSpec((tm, n), lambda i: (i, 0)),
        out_shape=jax.ShapeDtypeStruct((m, n), F32),
        compiler_params=_cparams("parallel"),
        name="ffn_down",
    )(a, w, res, gain)


def _merge_kernel(y_ref, o_ref, gs_ref, ga_ref, ws_ref, wa_ref, out_ref):
    ys = _dot(y_ref[...].astype(BF16), ws_ref[...])
    oa = _dot(o_ref[...].astype(BF16), wa_ref[...])
    out_ref[...] = (jax.nn.sigmoid(gs_ref[...]) * ys
                    + jax.nn.sigmoid(ga_ref[...]) * oa).astype(BF16)


def _merge(y, o, proj, col0, ws, wa, layer, tm, tn):
    m = y.shape[0]
    gs0, ga0 = (OFF_GS - col0) // tn, (OFF_GA - col0) // tn
    return pl.pallas_call(
        _merge_kernel,
        grid=(m // tm, D_MODEL // tn),
        in_specs=[pl.BlockSpec((tm, D_INNER), lambda i, j: (i, 0)),
                  pl.BlockSpec((tm, D_MODEL), lambda i, j: (i, 0)),
                  pl.BlockSpec((tm, tn), lambda i, j: (i, gs0 + j)),
                  pl.BlockSpec((tm, tn), lambda i, j: (i, ga0 + j)),
                  pl.BlockSpec((None, D_INNER, tn), lambda i, j: (layer, 0, j)),
                  pl.BlockSpec((None, D_MODEL, tn), lambda i, j: (layer, 0, j))],
        out_specs=pl.BlockSpec((tm, tn), lambda i, j: (i, j)),
        out_shape=jax.ShapeDtypeStruct((m, D_MODEL), BF16),
        compiler_params=_cparams("parallel", "parallel"),
        name="merge",
    )(y, o, proj, proj, ws, wa)


def _gated_norm(y, gate, gain):
    h = y * gate
    r = lax.rsqrt(jnp.mean(h * h, axis=-1, keepdims=True) + EPS)
    return h * r * gain


LOG2E = 1.4426950408889634


def _hosted_casts(arrays, steps):
    specs = [pl.BlockSpec((a.shape[0] // steps, a.shape[1]), lambda i: (i, 0)) for a in arrays]
    shapes = [jax.ShapeDtypeStruct(a.shape, BF16) for a in arrays]
    return specs, shapes


def _run_hosted_casts(src_refs, dst_refs):
    for src, dst in zip(src_refs, dst_refs):
        dst[...] = src[...].astype(BF16)


def _ssd_kernel(*refs, n_cast):
    xc_ref, zs_ref, dt_ref, dtb_ref, alog_ref, dsk_ref, gn_ref = refs[:7]
    y_ref, hout_ref = refs[7 + n_cast:9 + n_cast]
    ht_ref = refs[-1]
    _run_hosted_casts(refs[7:7 + n_cast], refs[9 + n_cast:9 + 2 * n_cast])
    ci = pl.program_id(0)

    @pl.when(ci == 0)
    def _():
        ht_ref[...] = jnp.zeros_like(ht_ref)

    dt = _softplus(dt_ref[...] + dtb_ref[...])
    a = dt * (-jnp.exp(alog_ref[...])) * LOG2E
    row = lax.broadcasted_iota(jnp.int32, (CHUNK, CHUNK), 0)
    col = lax.broadcasted_iota(jnp.int32, (CHUNK, CHUNK), 1)
    causal = col <= row
    tri = jnp.where(causal, 1.0, 0.0).astype(BF16)
    a_cs = _dot3_l(tri, a)
    a_cs_t = a_cs.T
    ea = jnp.exp2(a_cs)
    to_end = jnp.exp2(a_cs[CHUNK - 1:CHUNK, :] - a_cs)
    stack = jnp.concatenate([dt, to_end, ea], axis=0)

    lane = lax.broadcasted_iota(jnp.int32, (CHUNK, LANES), 1)
    low_half = lane < HEADDIM
    hrow = lax.broadcasted_iota(jnp.int32, (LANES, GROUP_W), 0)
    hcol = lax.broadcasted_iota(jnp.int32, (LANES, GROUP_W), 1)

    for g in range(N_GROUPS):
        sl = slice(g * GROUP_W, (g + 1) * GROUP_W)
        sel = jnp.where(hrow == g * (GROUP_W // HEADDIM) + _div_pow2(hcol, HEADDIM), 1.0, 0.0).astype(BF16)
        ex = _dot3_r(stack, sel)
        dt_e, te_e, ea_e = ex[0:CHUNK], ex[CHUNK:2 * CHUNK], ex[2 * CHUNK:3 * CHUNK]

        x_g = xc_ref[:, sl]
        b_g = xc_ref[:, D_INNER + g * D_STATE:D_INNER + (g + 1) * D_STATE]
        c_g = xc_ref[:, D_INNER + BC_W + g * D_STATE:D_INNER + BC_W + (g + 1) * D_STATE]
        xdt = x_g * dt_e
        xdt_b = xdt.astype(BF16)
        cb16, bb16 = c_g.astype(BF16), b_g.astype(BF16)
        cb = _dot_nt(cb16, bb16)

        ht_g = ht_ref[:, sl]
        y_g = _dot(cb16, ht_g.astype(BF16)) * ea_e + dsk_ref[:, sl] * x_g
        states_t = _dot(b_g.T.astype(BF16), (xdt * te_e).astype(BF16))
        ht_ref[:, sl] = ea_e[CHUNK - 1:CHUNK, :] * ht_g + states_t

        y_pairs = []
        for jj in range(GROUP_W // LANES):
            xp = xdt_b[:, jj * LANES:(jj + 1) * LANES]
            halves = []
            for hh in range(2):
                h = g * (GROUP_W // HEADDIM) + 2 * jj + hh
                seg = a_cs[:, h:h + 1] - a_cs_t[h:h + 1, :]
                decay = jnp.exp2(jnp.where(causal, seg, -jnp.inf))
                halves.append(_dot((cb * decay).astype(BF16), xp))
            y_pairs.append(jnp.where(low_half, halves[0], halves[1]))
        y_g = y_g + jnp.concatenate(y_pairs, axis=1)

        y_ref[:, sl] = _gated_norm(y_g, zs_ref[:, sl], gn_ref[:, sl]).astype(BF16)

    @pl.when(ci == pl.num_programs(0) - 1)
    def _():
        for i in range(D_INNER // LANES):
            hout_ref[i * LANES:(i + 1) * LANES, :] = ht_ref[:, i * LANES:(i + 1) * LANES].T


def _ssd_prompt(xc, rest, dt_raw, dtb, alog, dsk, gn, casts=()):
    nchunk = SEQ // CHUNK
    full = lambda r, c: pl.BlockSpec((r, c), lambda i: (0, 0))
    cast_specs, cast_shapes = _hosted_casts(casts, nchunk)
    return pl.pallas_call(
        functools.partial(_ssd_kernel, n_cast=len(casts)),
        grid=(nchunk,),
        in_specs=[pl.BlockSpec((CHUNK, CONV_DIM), lambda i: (i, 0)),
                  pl.BlockSpec((CHUNK, D_INNER), lambda i: (i, 0)),
                  pl.BlockSpec((CHUNK, LANES), lambda i: (i, 0)),
                  full(1, LANES), full(1, LANES), full(1, D_INNER), full(1, D_INNER)] + cast_specs,
        out_specs=[pl.BlockSpec((CHUNK, D_INNER), lambda i: (i, 0)),
                   pl.BlockSpec((D_INNER, D_STATE), lambda i: (0, 0))] + cast_specs,
        out_shape=[jax.ShapeDtypeStruct((SEQ, D_INNER), BF16),
                   jax.ShapeDtypeStruct((D_INNER, D_STATE), F32)] + cast_shapes,
        scratch_shapes=[pltpu.VMEM((D_STATE, D_INNER), F32)],
        compiler_params=_cparams("arbitrary"),
        name="ssd_prompt",
    )(xc, rest, dt_raw, dtb, alog, dsk, gn, *casts)


def _attn_kernel(*refs, n_cast):
    sink_ref, q_ref, kp_ref, kc_ref, vp_ref, vc_ref = refs[:6]
    o_ref = refs[6 + n_cast]
    _run_hosted_casts(refs[6:6 + n_cast], refs[7 + n_cast:7 + 2 * n_cast])
    i = pl.program_id(0)
    band = 2 * CHUNK
    l_idx = lax.broadcasted_iota(jnp.int32, (CHUNK, band), 0)
    s_idx = lax.broadcasted_iota(jnp.int32, (CHUNK, band), 1)
    first_key = jnp.where(i > 0, 0, CHUNK)
    mask = (s_idx >= jnp.maximum(l_idx, first_key)) & (s_idx <= l_idx + WINDOW)
    low_q = lax.broadcasted_iota(jnp.int32, (CHUNK, LANES), 1) < HEADDIM
    low_kv = lax.broadcasted_iota(jnp.int32, (band, LANES), 1) < HEADDIM

    for c in range(KV_DIM // LANES):
        csl = slice(c * LANES, (c + 1) * LANES)
        k2 = jnp.concatenate([kp_ref[:, csl], kc_ref[:, csl]], axis=0)
        v2 = jnp.concatenate([vp_ref[:, csl], vc_ref[:, csl]], axis=0)
        k2r = pltpu.roll(k2, HEADDIM, axis=1)
        v2r = pltpu.roll(v2, HEADDIM, axis=1)
        for u in range(2):
            kv_head = 2 * c + u
            kd = (jnp.where(low_kv, k2, k2r) if u == 0 else jnp.where(low_kv, k2r, k2)).astype(BF16)
            vd = (jnp.where(low_kv, v2, v2r) if u == 0 else jnp.where(low_kv, v2r, v2)).astype(BF16)
            for jj in range(Q_PER_KV // 2):
                j = kv_head * (Q_PER_KV // 2) + jj
                q2 = q_ref[:, j * LANES:(j + 1) * LANES] * (HEADDIM ** -0.5)
                halves = []
                for hh in range(2):
                    qm = jnp.where(low_q if hh == 0 else jnp.logical_not(low_q), q2, 0.0)
                    s = jnp.where(mask, _dot_nt(qm.astype(BF16), kd), -jnp.inf)
                    sink = sink_ref[2 * j + hh]
                    m = jnp.maximum(jnp.max(s, axis=-1, keepdims=True), sink)
                    p = jnp.exp(s - m)
                    den = jnp.sum(p, axis=-1, keepdims=True) + jnp.exp(sink - m)
                    halves.append(_dot(p.astype(BF16), vd) / den)
                o_ref[:, j * LANES:(j + 1) * LANES] = jnp.where(low_q, halves[0], halves[1]).astype(BF16)


def _attn_prompt(proj, col0, sinks, casts=()):
    nb = SEQ // CHUNK
    prev = lambda i: jnp.maximum(i - 1, 0)
    qb, kb, vb = (OFF_Q - col0) // D_MODEL, (OFF_K - col0) // KV_DIM, (OFF_V - col0) // KV_DIM
    cast_specs, cast_shapes = _hosted_casts(casts, nb)
    return pl.pallas_call(
        functools.partial(_attn_kernel, n_cast=len(casts)),
        grid=(nb,),
        in_specs=[pl.BlockSpec(memory_space=pltpu.SMEM),
                  pl.BlockSpec((CHUNK, D_MODEL), lambda i: (i, qb)),
                  pl.BlockSpec((CHUNK, KV_DIM), lambda i: (prev(i), kb)),
                  pl.BlockSpec((CHUNK, KV_DIM), lambda i: (i, kb)),
                  pl.BlockSpec((CHUNK, KV_DIM), lambda i: (prev(i), vb)),
                  pl.BlockSpec((CHUNK, KV_DIM), lambda i: (i, vb))] + cast_specs,
        out_specs=[pl.BlockSpec((CHUNK, D_MODEL), lambda i: (i, 0))] + cast_specs,
        out_shape=[jax.ShapeDtypeStruct((SEQ, D_MODEL), BF16)] + cast_shapes,
        compiler_params=_cparams("parallel"),
        name="attn_prompt",
    )(sinks, proj, proj, proj, proj, proj, *casts)


def _alias_prev(prev, n_in, first_out):
    if prev is None:
        return [], {}, ()
    specs = [pl.BlockSpec(memory_space=pl.ANY)] * len(prev)
    return specs, {n_in + t: first_out + t for t in range(len(prev))}, tuple(prev)


def _conv_step_kernel(x_ref, dt_ref, cs_ref, cw_ref, cb_ref, dtb_ref, alog_ref, *rest):
    xc_ref, csn_ref, dto_ref, dao_ref = rest[-4:]
    taps = [cs_ref[k] for k in range(CONV_W - 1)] + [x_ref[...]]
    acc = cb_ref[...]
    for k in range(CONV_W):
        acc = acc + taps[k] * cw_ref[k:k + 1, :]
    xc_ref[...] = _silu(acc)
    for k in range(CONV_W - 1):
        csn_ref[k] = taps[k + 1]
    dt = _softplus(dt_ref[...] + dtb_ref[...])
    dto_ref[...] = dt
    dao_ref[...] = jnp.exp(dt * (-jnp.exp(alog_ref[...])))


def _conv_step(proj, conv_state, layer, cw, cb, dtb, alog, prev):
    nb = DEC_BATCH
    small = pl.BlockSpec((1, LANES), lambda s: (0, 0))
    head = pl.BlockSpec((nb, LANES), lambda s: (0, 0))
    state = pl.BlockSpec((None, CONV_W - 1, nb, GROUP_W), lambda s: (layer, 0, 0, s))
    in_specs = [pl.BlockSpec((nb, GROUP_W), lambda s: (0, OFF_X // GROUP_W + s)),
                pl.BlockSpec((nb, LANES), lambda s: (0, OFF_DT // LANES)),
                state,
                pl.BlockSpec((CONV_W, GROUP_W), lambda s: (0, s)),
                pl.BlockSpec((1, GROUP_W), lambda s: (0, s)),
                small, small]
    alias_specs, aliases, alias_args = _alias_prev(prev, len(in_specs), 1)
    return pl.pallas_call(
        _conv_step_kernel,
        grid=(CONV_DIM // GROUP_W,),
        in_specs=in_specs + alias_specs,
        out_specs=[pl.BlockSpec((nb, GROUP_W), lambda s: (0, s)), state, head, head],
        out_shape=[jax.ShapeDtypeStruct((nb, CONV_DIM), F32),
                   jax.ShapeDtypeStruct(conv_state.shape, F32),
                   jax.ShapeDtypeStruct((nb, LANES), F32),
                   jax.ShapeDtypeStruct((nb, LANES), F32)],
        input_output_aliases=aliases,
        compiler_params=_cparams("arbitrary"),
        name="conv_step",
    )(proj, proj, conv_state, cw, cb, dtb, alog, *alias_args)


STEP_B = 8
STEP_GROUPS = 4
STEP_W = STEP_GROUPS * GROUP_W


def _ssm_step_kernel(xs_ref, b_ref, c_ref, z_ref, dt_ref, da_ref, dsk_ref, gn_ref, h_ref, *rest):
    y_ref, hn_ref = rest[-2:]
    part = pl.program_id(1)
    xs = xs_ref[...]
    hrow = lax.broadcasted_iota(jnp.int32, (LANES, STEP_W), 0)
    hcol = lax.broadcasted_iota(jnp.int32, (LANES, STEP_W), 1)
    sel = jnp.where(hrow == part * (STEP_W // HEADDIM) + _div_pow2(hcol, HEADDIM), 1.0, 0.0).astype(BF16)
    dt_e = _dot3_r(dt_ref[...], sel)
    da_e = _dot3_r(da_ref[...], sel)
    xdt = xs * dt_e
    lane_group = _div_pow2(lax.broadcasted_iota(jnp.int32, (STEP_B, STEP_W), 1), GROUP_W)

    pieces = [jnp.where(lane_group == g, xdt, 0.0) for g in range(STEP_GROUPS)]
    pieces += [p.astype(F32) for p in _split3(da_e)]
    n_used = len(pieces) * STEP_B
    pieces.append(jnp.zeros((LANES - n_used, STEP_W), F32))
    ut = jnp.concatenate(pieces, axis=0)
    u_big = jnp.concatenate([ut[:, i * LANES:(i + 1) * LANES].T for i in range(STEP_W // LANES)],
                            axis=0).astype(BF16)

    b_rows = jnp.concatenate([b_ref[:, g * D_STATE:(g + 1) * D_STATE] for g in range(STEP_GROUPS)]
                             + [jnp.zeros((LANES - STEP_GROUPS * STEP_B, D_STATE), F32)], axis=0)
    rrow = lax.broadcasted_iota(jnp.int32, (LANES, D_STATE), 0)
    ones_rows = jnp.where((rrow >= STEP_GROUPS * STEP_B) & (rrow < n_used), 1.0, 0.0)
    big_r = jnp.concatenate([b_rows, ones_rows], axis=1)
    r_seq = _mod_pow2(lax.broadcasted_iota(jnp.int32, (LANES, 2 * D_STATE), 0), STEP_B)

    cc = jnp.concatenate([c_ref[:, g * D_STATE:(g + 1) * D_STATE] for g in range(STEP_GROUPS)],
                         axis=0).astype(BF16)
    crow = lax.broadcasted_iota(jnp.int32, (STEP_GROUPS * STEP_B, STEP_W), 0)
    ccol_group = _div_pow2(lax.broadcasted_iota(jnp.int32, (STEP_GROUPS * STEP_B, STEP_W), 1), GROUP_W)
    sub = lax.broadcasted_iota(jnp.int32, (STEP_B, STEP_W), 0)

    y_off = jnp.zeros((STEP_B, STEP_W), F32)
    for b in range(STEP_B):
        h0 = h_ref[b]
        res = _dot(u_big, jnp.where(r_seq == b, big_r, 0.0).astype(BF16))
        hn_ref[b] = res[:, D_STATE:] * h0 + res[:, :D_STATE]
        y_all = _dot_nt(cc, h0.astype(BF16))
        pick = (crow == ccol_group * STEP_B + b)
        y_b = jnp.sum(jnp.where(pick, y_all, 0.0), axis=0, keepdims=True)
        y_off = jnp.where(sub == b, y_b, y_off)

    cb_e = jnp.zeros((STEP_B, STEP_W), F32)
    for g in range(STEP_GROUPS):
        gsl = slice(g * D_STATE, (g + 1) * D_STATE)
        cb_g = jnp.sum(c_ref[:, gsl] * b_ref[:, gsl], axis=-1, keepdims=True)
        cb_e = jnp.where(lane_group == g, cb_g, cb_e)

    y = da_e * y_off + cb_e * xdt + dsk_ref[...] * xs
    for g in range(STEP_GROUPS):
        sl = slice(g * GROUP_W, (g + 1) * GROUP_W)
        y_ref[:, sl] = _gated_norm(y[:, sl], _silu(z_ref[:, sl]), gn_ref[:, sl])


def _ssm_step(xc, proj, dt, da, dsk, gn, state, layer, prev):
    nparts = D_INNER // STEP_W
    state_spec = pl.BlockSpec((None, STEP_B, STEP_W, D_STATE), lambda i, p: (layer, i, p, 0))
    in_specs = [pl.BlockSpec((STEP_B, STEP_W), lambda i, p: (i, p)),
                pl.BlockSpec((STEP_B, STEP_GROUPS * D_STATE),
                             lambda i, p: (i, D_INNER // (STEP_GROUPS * D_STATE) + p)),
                pl.BlockSpec((STEP_B, STEP_GROUPS * D_STATE),
                             lambda i, p: (i, (D_INNER + BC_W) // (STEP_GROUPS * D_STATE) + p)),
                pl.BlockSpec((STEP_B, STEP_W), lambda i, p: (i, OFF_Z // STEP_W + p)),
                pl.BlockSpec((STEP_B, LANES), lambda i, p: (i, 0)),
                pl.BlockSpec((STEP_B, LANES), lambda i, p: (i, 0)),
                pl.BlockSpec((1, STEP_W), lambda i, p: (0, p)),
                pl.BlockSpec((1, STEP_W), lambda i, p: (0, p)),
                state_spec]
    alias_specs, aliases, alias_args = _alias_prev(prev, len(in_specs), 1)
    return pl.pallas_call(
        _ssm_step_kernel,
        grid=(DEC_BATCH // STEP_B, nparts),
        in_specs=in_specs + alias_specs,
        out_specs=[pl.BlockSpec((STEP_B, STEP_W), lambda i, p: (i, p)), state_spec],
        out_shape=[jax.ShapeDtypeStruct((DEC_BATCH, D_INNER), F32),
                   jax.ShapeDtypeStruct(state.shape, F32)],
        input_output_aliases=aliases,
        compiler_params=_cparams("parallel", "parallel"),
        name="ssm_step",
    )(xc, xc, xc, proj, dt, da, dsk, gn, state, *alias_args)


ATTN_B = 8


def _attn_step_kernel(sink_ref, q_ref, kn_ref, vn_ref, kc_ref, vc_ref, *rest):
    o_ref, ko_ref, vo_ref = rest[-3:]
    lane_kv = _div_pow2(lax.broadcasted_iota(jnp.int32, (N_KV, KV_DIM), 1), HEADDIM)
    own = lane_kv == lax.broadcasted_iota(jnp.int32, (N_KV, KV_DIM), 0)
    newest = lax.broadcasted_iota(jnp.int32, (KV_DIM, WINDOW), 1) == WINDOW - 1
    sink = sink_ref[...]
    pad = jnp.zeros((LANES - ATTN_B, KV_DIM), F32)
    kn_t = jnp.concatenate([kn_ref[...], pad], axis=0).T
    vn_t = jnp.concatenate([vn_ref[...], pad], axis=0).T
    for b in range(ATTN_B):
        qrow = q_ref[b:b + 1, :] * (HEADDIM ** -0.5)
        qm = jnp.concatenate(
            [jnp.where(own, jnp.broadcast_to(qrow[:, r * KV_DIM:(r + 1) * KV_DIM], (N_KV, KV_DIM)), 0.0)
             for r in range(Q_PER_KV)], axis=0)
        k_new = kn_ref[b:b + 1, :]
        v_new = vn_ref[b:b + 1, :]
        kt = kc_ref[b]
        vt = vc_ref[b]
        s = _dot(qm.astype(BF16), kt.astype(BF16))
        s_new = jnp.sum(qm * k_new, axis=-1, keepdims=True)
        m = jnp.maximum(jnp.maximum(jnp.max(s, axis=-1, keepdims=True), s_new), sink)
        p = jnp.exp(s - m)
        p_new = jnp.exp(s_new - m)
        den = jnp.sum(p, axis=-1, keepdims=True) + p_new + jnp.exp(sink - m)
        o_full = _dot_nt((p / den).astype(BF16), vt.astype(BF16)) + (p_new / den) * v_new
        parts = []
        for r in range(Q_PER_KV):
            o_r = jnp.where(own, o_full[r * N_KV:(r + 1) * N_KV, :], 0.0)
            parts.append(jnp.sum(o_r, axis=0, keepdims=True))
        o_ref[b:b + 1, :] = jnp.concatenate(parts, axis=1)
        ko_ref[b] = jnp.where(newest, jnp.broadcast_to(kn_t[:, b:b + 1], (KV_DIM, WINDOW)),
                              pltpu.roll(kt, WINDOW - 1, axis=1))
        vo_ref[b] = jnp.where(newest, jnp.broadcast_to(vn_t[:, b:b + 1], (KV_DIM, WINDOW)),
                              pltpu.roll(vt, WINDOW - 1, axis=1))


def _attn_step(q, proj, sinks, cache_k, cache_v, layer, prev):
    cache_spec = pl.BlockSpec((None, ATTN_B, KV_DIM, WINDOW), lambda i: (layer, i, 0, 0))
    in_specs = [pl.BlockSpec((N_HEADS, 1), lambda i: (0, 0)),
                pl.BlockSpec((ATTN_B, D_MODEL), lambda i: (i, 0)),
                pl.BlockSpec((ATTN_B, KV_DIM), lambda i: (i, OFF_K // KV_DIM)),
                pl.BlockSpec((ATTN_B, KV_DIM), lambda i: (i, OFF_V // KV_DIM)),
                cache_spec, cache_spec]
    alias_specs, aliases, alias_args = _alias_prev(prev, len(in_specs), 1)
    return pl.pallas_call(
        _attn_step_kernel,
        grid=(DEC_BATCH // ATTN_B,),
        in_specs=in_specs + alias_specs,
        out_specs=[pl.BlockSpec((ATTN_B, D_MODEL), lambda i: (i, 0)), cache_spec, cache_spec],
        out_shape=[jax.ShapeDtypeStruct((DEC_BATCH, D_MODEL), F32),
                   jax.ShapeDtypeStruct(cache_k.shape, F32),
                   jax.ShapeDtypeStruct(cache_v.shape, F32)],
        input_output_aliases=aliases,
        compiler_params=_cparams("parallel"),
        name="attn_step",
    )(sinks, q, proj, proj, cache_k, cache_v, *alias_args)


SRC_X = D_INNER
SRC_DT = SRC_X + CONV_DIM
W_TILE = 512
W_TILE_HOSTED = 256


def _w_tile_src_row(t, tile_w):
    per_tile = tile_w // N_SSM_HEADS
    unit = jnp.where(
        t < OFF_Z // tile_w, SRC_X // N_SSM_HEADS + t * per_tile,
        jnp.where(t < OFF_Q // tile_w, (t - OFF_Z // tile_w) * per_tile,
                  jnp.where(t < OFF_DT // tile_w, t * per_tile + 1, SRC_DT // N_SSM_HEADS)))
    return unit * N_SSM_HEADS


def _w_tile_relayout(x, t, tile_w):
    first_pad = OFF_DT // tile_w
    valid = jnp.where(t < first_pad, tile_w, jnp.where(t == first_pad, N_SSM_HEADS, 0))
    row = lax.broadcasted_iota(jnp.int32, x.shape, 0)
    return jnp.where(row < valid, x, 0.0).T.astype(BF16)


def _w_tile_specs(layer, tile_w, tile_of):
    src = pl.BlockSpec((pl.Element(1), pl.Element(tile_w), pl.Element(D_MODEL)),
                       lambda *ids: (layer, _w_tile_src_row(tile_of(*ids), tile_w), 0))
    dst = pl.BlockSpec((None, D_MODEL, tile_w), lambda *ids: (0, 0, tile_of(*ids)))
    return src, dst


def _w_in_layout_kernel(x_ref, o_ref):
    o_ref[...] = _w_tile_relayout(x_ref[0], pl.program_id(0), W_TILE)


def _w_in_layout(w_t, layer):
    src, dst = _w_tile_specs(layer, W_TILE, lambda t: t)
    return pl.pallas_call(
        _w_in_layout_kernel,
        grid=(PROJ_W // W_TILE,),
        in_specs=[src],
        out_specs=dst,
        out_shape=jax.ShapeDtypeStruct((1, D_MODEL, PROJ_W), BF16),
        compiler_params=_cparams("parallel"),
        name="w_in_layout",
    )(w_t)


def _small_params(l, norm_mix, conv_w, conv_b, dt_bias, a_log, d_skip, ssm_norm, attn_sinks, norm_ffn):
    pad = lambda v: jnp.pad(v, (0, LANES - N_SSM_HEADS)).reshape(1, LANES)
    return dict(
        norm_mix=norm_mix[l].reshape(1, D_MODEL),
        conv_w=conv_w[l], conv_b=conv_b[l].reshape(1, CONV_DIM),
        dt_bias=pad(dt_bias[l]), a_log=pad(a_log[l]),
        d_skip=jnp.repeat(d_skip[l], HEADDIM).reshape(1, D_INNER),
        ssm_norm=ssm_norm[l].reshape(1, D_INNER),
        sinks=attn_sinks[l],
        sinks_rm=attn_sinks[l].reshape(N_KV, Q_PER_KV).T.reshape(N_HEADS, 1),
        norm_ffn=norm_ffn[l].reshape(1, D_MODEL))


def _dense_tail(x, y, o, proj, col0, p, w, l, tm, norm_final):
    merged = _merge(y, o, proj, col0, w["w_ssm"], w["w_attn"], l, tm, 512)
    x1, xn = _out_proj(merged, w["w_out"], l, x, p["norm_ffn"], min(tm, 512))
    hmid = _ffn_up(xn, w["w_gate"], w["w_up"], l, tm, 512)
    return _ffn_down(hmid, w["w_down"], l, x1, norm_final, l == DEPTH - 1, min(tm, 512))


PROMPT_TM = 1024
PROJ_TN = 1280


def kernel(x_prompt, x_sample, state_ssm, state_conv, cache_k, cache_v, norm_mix, w_in, conv_w,
           conv_b, dt_bias, a_log, d_skip, ssm_norm, w_ssm_proj, w_attn_proj, attn_sinks, w_out,
           norm_ffn, w_gate, w_up, w_down, norm_final):
    xp = x_prompt.reshape(SEQ, D_MODEL)
    xs = x_sample.reshape(DEC_BATCH, D_MODEL)
    state4 = state_ssm.reshape(DEPTH, DEC_BATCH, D_INNER, D_STATE)
    to_kt = lambda c: c.transpose(0, 1, 3, 4, 2).reshape(DEPTH, DEC_BATCH, KV_DIM, WINDOW)
    ck, cv = to_kt(cache_k), to_kt(cache_v)
    conv_t = state_conv.transpose(0, 2, 1, 3)
    outs = {k: [] for k in ("p_ssm", "p_conv", "p_k", "p_v")}

    w_in_t = w_in.transpose(0, 2, 1)
    w = {}
    w_in_l = _w_in_layout(w_in_t, 0)
    rows =lambda t: t.reshape(-1, t.shape[-1])
    ssd_casts = dict(w_ssm=w_ssm_proj, w_attn=w_attn_proj, w_out=w_out, w_down=w_down)
    attn_casts = dict(w_gate=w_gate, w_up=w_up)

    g_final = norm_final.reshape(1, D_MODEL)
    h_all = conv_all = kv_all = None
    for l in range(DEPTH):
        p = _small_params(l, norm_mix, conv_w, conv_b, dt_bias, a_log, d_skip, ssm_norm,
                          attn_sinks, norm_ffn)

        xc_p, dt_p, tails = _in_proj_conv(xp, p["norm_mix"], w_in_l, 0, p["conv_w"], p["conv_b"],
                                          PROMPT_TM, 1024)
        if l + 1 < DEPTH:
            rest, w_in_next = _in_proj_rest(xp, p["norm_mix"], w_in_l, 0, PROMPT_TM, 1024, w_in_t, l + 1)
        else:
            (rest,), w_in_next = _in_proj_rest(xp, p["norm_mix"], w_in_l, 0, PROMPT_TM, 1024), None
        hosted = [ssd_casts, attn_casts] if l == 0 else [{}, {}]
        y, h_p, *cast_a = _ssd_prompt(xc_p, rest, dt_p, p["dt_bias"], p["a_log"], p["d_skip"],
                                      p["ssm_norm"], [rows(t) for t in hosted[0].values()])
        o, *cast_b = _attn_prompt(rest, OFF_Z, p["sinks"], [rows(t) for t in hosted[1].values()])
        for group, done in zip(hosted, (cast_a, cast_b)):
            w.update({k: c.reshape(t.shape) for (k, t), c in zip(group.items(), done)})
        last_kv = lambda off: rest[SEQ - WINDOW:, off - OFF_Z:off - OFF_Z + KV_DIM].reshape(
            1, WINDOW, N_KV, HEADDIM)
        outs["p_ssm"].append(h_p.reshape(1, N_SSM_HEADS, HEADDIM, D_STATE))
        outs["p_conv"].append(tails[-1, SUBLANES - (CONV_W - 1):].reshape(1, CONV_W - 1, CONV_DIM))
        outs["p_k"].append(last_kv(OFF_K))
        outs["p_v"].append(last_kv(OFF_V))
        xp = _dense_tail(xp, y, o, rest, OFF_Z, p, w, l, PROMPT_TM, g_final)

        proj_s = _in_proj(xs, p["norm_mix"], w_in_l, 0, DEC_BATCH, PROJ_TN)
        xc, conv_all, dt, da = _conv_step(proj_s, conv_t, l, p["conv_w"], p["conv_b"],
                                          p["dt_bias"], p["a_log"],
                                          None if conv_all is None else (conv_all,))
        y_s, h_all = _ssm_step(xc, proj_s, dt, da, p["d_skip"], p["ssm_norm"], state4, l,
                               None if h_all is None else (h_all,))
        swap_rm = lambda t, a, b: t.reshape(DEC_BATCH, a, b, HEADDIM).transpose(0, 2, 1, 3).reshape(
            DEC_BATCH, D_MODEL)
        q_s = swap_rm(proj_s[:, OFF_Q:OFF_Q + D_MODEL], N_KV, Q_PER_KV)
        o_s, k_all, v_all = _attn_step(q_s, proj_s, p["sinks_rm"], ck, cv, l, kv_all)
        o_s = swap_rm(o_s, Q_PER_KV, N_KV)
        kv_all = (k_all, v_all)
        xs = _dense_tail(xs, y_s, o_s, proj_s, 0, p, w, l, DEC_BATCH, g_final)
        w_in_l = w_in_next

    y_prompt = xp.reshape(1, SEQ, D_MODEL)
    y_sample = xs.reshape(DEC_BATCH, 1, D_MODEL)
    st = lambda k: jnp.stack(outs[k])
    from_kt = lambda c: c.reshape(DEPTH, DEC_BATCH, N_KV, HEADDIM, WINDOW).transpose(0, 1, 4, 2, 3)
    return (y_prompt, y_sample, st("p_ssm"), st("p_conv"), st("p_k"), st("p_v"),
            h_all.reshape(DEPTH, DEC_BATCH, N_SSM_HEADS, HEADDIM, D_STATE),
            conv_all.transpose(0, 2, 1, 3), from_kt(kv_all[0]), from_kt(kv_all[1]))
```

```python
import functools

import jax
import jax.numpy as jnp
from jax import lax
from jax.experimental import pallas as pl
from jax.experimental.pallas import tpu as pltpu

F32 = jnp.float32
BF16 = jnp.bfloat16

D_MODEL = 2048
SEQ = 8192
DEPTH = 2
DEC_BATCH = 128
D_INNER = 4096
HEADDIM = 64
N_SSM_HEADS = 64
D_STATE = 128
N_GROUPS = 8
GROUP_W = D_INNER // N_GROUPS
CONV_W = 4
BC_W = N_GROUPS * D_STATE
CONV_DIM = D_INNER + 2 * BC_W
CHUNK = 128
N_HEADS = 32
N_KV = 8
Q_PER_KV = 4
KV_DIM = N_KV * HEADDIM
WINDOW = 128
D_FF = 5632
EPS = 1e-6
IN_DIM = 17472

OFF_X = 0
OFF_B = OFF_X + D_INNER
OFF_C = OFF_B + BC_W
OFF_Z = CONV_DIM
OFF_Q = 10240
OFF_K = 12288
OFF_V = 12800
OFF_GS = 13312
OFF_GA = 15360
OFF_DT = 17408
PROJ_W = 17920

LANES = 128
SUBLANES = 8
VMEM_LIMIT = 56 * 1024 * 1024


def _cparams(*sem):
    return pltpu.CompilerParams(dimension_semantics=sem, vmem_limit_bytes=VMEM_LIMIT)


def _dot(a, b):
    return jnp.dot(a, b, preferred_element_type=F32)


def _dot_nt(a, b):
    return lax.dot_general(a, b, (((1,), (1,)), ((), ())), preferred_element_type=F32)


def _div_pow2(v, d):
    return jnp.right_shift(v, d.bit_length() - 1)


def _mod_pow2(v, d):
    return jnp.bitwise_and(v, d - 1)


def _split3(x):
    hi = x.astype(BF16)
    r1 = x - hi.astype(F32)
    mid = r1.astype(BF16)
    lo = (r1 - mid.astype(F32)).astype(BF16)
    return hi, mid, lo


def _dot3_r(x, sel):
    return _dot(jnp.concatenate(_split3(x), axis=1), jnp.concatenate([sel, sel, sel], axis=0))


def _dot3_l(sel, x):
    return _dot(jnp.concatenate([sel, sel, sel], axis=1), jnp.concatenate(_split3(x), axis=0))


def _silu(x):
    h = 0.5 * x
    return h + h * jnp.tanh(h)


def _softplus(x):
    return jnp.maximum(x, 0.0) + jnp.log1p(jnp.exp(-jnp.abs(x)))


def _rms_scale(x, g):
    r = lax.rsqrt(jnp.mean(x * x, axis=-1, keepdims=True) + EPS)
    return x * r * g


def _norm_matmul_kernel(x_ref, g_ref, w_ref, o_ref, xn_ref):
    @pl.when(pl.program_id(1) == 0)
    def _():
        xn_ref[...] = _rms_scale(x_ref[...], g_ref[...]).astype(BF16)

    o_ref[...] = _dot(xn_ref[...], w_ref[...])


def _in_proj(x, g, w, layer, tm, tn):
    m, k = x.shape
    n = w.shape[2]
    return pl.pallas_call(
        _norm_matmul_kernel,
        grid=(m // tm, n // tn),
        in_specs=[pl.BlockSpec((tm, k), lambda i, j: (i, 0)),
                  pl.BlockSpec((1, k), lambda i, j: (0, 0)),
                  pl.BlockSpec((None, k, tn), lambda i, j: (layer, 0, j))],
        out_specs=pl.BlockSpec((tm, tn), lambda i, j: (i, j)),
        out_shape=jax.ShapeDtypeStruct((m, n), F32),
        scratch_shapes=[pltpu.VMEM((tm, k), BF16)],
        compiler_params=_cparams("parallel", "arbitrary"),
        name="in_proj",
    )(x, g, w)


def _normed_bf16(x_ref, g_ref):
    if x_ref.dtype == BF16:
        return x_ref[...]
    return _rms_scale(x_ref[...], g_ref[...]).astype(BF16)


MXU_N = 256
REST_W = OFF_DT - OFF_Z
CONV_ROWS = 64


def _proj_conv_kernel(x_ref, g_ref, w_ref, wdt_ref, cw_ref, cb_ref, xc_ref, dt_ref, tail_ref,
                      xn_ref, xpad_ref, carry_ref):
    i = pl.program_id(0)
    j = pl.program_id(1)
    tm, tn = xc_ref.shape

    @pl.when(j == 0)
    def _():
        xn = _normed_bf16(x_ref, g_ref)
        xn_ref[...] = xn
        dt_ref[...] = _dot(xn, wdt_ref[...])

    @pl.when(i == 0)
    def _():
        carry_ref[j] = jnp.zeros((SUBLANES, tn), F32)

    xpad_ref[0:SUBLANES, :] = carry_ref[j]
    for cs in range(tn // MXU_N):
        sl = slice(cs * MXU_N, (cs + 1) * MXU_N)
        xpad_ref[SUBLANES:SUBLANES + tm, sl] = _dot(xn_ref[...], w_ref[:, sl])
        for c0 in range(cs * MXU_N, (cs + 1) * MXU_N, LANES):
            cl = slice(c0, c0 + LANES)
            for r0 in range(0, tm, CONV_ROWS):
                xp = xpad_ref[r0:r0 + SUBLANES + CONV_ROWS, cl]
                acc = cb_ref[:, cl] + xp[SUBLANES:] * cw_ref[CONV_W - 1:CONV_W, cl]
                for back in range(1, CONV_W):
                    k = CONV_W - 1 - back
                    acc = acc + pltpu.roll(xp, back, axis=0)[SUBLANES:] * cw_ref[k:k + 1, cl]
                xc_ref[r0:r0 + CONV_ROWS, cl] = _silu(acc)
    tail = xpad_ref[tm:tm + SUBLANES, :]
    carry_ref[j] = tail
    tail_ref[...] = tail


def _in_proj_conv(x, g, w, layer, cw, cb, tm, tn):
    m, k = x.shape
    ntile = CONV_DIM // tn
    return pl.pallas_call(
        _proj_conv_kernel,
        grid=(m // tm, ntile),
        in_specs=[pl.BlockSpec((tm, k), lambda i, j: (i, 0)),
                  pl.BlockSpec((1, k), lambda i, j: (0, 0)),
                  pl.BlockSpec((None, k, tn), lambda i, j: (layer, 0, OFF_X // tn + j)),
                  pl.BlockSpec((None, k, LANES), lambda i, j: (layer, 0, OFF_DT // LANES)),
                  pl.BlockSpec((CONV_W, tn), lambda i, j: (0, j)),
                  pl.BlockSpec((1, tn), lambda i, j: (0, j))],
        out_specs=[pl.BlockSpec((tm, tn), lambda i, j: (i, j)),
                   pl.BlockSpec((tm, LANES), lambda i, j: (i, 0)),
                   pl.BlockSpec((None, SUBLANES, tn), lambda i, j: (i, 0, j))],
        out_shape=[jax.ShapeDtypeStruct((m, CONV_DIM), F32),
                   jax.ShapeDtypeStruct((m, LANES), F32),
                   jax.ShapeDtypeStruct((m // tm, SUBLANES, CONV_DIM), F32)],
        scratch_shapes=[pltpu.VMEM((tm, k), BF16),
                        pltpu.VMEM((tm + SUBLANES, tn), F32),
                        pltpu.VMEM((ntile, SUBLANES, tn), F32)],
        compiler_params=_cparams("arbitrary", "arbitrary"),
        name="in_proj_conv",
    )(x, g, w, w, cw, cb)


def _proj_rest_kernel(*refs, n_silu_tiles, hosted_tile):
    if hosted_tile is None:
        x_ref, g_ref, w_ref, o_ref, xn_ref = refs
    else:
        x_ref, g_ref, w_ref, wsrc_ref, o_ref, wdst_ref, xn_ref = refs
    j = pl.program_id(1)
    tn = o_ref.shape[1]

    def relayout():
        if hosted_tile is not None:
            t = hosted_tile(pl.program_id(0), j)
            wdst_ref[...] = _w_tile_relayout(wsrc_ref[0], t, W_TILE_HOSTED)

    @pl.when(j == 0)
    def _():
        xn_ref[...] = _normed_bf16(x_ref, g_ref)

    @pl.when(j < n_silu_tiles)
    def _():
        relayout()
        for cs in range(tn // MXU_N):
            sl = slice(cs * MXU_N, (cs + 1) * MXU_N)
            o_ref[:, sl] = _silu(_dot(xn_ref[...], w_ref[:, sl]))

    @pl.when(j >= n_silu_tiles)
    def _():
        relayout()
        o_ref[...] = _dot(xn_ref[...], w_ref[...])


def _in_proj_rest(x, g, w, layer, tm, tn, next_w_t=None, next_layer=None):
    m, k = x.shape
    ncol = REST_W // tn
    in_specs = [pl.BlockSpec((tm, k), lambda i, j: (i, 0)),
                pl.BlockSpec((1, k), lambda i, j: (0, 0)),
                pl.BlockSpec((None, k, tn), lambda i, j: (layer, 0, OFF_Z // tn + j))]
    out_specs = [pl.BlockSpec((tm, tn), lambda i, j: (i, j))]
    out_shape = [jax.ShapeDtypeStruct((m, REST_W), F32)]
    args = [x, g, w]
    hosted_tile = None
    if next_w_t is not None:
        n_tiles = PROJ_W // W_TILE_HOSTED
        assert (m // tm) * ncol >= n_tiles
        hosted_tile = lambda i, j: jnp.minimum(i * ncol + j, n_tiles - 1)
        src, dst = _w_tile_specs(next_layer, W_TILE_HOSTED, hosted_tile)
        in_specs.append(src)
        out_specs.append(dst)
        out_shape.append(jax.ShapeDtypeStruct((1, D_MODEL, PROJ_W), BF16))
        args.append(next_w_t)
    return pl.pallas_call(
        functools.partial(_proj_rest_kernel, n_silu_tiles=D_INNER // tn, hosted_tile=hosted_tile),
        grid=(m // tm, ncol),
        in_specs=in_specs,
        out_specs=out_specs,
        out_shape=out_shape,
        scratch_shapes=[pltpu.VMEM((tm, k), BF16)],
        compiler_params=_cparams("arbitrary", "arbitrary"),
        name="in_proj_rest",
    )(*args)


def _ffn_up_kernel(xn_ref, wg_ref, wu_ref, o_ref):
    xn = xn_ref[...]
    o_ref[...] = (_silu(_dot(xn, wg_ref[...])) * _dot(xn, wu_ref[...])).astype(BF16)


def _ffn_up(xn, wg, wu, layer, tm, tn):
    m, k = xn.shape
    n = wg.shape[2]
    return pl.pallas_call(
        _ffn_up_kernel,
        grid=(m // tm, n // tn),
        in_specs=[pl.BlockSpec((tm, k), lambda i, j: (i, 0)),
                  pl.BlockSpec((None, k, tn), lambda i, j: (layer, 0, j)),
                  pl.BlockSpec((None, k, tn), lambda i, j: (layer, 0, j))],
        out_specs=pl.BlockSpec((tm, tn), lambda i, j: (i, j)),
        out_shape=jax.ShapeDtypeStruct((m, n), BF16),
        compiler_params=_cparams("parallel", "parallel"),
        name="ffn_up",
    )(xn, wg, wu)


def _out_proj_kernel(a_ref, w_ref, r_ref, g_ref, o_ref, xn_ref):
    y = r_ref[...] + _dot(a_ref[...], w_ref[...])
    o_ref[...] = y
    xn_ref[...] = _rms_scale(y, g_ref[...]).astype(BF16)


def _out_proj(a, w, layer, res, gain, tm):
    m, k = a.shape
    n = w.shape[2]
    return pl.pallas_call(
        _out_proj_kernel,
        grid=(m // tm,),
        in_specs=[pl.BlockSpec((tm, k), lambda i: (i, 0)),
                  pl.BlockSpec((None, k, n), lambda i: (layer, 0, 0)),
                  pl.BlockSpec((tm, n), lambda i: (i, 0)),
                  pl.BlockSpec((1, n), lambda i: (0, 0))],
        out_specs=[pl.BlockSpec((tm, n), lambda i: (i, 0)),
                   pl.BlockSpec((tm, n), lambda i: (i, 0))],
        out_shape=[jax.ShapeDtypeStruct((m, n), F32),
                   jax.ShapeDtypeStruct((m, n), BF16)],
        compiler_params=_cparams("parallel"),
        name="out_proj",
    )(a, w, res, gain)


def _ffn_down_kernel(a_ref, w_ref, r_ref, g_ref, o_ref, *xn_ref, norm):
    y = r_ref[...] + _dot(a_ref[...], w_ref[...])
    o_ref[...] = _rms_scale(y, g_ref[...]) if norm == "final" else y
    if norm == "next":
        xn_ref[0][...] = _rms_scale(y, g_ref[...]).astype(BF16)


def _ffn_down(a, w, layer, res, gain, norm, tm):
    m, k = a.shape
    n = w.shape[2]
    row_spec = pl.BlockSpec((tm, n), lambda i: (i, 0))
    extra = norm == "next"
    return pl.pallas_call(
        functools.partial(_ffn_down_kernel, norm=norm),
        grid=(m // tm,),
        in_specs=[pl.BlockSpec((tm, k), lambda i: (i, 0)),
                  pl.BlockSpec((None, k, n), lambda i: (layer, 0, 0), pipeline_mode=pl.Buffered(1)),
                  row_spec,
                  pl.BlockSpec((1, n), lambda i: (0, 0))],
        out_specs=[row_spec] + [row_spec] * extra,
        out_shape=[jax.ShapeDtypeStruct((m, n), F32)] + [jax.ShapeDtypeStruct((m, n), BF16)] * extra,
        compiler_params=_cparams("parallel"),
        name="ffn_down",
    )(a, w, res, gain)


def _merge_kernel(y_ref, o_ref, gs_ref, ga_ref, ws_ref, wa_ref, out_ref):
    ys = _dot(y_ref[...].astype(BF16), ws_ref[...])
    oa = _dot(o_ref[...].astype(BF16), wa_ref[...])
    out_ref[...] = (jax.nn.sigmoid(gs_ref[...]) * ys
                    + jax.nn.sigmoid(ga_ref[...]) * oa).astype(BF16)


def _merge(y, o, proj, col0, ws, wa, layer, tm, tn):
    m = y.shape[0]
    gs0, ga0 = (OFF_GS - col0) // tn, (OFF_GA - col0) // tn
    return pl.pallas_call(
        _merge_kernel,
        grid=(m // tm, D_MODEL // tn),
        in_specs=[pl.BlockSpec((tm, D_INNER), lambda i, j: (i, 0)),
                  pl.BlockSpec((tm, D_MODEL), lambda i, j: (i, 0)),
                  pl.BlockSpec((tm, tn), lambda i, j: (i, gs0 + j)),
                  pl.BlockSpec((tm, tn), lambda i, j: (i, ga0 + j)),
                  pl.BlockSpec((None, D_INNER, tn), lambda i, j: (layer, 0, j)),
                  pl.BlockSpec((None, D_MODEL, tn), lambda i, j: (layer, 0, j))],
        out_specs=pl.BlockSpec((tm, tn), lambda i, j: (i, j)),
        out_shape=jax.ShapeDtypeStruct((m, D_MODEL), BF16),
        compiler_params=_cparams("parallel", "parallel"),
        name="merge",
    )(y, o, proj, proj, ws, wa)


def _gated_norm(y, gate, gain):
    h = y * gate
    r = lax.rsqrt(jnp.mean(h * h, axis=-1, keepdims=True) + EPS)
    return h * r * gain


LOG2E = 1.4426950408889634


def _hosted_casts(arrays, steps):
    specs = [pl.BlockSpec((a.shape[0] // steps, a.shape[1]), lambda i: (i, 0)) for a in arrays]
    shapes = [jax.ShapeDtypeStruct(a.shape, BF16) for a in arrays]
    return specs, shapes


def _run_hosted_casts(src_refs, dst_refs):
    for src, dst in zip(src_refs, dst_refs):
        dst[...] = src[...].astype(BF16)


def _ssd_kernel(*refs, n_cast):
    xc_ref, zs_ref, dt_ref, dtb_ref, alog_ref, dsk_ref, gn_ref = refs[:7]
    y_ref, hout_ref = refs[7 + n_cast:9 + n_cast]
    ht_ref = refs[-1]
    _run_hosted_casts(refs[7:7 + n_cast], refs[9 + n_cast:9 + 2 * n_cast])
    ci = pl.program_id(0)

    @pl.when(ci == 0)
    def _():
        ht_ref[...] = jnp.zeros_like(ht_ref)

    dt = _softplus(dt_ref[...] + dtb_ref[...])
    a = dt * (-jnp.exp(alog_ref[...])) * LOG2E
    row = lax.broadcasted_iota(jnp.int32, (CHUNK, CHUNK), 0)
    col = lax.broadcasted_iota(jnp.int32, (CHUNK, CHUNK), 1)
    causal = col <= row
    tri = jnp.where(causal, 1.0, 0.0).astype(BF16)
    a_cs = _dot3_l(tri, a)
    a_cs_t = a_cs.T
    ea = jnp.exp2(a_cs)
    to_end = jnp.exp2(a_cs[CHUNK - 1:CHUNK, :] - a_cs)
    stack = jnp.concatenate([dt, to_end, ea], axis=0)

    lane = lax.broadcasted_iota(jnp.int32, (CHUNK, LANES), 1)
    low_half = lane < HEADDIM
    hrow = lax.broadcasted_iota(jnp.int32, (LANES, GROUP_W), 0)
    hcol = lax.broadcasted_iota(jnp.int32, (LANES, GROUP_W), 1)

    for g in range(N_GROUPS):
        sl = slice(g * GROUP_W, (g + 1) * GROUP_W)
        sel = jnp.where(hrow == g * (GROUP_W // HEADDIM) + _div_pow2(hcol, HEADDIM), 1.0, 0.0).astype(BF16)
        ex = _dot3_r(stack, sel)
        dt_e, te_e, ea_e = ex[0:CHUNK], ex[CHUNK:2 * CHUNK], ex[2 * CHUNK:3 * CHUNK]

        x_g = xc_ref[:, sl]
        b_g = xc_ref[:, D_INNER + g * D_STATE:D_INNER + (g + 1) * D_STATE]
        c_g = xc_ref[:, D_INNER + BC_W + g * D_STATE:D_INNER + BC_W + (g + 1) * D_STATE]
        xdt = x_g * dt_e
        xdt_b = xdt.astype(BF16)
        cb16, bb16 = c_g.astype(BF16), b_g.astype(BF16)
        cb = _dot_nt(cb16, bb16)

        ht_g = ht_ref[:, sl]
        y_g = _dot(cb16, ht_g.astype(BF16)) * ea_e + dsk_ref[:, sl] * x_g
        states_t = _dot(b_g.T.astype(BF16), (xdt * te_e).astype(BF16))
        ht_ref[:, sl] = ea_e[CHUNK - 1:CHUNK, :] * ht_g + states_t

        y_pairs = []
        for jj in range(GROUP_W // LANES):
            xp = xdt_b[:, jj * LANES:(jj + 1) * LANES]
            halves = []
            for hh in range(2):
                h = g * (GROUP_W // HEADDIM) + 2 * jj + hh
                seg = a_cs[:, h:h + 1] - a_cs_t[h:h + 1, :]
                decay = jnp.exp2(jnp.where(causal, seg, -jnp.inf))
                halves.append(_dot((cb * decay).astype(BF16), xp))
            y_pairs.append(jnp.where(low_half, halves[0], halves[1]))
        y_g = y_g + jnp.concatenate(y_pairs, axis=1)

        y_ref[:, sl] = _gated_norm(y_g, zs_ref[:, sl], gn_ref[:, sl]).astype(BF16)

    @pl.when(ci == pl.num_programs(0) - 1)
    def _():
        for i in range(D_INNER // LANES):
            hout_ref[i * LANES:(i + 1) * LANES, :] = ht_ref[:, i * LANES:(i + 1) * LANES].T


def _ssd_prompt(xc, rest, dt_raw, dtb, alog, dsk, gn, casts=()):
    nchunk = SEQ // CHUNK
    full = lambda r, c: pl.BlockSpec((r, c), lambda i: (0, 0))
    cast_specs, cast_shapes = _hosted_casts(casts, nchunk)
    return pl.pallas_call(
        functools.partial(_ssd_kernel, n_cast=len(casts)),
        grid=(nchunk,),
        in_specs=[pl.BlockSpec((CHUNK, CONV_DIM), lambda i: (i, 0)),
                  pl.BlockSpec((CHUNK, D_INNER), lambda i: (i, 0)),
                  pl.BlockSpec((CHUNK, LANES), lambda i: (i, 0)),
                  full(1, LANES), full(1, LANES), full(1, D_INNER), full(1, D_INNER)] + cast_specs,
        out_specs=[pl.BlockSpec((CHUNK, D_INNER), lambda i: (i, 0)),
                   pl.BlockSpec((D_INNER, D_STATE), lambda i: (0, 0))] + cast_specs,
        out_shape=[jax.ShapeDtypeStruct((SEQ, D_INNER), BF16),
                   jax.ShapeDtypeStruct((D_INNER, D_STATE), F32)] + cast_shapes,
        scratch_shapes=[pltpu.VMEM((D_STATE, D_INNER), F32)],
        compiler_params=_cparams("arbitrary"),
        name="ssd_prompt",
    )(xc, rest, dt_raw, dtb, alog, dsk, gn, *casts)


def _attn_kernel(*refs, n_cast):
    sink_ref, q_ref, kp_ref, kc_ref, vp_ref, vc_ref = refs[:6]
    o_ref = refs[6 + n_cast]
    _run_hosted_casts(refs[6:6 + n_cast], refs[7 + n_cast:7 + 2 * n_cast])
    i = pl.program_id(0)
    band = 2 * CHUNK
    l_idx = lax.broadcasted_iota(jnp.int32, (CHUNK, band), 0)
    s_idx = lax.broadcasted_iota(jnp.int32, (CHUNK, band), 1)
    first_key = jnp.where(i > 0, 0, CHUNK)
    mask = (s_idx >= jnp.maximum(l_idx, first_key)) & (s_idx <= l_idx + WINDOW)
    low_q = lax.broadcasted_iota(jnp.int32, (CHUNK, LANES), 1) < HEADDIM
    low_kv = lax.broadcasted_iota(jnp.int32, (band, LANES), 1) < HEADDIM

    for c in range(KV_DIM // LANES):
        csl = slice(c * LANES, (c + 1) * LANES)
        k2 = jnp.concatenate([kp_ref[:, csl], kc_ref[:, csl]], axis=0)
        v2 = jnp.concatenate([vp_ref[:, csl], vc_ref[:, csl]], axis=0)
        k2r = pltpu.roll(k2, HEADDIM, axis=1)
        v2r = pltpu.roll(v2, HEADDIM, axis=1)
        for u in range(2):
            kv_head = 2 * c + u
            kd = (jnp.where(low_kv, k2, k2r) if u == 0 else jnp.where(low_kv, k2r, k2)).astype(BF16)
            vd = (jnp.where(low_kv, v2, v2r) if u == 0 else jnp.where(low_kv, v2r, v2)).astype(BF16)
            for jj in range(Q_PER_KV // 2):
                j = kv_head * (Q_PER_KV // 2) + jj
                q2 = q_ref[:, j * LANES:(j + 1) * LANES] * (HEADDIM ** -0.5)
                halves = []
                for hh in range(2):
                    qm = jnp.where(low_q if hh == 0 else jnp.logical_not(low_q), q2, 0.0)
                    s = jnp.where(mask, _dot_nt(qm.astype(BF16), kd), -jnp.inf)
                    sink = sink_ref[2 * j + hh]
                    m = jnp.maximum(jnp.max(s, axis=-1, keepdims=True), sink)
                    p = jnp.exp(s - m)
                    den = jnp.sum(p, axis=-1, keepdims=True) + jnp.exp(sink - m)
                    halves.append(_dot(p.astype(BF16), vd) / den)
                o_ref[:, j * LANES:(j + 1) * LANES] = jnp.where(low_q, halves[0], halves[1]).astype(BF16)


def _attn_prompt(proj, col0, sinks, casts=()):
    nb = SEQ // CHUNK
    prev = lambda i: jnp.maximum(i - 1, 0)
    qb, kb, vb = (OFF_Q - col0) // D_MODEL, (OFF_K - col0) // KV_DIM, (OFF_V - col0) // KV_DIM
    cast_specs, cast_shapes = _hosted_casts(casts, nb)
    return pl.pallas_call(
        functools.partial(_attn_kernel, n_cast=len(casts)),
        grid=(nb,),
        in_specs=[pl.BlockSpec(memory_space=pltpu.SMEM),
                  pl.BlockSpec((CHUNK, D_MODEL), lambda i: (i, qb)),
                  pl.BlockSpec((CHUNK, KV_DIM), lambda i: (prev(i), kb)),
                  pl.BlockSpec((CHUNK, KV_DIM), lambda i: (i, kb)),
                  pl.BlockSpec((CHUNK, KV_DIM), lambda i: (prev(i), vb)),
                  pl.BlockSpec((CHUNK, KV_DIM), lambda i: (i, vb))] + cast_specs,
        out_specs=[pl.BlockSpec((CHUNK, D_MODEL), lambda i: (i, 0))] + cast_specs,
        out_shape=[jax.ShapeDtypeStruct((SEQ, D_MODEL), BF16)] + cast_shapes,
        compiler_params=_cparams("parallel"),
        name="attn_prompt",
    )(sinks, proj, proj, proj, proj, proj, *casts)


def _alias_prev(prev, n_in, first_out):
    if prev is None:
        return [], {}, ()
    specs = [pl.BlockSpec(memory_space=pl.ANY)] * len(prev)
    return specs, {n_in + t: first_out + t for t in range(len(prev))}, tuple(prev)


def _conv_step_kernel(x_ref, dt_ref, cs_ref, cw_ref, cb_ref, dtb_ref, alog_ref, *rest):
    xc_ref, csn_ref, dto_ref, dao_ref = rest[-4:]
    taps = [cs_ref[k] for k in range(CONV_W - 1)] + [x_ref[...]]
    acc = cb_ref[...]
    for k in range(CONV_W):
        acc = acc + taps[k] * cw_ref[k:k + 1, :]
    xc_ref[...] = _silu(acc)
    for k in range(CONV_W - 1):
        csn_ref[k] = taps[k + 1]
    dt = _softplus(dt_ref[...] + dtb_ref[...])
    dto_ref[...] = dt
    dao_ref[...] = jnp.exp(dt * (-jnp.exp(alog_ref[...])))


def _conv_step(proj, conv_state, layer, cw, cb, dtb, alog, prev):
    nb = DEC_BATCH
    small = pl.BlockSpec((1, LANES), lambda s: (0, 0))
    head = pl.BlockSpec((nb, LANES), lambda s: (0, 0))
    state = pl.BlockSpec((None, CONV_W - 1, nb, GROUP_W), lambda s: (layer, 0, 0, s))
    in_specs = [pl.BlockSpec((nb, GROUP_W), lambda s: (0, OFF_X // GROUP_W + s)),
                pl.BlockSpec((nb, LANES), lambda s: (0, OFF_DT // LANES)),
                state,
                pl.BlockSpec((CONV_W, GROUP_W), lambda s: (0, s)),
                pl.BlockSpec((1, GROUP_W), lambda s: (0, s)),
                small, small]
    alias_specs, aliases, alias_args = _alias_prev(prev, len(in_specs), 1)
    return pl.pallas_call(
        _conv_step_kernel,
        grid=(CONV_DIM // GROUP_W,),
        in_specs=in_specs + alias_specs,
        out_specs=[pl.BlockSpec((nb, GROUP_W), lambda s: (0, s)), state, head, head],
        out_shape=[jax.ShapeDtypeStruct((nb, CONV_DIM), F32),
                   jax.ShapeDtypeStruct(conv_state.shape, F32),
                   jax.ShapeDtypeStruct((nb, LANES), F32),
                   jax.ShapeDtypeStruct((nb, LANES), F32)],
        input_output_aliases=aliases,
        compiler_params=_cparams("arbitrary"),
        name="conv_step",
    )(proj, proj, conv_state, cw, cb, dtb, alog, *alias_args)


STEP_B = 8
STEP_GROUPS = 4
STEP_W = STEP_GROUPS * GROUP_W


def _ssm_step_kernel(xs_ref, b_ref, c_ref, z_ref, dt_ref, da_ref, dsk_ref, gn_ref, h_ref, *rest):
    y_ref, hn_ref = rest[-2:]
    part = pl.program_id(1)
    xs = xs_ref[...]
    hrow = lax.broadcasted_iota(jnp.int32, (LANES, STEP_W), 0)
    hcol = lax.broadcasted_iota(jnp.int32, (LANES, STEP_W), 1)
    sel = jnp.where(hrow == part * (STEP_W // HEADDIM) + _div_pow2(hcol, HEADDIM), 1.0, 0.0).astype(BF16)
    dt_e = _dot3_r(dt_ref[...], sel)
    da_e = _dot3_r(da_ref[...], sel)
    xdt = xs * dt_e
    lane_group = _div_pow2(lax.broadcasted_iota(jnp.int32, (STEP_B, STEP_W), 1), GROUP_W)

    pieces = [jnp.where(lane_group == g, xdt, 0.0) for g in range(STEP_GROUPS)]
    pieces += [p.astype(F32) for p in _split3(da_e)]
    n_used = len(pieces) * STEP_B
    pieces.append(jnp.zeros((LANES - n_used, STEP_W), F32))
    ut = jnp.concatenate(pieces, axis=0)
    u_big = jnp.concatenate([ut[:, i * LANES:(i + 1) * LANES].T for i in range(STEP_W // LANES)],
                            axis=0).astype(BF16)

    b_rows = jnp.concatenate([b_ref[:, g * D_STATE:(g + 1) * D_STATE] for g in range(STEP_GROUPS)]
                             + [jnp.zeros((LANES - STEP_GROUPS * STEP_B, D_STATE), F32)], axis=0)
    rrow = lax.broadcasted_iota(jnp.int32, (LANES, D_STATE), 0)
    ones_rows = jnp.where((rrow >= STEP_GROUPS * STEP_B) & (rrow < n_used), 1.0, 0.0)
    big_r = jnp.concatenate([b_rows, ones_rows], axis=1)
    r_seq = _mod_pow2(lax.broadcasted_iota(jnp.int32, (LANES, 2 * D_STATE), 0), STEP_B)

    cc = jnp.concatenate([c_ref[:, g * D_STATE:(g + 1) * D_STATE] for g in range(STEP_GROUPS)],
                         axis=0).astype(BF16)
    crow = lax.broadcasted_iota(jnp.int32, (STEP_GROUPS * STEP_B, STEP_W), 0)
    ccol_group = _div_pow2(lax.broadcasted_iota(jnp.int32, (STEP_GROUPS * STEP_B, STEP_W), 1), GROUP_W)
    sub = lax.broadcasted_iota(jnp.int32, (STEP_B, STEP_W), 0)

    y_off = jnp.zeros((STEP_B, STEP_W), F32)
    for b in range(STEP_B):
        h0 = h_ref[b]
        res = _dot(u_big, jnp.where(r_seq == b, big_r, 0.0).astype(BF16))
        hn_ref[b] = res[:, D_STATE:] * h0 + res[:, :D_STATE]
        y_all = _dot_nt(cc, h0.astype(BF16))
        pick = (crow == ccol_group * STEP_B + b)
        y_b = jnp.sum(jnp.where(pick, y_all, 0.0), axis=0, keepdims=True)
        y_off = jnp.where(sub == b, y_b, y_off)

    cb_e = jnp.zeros((STEP_B, STEP_W), F32)
    for g in range(STEP_GROUPS):
        gsl = slice(g * D_STATE, (g + 1) * D_STATE)
        cb_g = jnp.sum(c_ref[:, gsl] * b_ref[:, gsl], axis=-1, keepdims=True)
        cb_e = jnp.where(lane_group == g, cb_g, cb_e)

    y = da_e * y_off + cb_e * xdt + dsk_ref[...] * xs
    for g in range(STEP_GROUPS):
        sl = slice(g * GROUP_W, (g + 1) * GROUP_W)
        y_ref[:, sl] = _gated_norm(y[:, sl], _silu(z_ref[:, sl]), gn_ref[:, sl])


def _ssm_step(xc, proj, dt, da, dsk, gn, state, layer, prev):
    nparts = D_INNER // STEP_W
    state_spec = pl.BlockSpec((None, STEP_B, STEP_W, D_STATE), lambda i, p: (layer, i, p, 0))
    in_specs = [pl.BlockSpec((STEP_B, STEP_W), lambda i, p: (i, p)),
                pl.BlockSpec((STEP_B, STEP_GROUPS * D_STATE),
                             lambda i, p: (i, D_INNER // (STEP_GROUPS * D_STATE) + p)),
                pl.BlockSpec((STEP_B, STEP_GROUPS * D_STATE),
                             lambda i, p: (i, (D_INNER + BC_W) // (STEP_GROUPS * D_STATE) + p)),
                pl.BlockSpec((STEP_B, STEP_W), lambda i, p: (i, OFF_Z // STEP_W + p)),
                pl.BlockSpec((STEP_B, LANES), lambda i, p: (i, 0)),
                pl.BlockSpec((STEP_B, LANES), lambda i, p: (i, 0)),
                pl.BlockSpec((1, STEP_W), lambda i, p: (0, p)),
                pl.BlockSpec((1, STEP_W), lambda i, p: (0, p)),
                state_spec]
    alias_specs, aliases, alias_args = _alias_prev(prev, len(in_specs), 1)
    return pl.pallas_call(
        _ssm_step_kernel,
        grid=(DEC_BATCH // STEP_B, nparts),
        in_specs=in_specs + alias_specs,
        out_specs=[pl.BlockSpec((STEP_B, STEP_W), lambda i, p: (i, p)), state_spec],
        out_shape=[jax.ShapeDtypeStruct((DEC_BATCH, D_INNER), F32),
                   jax.ShapeDtypeStruct(state.shape, F32)],
        input_output_aliases=aliases,
        compiler_params=_cparams("parallel", "parallel"),
        name="ssm_step",
    )(xc, xc, xc, proj, dt, da, dsk, gn, state, *alias_args)


ATTN_B = 8


def _attn_step_kernel(sink_ref, q_ref, kn_ref, vn_ref, kc_ref, vc_ref, *rest):
    o_ref, ko_ref, vo_ref = rest[-3:]
    lane_kv = _div_pow2(lax.broadcasted_iota(jnp.int32, (N_KV, KV_DIM), 1), HEADDIM)
    own = lane_kv == lax.broadcasted_iota(jnp.int32, (N_KV, KV_DIM), 0)
    newest = lax.broadcasted_iota(jnp.int32, (KV_DIM, WINDOW), 1) == WINDOW - 1
    sink = sink_ref[...]
    pad = jnp.zeros((LANES - ATTN_B, KV_DIM), F32)
    kn_t = jnp.concatenate([kn_ref[...], pad], axis=0).T
    vn_t = jnp.concatenate([vn_ref[...], pad], axis=0).T
    for b in range(ATTN_B):
        qrow = q_ref[b:b + 1, :] * (HEADDIM ** -0.5)
        qm = jnp.concatenate(
            [jnp.where(own, jnp.broadcast_to(qrow[:, r * KV_DIM:(r + 1) * KV_DIM], (N_KV, KV_DIM)), 0.0)
             for r in range(Q_PER_KV)], axis=0)
        k_new = kn_ref[b:b + 1, :]
        v_new = vn_ref[b:b + 1, :]
        kt = kc_ref[b]
        vt = vc_ref[b]
        s = _dot(qm.astype(BF16), kt.astype(BF16))
        s_new = jnp.sum(qm * k_new, axis=-1, keepdims=True)
        m = jnp.maximum(jnp.maximum(jnp.max(s, axis=-1, keepdims=True), s_new), sink)
        p = jnp.exp(s - m)
        p_new = jnp.exp(s_new - m)
        den = jnp.sum(p, axis=-1, keepdims=True) + p_new + jnp.exp(sink - m)
        o_full = _dot_nt((p / den).astype(BF16), vt.astype(BF16)) + (p_new / den) * v_new
        parts = []
        for r in range(Q_PER_KV):
            o_r = jnp.where(own, o_full[r * N_KV:(r + 1) * N_KV, :], 0.0)
            parts.append(jnp.sum(o_r, axis=0, keepdims=True))
        o_ref[b:b + 1, :] = jnp.concatenate(parts, axis=1)
        ko_ref[b] = jnp.where(newest, jnp.broadcast_to(kn_t[:, b:b + 1], (KV_DIM, WINDOW)),
                              pltpu.roll(kt, WINDOW - 1, axis=1))
        vo_ref[b] = jnp.where(newest, jnp.broadcast_to(vn_t[:, b:b + 1], (KV_DIM, WINDOW)),
                              pltpu.roll(vt, WINDOW - 1, axis=1))


def _attn_step(q, proj, sinks, cache_k, cache_v, layer, prev):
    cache_spec = pl.BlockSpec((None, ATTN_B, KV_DIM, WINDOW), lambda i: (layer, i, 0, 0))
    in_specs = [pl.BlockSpec((N_HEADS, 1), lambda i: (0, 0)),
                pl.BlockSpec((ATTN_B, D_MODEL), lambda i: (i, 0)),
                pl.BlockSpec((ATTN_B, KV_DIM), lambda i: (i, OFF_K // KV_DIM)),
                pl.BlockSpec((ATTN_B, KV_DIM), lambda i: (i, OFF_V // KV_DIM)),
                cache_spec, cache_spec]
    alias_specs, aliases, alias_args = _alias_prev(prev, len(in_specs), 1)
    return pl.pallas_call(
        _attn_step_kernel,
        grid=(DEC_BATCH // ATTN_B,),
        in_specs=in_specs + alias_specs,
        out_specs=[pl.BlockSpec((ATTN_B, D_MODEL), lambda i: (i, 0)), cache_spec, cache_spec],
        out_shape=[jax.ShapeDtypeStruct((DEC_BATCH, D_MODEL), F32),
                   jax.ShapeDtypeStruct(cache_k.shape, F32),
                   jax.ShapeDtypeStruct(cache_v.shape, F32)],
        input_output_aliases=aliases,
        compiler_params=_cparams("parallel"),
        name="attn_step",
    )(sinks, q, proj, proj, cache_k, cache_v, *alias_args)


SRC_X = D_INNER
SRC_DT = SRC_X + CONV_DIM
W_TILE = 512
W_TILE_HOSTED = 256


def _w_tile_src_row(t, tile_w):
    per_tile = tile_w // N_SSM_HEADS
    unit = jnp.where(
        t < OFF_Z // tile_w, SRC_X // N_SSM_HEADS + t * per_tile,
        jnp.where(t < OFF_Q // tile_w, (t - OFF_Z // tile_w) * per_tile,
                  jnp.where(t < OFF_DT // tile_w, t * per_tile + 1, SRC_DT // N_SSM_HEADS)))
    return unit * N_SSM_HEADS


def _w_tile_relayout(x, t, tile_w):
    first_pad = OFF_DT // tile_w
    valid = jnp.where(t < first_pad, tile_w, jnp.where(t == first_pad, N_SSM_HEADS, 0))
    row = lax.broadcasted_iota(jnp.int32, x.shape, 0)
    return jnp.where(row < valid, x, 0.0).T.astype(BF16)


def _w_tile_specs(layer, tile_w, tile_of):
    src = pl.BlockSpec((pl.Element(1), pl.Element(tile_w), pl.Element(D_MODEL)),
                       lambda *ids: (layer, _w_tile_src_row(tile_of(*ids), tile_w), 0))
    dst = pl.BlockSpec((None, D_MODEL, tile_w), lambda *ids: (0, 0, tile_of(*ids)))
    return src, dst


def _w_in_layout_kernel(x_ref, o_ref):
    o_ref[...] = _w_tile_relayout(x_ref[0], pl.program_id(0), W_TILE)


def _w_in_layout(w_t, layer):
    src, dst = _w_tile_specs(layer, W_TILE, lambda t: t)
    return pl.pallas_call(
        _w_in_layout_kernel,
        grid=(PROJ_W // W_TILE,),
        in_specs=[src],
        out_specs=dst,
        out_shape=jax.ShapeDtypeStruct((1, D_MODEL, PROJ_W), BF16),
        compiler_params=_cparams("parallel"),
        name="w_in_layout",
    )(w_t)


def _small_params(l, norm_mix, conv_w, conv_b, dt_bias, a_log, d_skip, ssm_norm, attn_sinks, norm_ffn):
    pad = lambda v: jnp.pad(v, (0, LANES - N_SSM_HEADS)).reshape(1, LANES)
    return dict(
        norm_mix=norm_mix[l].reshape(1, D_MODEL),
        conv_w=conv_w[l], conv_b=conv_b[l].reshape(1, CONV_DIM),
        dt_bias=pad(dt_bias[l]), a_log=pad(a_log[l]),
        d_skip=jnp.repeat(d_skip[l], HEADDIM).reshape(1, D_INNER),
        ssm_norm=ssm_norm[l].reshape(1, D_INNER),
        sinks=attn_sinks[l],
        sinks_rm=attn_sinks[l].reshape(N_KV, Q_PER_KV).T.reshape(N_HEADS, 1),
        norm_ffn=norm_ffn[l].reshape(1, D_MODEL))


def _dense_tail(x, y, o, proj, col0, p, w, l, tm, norm_final, next_gain=None):
    merged = _merge(y, o, proj, col0, w["w_ssm"], w["w_attn"], l, tm, 512)
    x1, xn = _out_proj(merged, w["w_out"], l, x, p["norm_ffn"], min(tm, 512))
    hmid = _ffn_up(xn, w["w_gate"], w["w_up"], l, tm, 512)
    if l == DEPTH - 1:
        norm, gain = "final", norm_final
    elif next_gain is not None:
        norm, gain = "next", next_gain
    else:
        norm, gain = "plain", norm_final
    out = _ffn_down(hmid, w["w_down"], l, x1, gain, norm, min(tm, 512))
    return out[0], (out[1] if norm == "next" else None)


PROMPT_TM = 1024
PROJ_TN = 1280


def kernel(x_prompt, x_sample, state_ssm, state_conv, cache_k, cache_v, norm_mix, w_in, conv_w,
           conv_b, dt_bias, a_log, d_skip, ssm_norm, w_ssm_proj, w_attn_proj, attn_sinks, w_out,
           norm_ffn, w_gate, w_up, w_down, norm_final):
    xp = x_prompt.reshape(SEQ, D_MODEL)
    xs = x_sample.reshape(DEC_BATCH, D_MODEL)
    state4 = state_ssm.reshape(DEPTH, DEC_BATCH, D_INNER, D_STATE)
    to_kt = lambda c: c.transpose(0, 1, 3, 4, 2).reshape(DEPTH, DEC_BATCH, KV_DIM, WINDOW)
    ck, cv = to_kt(cache_k), to_kt(cache_v)
    conv_t = state_conv.transpose(0, 2, 1, 3)
    outs = {k: [] for k in ("p_ssm", "p_conv", "p_k", "p_v")}

    w_in_t = w_in.transpose(0, 2, 1)
    w = {}
    w_in_l = _w_in_layout(w_in_t, 0)
    rows =lambda t: t.reshape(-1, t.shape[-1])
    ssd_casts = dict(w_ssm=w_ssm_proj, w_attn=w_attn_proj, w_out=w_out, w_down=w_down)
    attn_casts = dict(w_gate=w_gate, w_up=w_up)

    g_final = norm_final.reshape(1, D_MODEL)
    h_all = conv_all = kv_all = xp_n = None
    for l in range(DEPTH):
        p = _small_params(l, norm_mix, conv_w, conv_b, dt_bias, a_log, d_skip, ssm_norm,
                          attn_sinks, norm_ffn)

        xp_in = xp if xp_n is None else xp_n
        xc_p, dt_p, tails = _in_proj_conv(xp_in, p["norm_mix"], w_in_l, 0, p["conv_w"], p["conv_b"],
                                          PROMPT_TM, 1024)
        if l + 1 < DEPTH:
            rest, w_in_next = _in_proj_rest(xp_in, p["norm_mix"], w_in_l, 0, PROMPT_TM, 1024, w_in_t, l + 1)
        else:
            (rest,), w_in_next = _in_proj_rest(xp_in, p["norm_mix"], w_in_l, 0, PROMPT_TM, 1024), None
        hosted = [ssd_casts, attn_casts] if l == 0 else [{}, {}]
        y, h_p, *cast_a = _ssd_prompt(xc_p, rest, dt_p, p["dt_bias"], p["a_log"], p["d_skip"],
                                      p["ssm_norm"], [rows(t) for t in hosted[0].values()])
        o, *cast_b = _attn_prompt(rest, OFF_Z, p["sinks"], [rows(t) for t in hosted[1].values()])
        for group, done in zip(hosted, (cast_a, cast_b)):
            w.update({k: c.reshape(t.shape) for (k, t), c in zip(group.items(), done)})
        last_kv = lambda off: rest[SEQ - WINDOW:, off - OFF_Z:off - OFF_Z + KV_DIM].reshape(
            1, WINDOW, N_KV, HEADDIM)
        outs["p_ssm"].append(h_p.reshape(1, N_SSM_HEADS, HEADDIM, D_STATE))
        outs["p_conv"].append(tails[-1, SUBLANES - (CONV_W - 1):].reshape(1, CONV_W - 1, CONV_DIM))
        outs["p_k"].append(last_kv(OFF_K))
        outs["p_v"].append(last_kv(OFF_V))
        next_gain = norm_mix[l + 1].reshape(1, D_MODEL) if l + 1 < DEPTH else None
        xp, xp_n = _dense_tail(xp, y, o, rest, OFF_Z, p, w, l, PROMPT_TM, g_final, next_gain)

        proj_s = _in_proj(xs, p["norm_mix"], w_in_l, 0, DEC_BATCH, PROJ_TN)
        xc, conv_all, dt, da = _conv_step(proj_s, conv_t, l, p["conv_w"], p["conv_b"],
                                          p["dt_bias"], p["a_log"],
                                          None if conv_all is None else (conv_all,))
        y_s, h_all = _ssm_step(xc, proj_s, dt, da, p["d_skip"], p["ssm_norm"], state4, l,
                               None if h_all is None else (h_all,))
        swap_rm = lambda t, a, b: t.reshape(DEC_BATCH, a, b, HEADDIM).transpose(0, 2, 1, 3).reshape(
            DEC_BATCH, D_MODEL)
        q_s = swap_rm(proj_s[:, OFF_Q:OFF_Q + D_MODEL], N_KV, Q_PER_KV)
        o_s, k_all, v_all = _attn_step(q_s, proj_s, p["sinks_rm"], ck, cv, l, kv_all)
        o_s = swap_rm(o_s, Q_PER_KV, N_KV)
        kv_all = (k_all, v_all)
        xs, _ = _dense_tail(xs, y_s, o_s, proj_s, 0, p, w, l, DEC_BATCH, g_final)
        w_in_l = w_in_next

    y_prompt = xp.reshape(1, SEQ, D_MODEL)
    y_sample = xs.reshape(DEC_BATCH, 1, D_MODEL)
    st = lambda k: jnp.stack(outs[k])
    from_kt = lambda c: c.reshape(DEPTH, DEC_BATCH, N_KV, HEADDIM, WINDOW).transpose(0, 1, 4, 2, 3)
    return (y_prompt, y_sample, st("p_ssm"), st("p_conv"), st("p_k"), st("p_v"),
            h_all.reshape(DEPTH, DEC_BATCH, N_SSM_HEADS, HEADDIM, D_STATE),
            conv_all.transpose(0, 2, 1, 3), from_kt(kv_all[0]), from_kt(kv_all[1]))
```
